```python
import jax
import jax.numpy as jnp
from jax import lax
import numpy as np

D_MODEL = 1024
BATCH = 4
SEQ = 4096
DEPTH = 4

GRID_W = 64
CTX_LEN = 256
N_MIXERS = 4
MIX_RET, MIX_NAT, MIX_POOL, MIX_SWA = 0, 1, 2, 3
N_RET = len(range(MIX_RET, DEPTH, N_MIXERS))
N_NAT = len(range(MIX_NAT, DEPTH, N_MIXERS))
N_POOL = len(range(MIX_POOL, DEPTH, N_MIXERS))
N_SWA = len(range(MIX_SWA, DEPTH, N_MIXERS))
EPS = 1e-6
NEG_INF = -1e30
ROPE_BASE = 10000.0
FFN_HIDDEN = 2816
RET_HEADS = 4
RET_QK_DIM = D_MODEL // RET_HEADS
RET_V_DIM = 2 * RET_QK_DIM
RET_CHUNK = 128
NAT_HEADS = 16
NAT_HEAD_DIM = D_MODEL // NAT_HEADS
NAT_KH = 8
NAT_KW = 16
POOL_WINDOWS = (2, 4, 8, 16)
POOL_GROUPS = len(POOL_WINDOWS)
POOL_GROUP_DIM = D_MODEL // POOL_GROUPS
SWA_Q_HEADS = 16
SWA_KV_HEADS = 4
SWA_HEAD_DIM = D_MODEL // SWA_Q_HEADS
SWA_WINDOW = 128
SWA_BLOCK = 128

kernel_name = 'hybrid_interleaved_diffusion_trunk'


def rms_norm(x, g):
    xf = x.astype(jnp.float32)
    y = xf * lax.rsqrt(jnp.mean(xf * xf, axis=-1, keepdims=True) + EPS)
    return (y * g.astype(jnp.float32)).astype(x.dtype)


def ada_in(h, g, m, j):
    xn = rms_norm(h, g)
    return (xn * (1.0 + m[:, j, 1][:, None]) + m[:, j, 0][:, None]).astype(h.dtype)


def gated_residual(h, m, j, y, w):
    return h + (w * m[:, j, 2][:, None] * y).astype(h.dtype)


def swiglu(x, w_in, w_out):
    a, b = jnp.split(x @ w_in, 2, axis=-1)
    return (jax.nn.silu(a) * b) @ w_out


def rope_angles(positions, dim):
    seg = dim // len(positions)
    inv = ROPE_BASE ** (-jnp.arange(0, seg, 2, dtype=jnp.float32) / seg)
    return jnp.concatenate([jnp.tile(p.astype(jnp.float32)[:, None] * inv, (1, 2)) for p in positions], axis=-1)


def apply_rope(x, ang, n_axes):
    xf = x.astype(jnp.float32)
    segs = jnp.split(xf, 2 * n_axes, axis=-1)
    rot = jnp.concatenate([s for a in range(n_axes) for s in (-segs[2 * a + 1], segs[2 * a])], axis=-1)
    return (xf * jnp.cos(ang)[None, :, None] + rot * jnp.sin(ang)[None, :, None]).astype(x.dtype)


def retention_scan(q, k, v, log_gamma, s0, include_diag):
    B, T, H, _ = q.shape
    dv = v.shape[-1]
    C = RET_CHUNK
    N = T // C
    idx = jnp.arange(C, dtype=jnp.float32)
    diff = idx[:, None] - idx[None, :]
    keep = (diff >= 0) if include_diag else (diff > 0)
    intra = jnp.where(keep[None], jnp.exp(jnp.maximum(diff, 0.0)[None] * log_gamma[:, None, None]), 0.0)
    q_dec = jnp.exp((idx + 1.0)[None] * log_gamma[:, None])
    k_dec = jnp.exp((C - 1.0 - idx)[None] * log_gamma[:, None])
    c_dec = jnp.exp(C * log_gamma)[:, None, None]

    def chunks(a):
        return a.reshape(B, N, C, H, a.shape[-1]).transpose(1, 0, 3, 2, 4)

    def step(s, qkv):
        qn, kn, vn = qkv
        att = jnp.einsum('bhid,bhjd->bhij', qn, kn) * intra
        o = jnp.einsum('bhij,bhjv->bhiv', att, vn) + jnp.einsum('bhid,bhdv->bhiv', qn * q_dec[..., None], s)
        s = s * c_dec + jnp.einsum('bhjd,bhjv->bhdv', kn * k_dec[..., None], vn)
        return s, o

    s_fin, o = lax.scan(step, s0, (chunks(q), chunks(k), chunks(v)))
    return o.transpose(1, 0, 3, 2, 4).reshape(B, T, H, dv), s_fin


def retention_bidir(q, k, v, lg_f, lg_b, s0_f, s0_b):
    o_f, s_f = retention_scan(q, k, v, lg_f, s0_f, True)
    rev = lambda a: jnp.flip(a, axis=1)
    o_b, s_b = retention_scan(rev(q), rev(k), rev(v), lg_b, s0_b, False)
    return o_f + rev(o_b), s_f, s_b


def retention_mixer(xl, xc, w_in, w_out, gn_g, decay_f, decay_b, ctx_out):
    H, DK, DV = RET_HEADS, RET_QK_DIM, RET_V_DIM
    lg_f = jax.nn.log_sigmoid(decay_f.astype(jnp.float32))
    lg_b = jax.nn.log_sigmoid(decay_b.astype(jnp.float32))

    def project(x, ang):
        B, T, _ = x.shape
        q, k, v, g = jnp.split(x @ w_in, [H * DK, 2 * H * DK, 2 * H * DK + H * DV], axis=-1)
        q = q.reshape(B, T, H, DK)
        k = k.reshape(B, T, H, DK) * DK ** -0.5
        v = v.reshape(B, T, H, DV)
        if ang is not None:
            q = apply_rope(q, ang, 1)
            k = apply_rope(k, ang, 1)
        return q.astype(jnp.float32), k.astype(jnp.float32), v.astype(jnp.float32), g

    def read_out(o, g):
        mu = jnp.mean(o, axis=-1, keepdims=True)
        var = jnp.mean(jnp.square(o - mu), axis=-1, keepdims=True)
        on = (o - mu) * lax.rsqrt(var + EPS) * gn_g.astype(jnp.float32).reshape(H, DV)
        B, T = o.shape[:2]
        return (jax.nn.silu(g) * on.reshape(B, T, H * DV).astype(g.dtype)) @ w_out

    qc, kc, vc, gc = project(xc, None)
    Bc, L = xc.shape[:2]
    if ctx_out:
        zeros = jnp.zeros((Bc, H, DK, DV), jnp.float32)
        oc, s_f, s_b = retention_bidir(qc, kc, vc, lg_f, lg_b, zeros, zeros)
        yc = read_out(oc, gc)
    else:
        pos = jnp.arange(L, dtype=jnp.float32)[:, None]
        s_f = jnp.einsum('bthd,bthv->bhdv', kc * jnp.exp((L - 1.0 - pos) * lg_f)[None, :, :, None], vc)
        s_b = jnp.einsum('bthd,bthv->bhdv', kc * jnp.exp(pos * lg_b)[None, :, :, None], vc)
        yc = None
    T = xl.shape[1]
    ang = rope_angles([jnp.arange(T)], DK)
    ql, kl, vl, gl = project(xl, ang)
    ol, _, _ = retention_bidir(ql, kl, vl, lg_f, lg_b, s_f, s_b)
    return read_out(ol, gl), yc


def nat_mixer(xl, xc, w_qkv, w_o, rpb, ctx_out):
    H, DH = NAT_HEADS, NAT_HEAD_DIM

    def project(x):
        B, T, _ = x.shape
        q, k, v = jnp.split(x @ w_qkv, 3, axis=-1)
        return (q * DH ** -0.5).reshape(B, T, H, DH), k.reshape(B, T, H, DH), v.reshape(B, T, H, DH)

    qc, kc, vc = project(xc)
    ql, kl, vl = project(xl)
    B, T, _ = xl.shape
    rows = T // GRID_W
    kh = min(NAT_KH, rows)
    nk = kh * NAT_KW
    qg = ql.reshape(B, rows, GRID_W, H, DH)
    kg = kl.reshape(B, rows, GRID_W, H, DH)
    vg = vl.reshape(B, rows, GRID_W, H, DH)
    cols = jnp.arange(GRID_W)
    col_idx = jnp.clip(cols - NAT_KW // 2, 0, GRID_W - NAT_KW)[:, None] + jnp.arange(NAT_KW)
    col_bias_idx = col_idx - cols[:, None] + NAT_KW - 1

    def row_block(r):
        r0 = jnp.clip(r - kh // 2, 0, rows - kh)
        kw = lax.dynamic_slice_in_dim(kg, r0, kh, axis=1)[:, :, col_idx]
        vw = lax.dynamic_slice_in_dim(vg, r0, kh, axis=1)[:, :, col_idx]
        qr = lax.dynamic_index_in_dim(qg, r, axis=1, keepdims=False)
        row_bias_idx = r0 + jnp.arange(kh) - r + NAT_KH - 1
        bias = rpb[:, row_bias_idx][:, :, col_bias_idx].transpose(0, 2, 1, 3)
        s_nb = jnp.einsum('bchd,bacnhd->bhcan', qr, kw).astype(jnp.float32) + bias.astype(jnp.float32)
        s_cx = jnp.einsum('bchd,bjhd->bhcj', qr, kc).astype(jnp.float32)
        p = jax.nn.softmax(jnp.concatenate([s_nb.reshape(B, H, GRID_W, nk), s_cx], axis=-1), axis=-1).astype(vl.dtype)
        p_nb = p[..., :nk].reshape(B, H, GRID_W, kh, NAT_KW)
        return jnp.einsum('bhcan,bacnhd->bchd', p_nb, vw) + jnp.einsum('bhcj,bjhd->bchd', p[..., nk:], vc)

    o = lax.map(row_block, jnp.arange(rows))
    yl = o.transpose(1, 0, 2, 3, 4).reshape(B, T, H * DH) @ w_o
    yc = None
    if ctx_out:
        p = jax.nn.softmax(jnp.einsum('bihd,bjhd->bhij', qc, kc).astype(jnp.float32), axis=-1).astype(vc.dtype)
        oc = jnp.einsum('bhij,bjhd->bihd', p, vc)
        yc = oc.reshape(oc.shape[0], oc.shape[1], H * DH) @ w_o
    return yl, yc


def pool_mixer(x, w_grp, scale):
    B, T, D = x.shape
    G, Dg = POOL_GROUPS, POOL_GROUP_DIM
    xf = x.astype(jnp.float32).reshape(B, T, G, Dg)
    csum = jnp.concatenate([jnp.zeros((B, 1, G, Dg), jnp.float32), jnp.cumsum(xf, axis=1)], axis=1)
    half = jnp.asarray([w // 2 for w in POOL_WINDOWS])
    t = jnp.arange(T)[:, None]
    lo = jnp.clip(t - half, 0, T)
    hi = jnp.clip(t + half, 0, T)
    grp = jnp.arange(G)[None, :]
    mean = (csum[:, hi, grp] - csum[:, lo, grp]) / (hi - lo).astype(jnp.float32)[None, :, :, None]
    pooled = (mean - xf).astype(x.dtype)
    return jnp.einsum('btgc,gcd->btgd', pooled, w_grp).reshape(B, T, D) * scale


def swa_mixer(xl, xc, w_qkv, w_o, sink, ctx_out):
    HQ, HKV, DH = SWA_Q_HEADS, SWA_KV_HEADS, SWA_HEAD_DIM
    G = HQ // HKV

    def project(x, ang):
        B, T, _ = x.shape
        q, k, v = jnp.split(x @ w_qkv, [HQ * DH, (HQ + HKV) * DH], axis=-1)
        q = q.reshape(B, T, HQ, DH)
        k = k.reshape(B, T, HKV, DH)
        v = v.reshape(B, T, HKV, DH)
        if ang is not None:
            q = apply_rope(q, ang, 2)
            k = apply_rope(k, ang, 2)
        return (q * DH ** -0.5).reshape(B, T, HKV, G, DH), k, v

    B, T, _ = xl.shape
    t = jnp.arange(T)
    ang = rope_angles([t // GRID_W, t % GRID_W], DH)
    ql, kl, vl = project(xl, ang)
    qc, kc, vc = project(xc, None)
    sink_l = sink.astype(jnp.float32).reshape(HKV, G)
    nb = T // SWA_BLOCK
    nw = SWA_WINDOW // SWA_BLOCK
    nk = (2 * nw + 1) * SWA_BLOCK

    def band(a):
        ap = jnp.pad(a, ((0, 0), (SWA_WINDOW, SWA_WINDOW), (0, 0), (0, 0))).reshape(B, nb + 2 * nw, SWA_BLOCK, HKV, DH)
        return jnp.concatenate([ap[:, o:o + nb] for o in range(2 * nw + 1)], axis=2)

    kb, vb = band(kl), band(vl)
    qb = ql.reshape(B, nb, SWA_BLOCK, HKV, G, DH)
    qpos = (jnp.arange(nb)[:, None] * SWA_BLOCK + jnp.arange(SWA_BLOCK))[:, :, None]
    kpos = (jnp.arange(nb)[:, None] * SWA_BLOCK - SWA_WINDOW + jnp.arange(nk))[:, None, :]
    allowed = (jnp.abs(qpos - kpos) <= SWA_WINDOW) & (kpos >= 0) & (kpos < T)
    s_loc = jnp.where(allowed, jnp.einsum('bnikgd,bnjkd->bkgnij', qb, kb).astype(jnp.float32), NEG_INF)
    s_cx = jnp.einsum('bnikgd,bjkd->bkgnij', qb, kc).astype(jnp.float32)
    s_sk = jnp.broadcast_to(sink_l[None, :, :, None, None, None], s_loc.shape[:-1] + (1,))
    p = jax.nn.softmax(jnp.concatenate([s_loc, s_cx, s_sk], axis=-1), axis=-1)[..., :-1].astype(vl.dtype)
    o = jnp.einsum('bkgnij,bnjkd->bnikgd', p[..., :nk], vb) + jnp.einsum('bkgnij,bjkd->bnikgd', p[..., nk:], vc)
    yl = o.reshape(B, T, HQ * DH) @ w_o
    yc = None
    if ctx_out:
        s = jnp.einsum('bikgd,bjkd->bkgij', qc, kc).astype(jnp.float32)
        sk = jnp.broadcast_to(sink_l[None, :, :, None, None], s.shape[:-1] + (1,))
        pc = jax.nn.softmax(jnp.concatenate([s, sk], axis=-1), axis=-1)[..., :-1].astype(vc.dtype)
        oc = jnp.einsum('bkgij,bjkd->bikgd', pc, vc)
        yc = oc.reshape(oc.shape[0], oc.shape[1], HQ * DH) @ w_o
    return yl, yc


def setup_inputs(seed: int = 0) -> dict:
    key = jax.random.key(seed)
    ks = jax.random.split(key, 24)
    D, F = D_MODEL, FFN_HIDDEN
    nrm = lambda k, shape, s: jax.random.normal(k, shape, jnp.float32) * s
    ret_in = 2 * RET_HEADS * RET_QK_DIM + 2 * RET_HEADS * RET_V_DIM
    ret_v = RET_HEADS * RET_V_DIM
    decay_logit = jnp.log(2.0 ** (5.0 + jnp.arange(RET_HEADS, dtype=jnp.float32)) - 1.0)
    nat_w = NAT_HEADS * NAT_HEAD_DIM
    swa_in = (SWA_Q_HEADS + 2 * SWA_KV_HEADS) * SWA_HEAD_DIM
    swa_o = SWA_Q_HEADS * SWA_HEAD_DIM
    return {
        'x': nrm(ks[0], (BATCH, SEQ, D), 1.0),
        'c': nrm(ks[1], (BATCH, D), 1.0),
        'ctx': nrm(ks[2], (BATCH, CTX_LEN, D), 1.0),
        'c_ctx': nrm(ks[3], (D,), 1.0),
        'w_mod': nrm(ks[4], (DEPTH, D, 9 * D), 0.5 * D ** -0.5),
        'b_mod': nrm(ks[5], (DEPTH, 9 * D), 0.01),
        'norm_g': 1.0 + nrm(ks[6], (DEPTH, 3, D), 0.02),
        'ffn_w_in': nrm(ks[7], (DEPTH, 2, D, 2 * F), D ** -0.5),
        'ffn_w_out': nrm(ks[8], (DEPTH, 2, F, D), F ** -0.5),
        'ret_w_in': nrm(ks[9], (N_RET, D, ret_in), D ** -0.5),
        'ret_w_out': nrm(ks[10], (N_RET, ret_v, D), ret_v ** -0.5),
        'ret_gn_g': 1.0 + nrm(ks[11], (N_RET, ret_v), 0.02),
        'ret_decay_f': decay_logit + nrm(ks[12], (N_RET, RET_HEADS), 0.05),
        'ret_decay_b': decay_logit + nrm(ks[13], (N_RET, RET_HEADS), 0.05),
        'nat_w_qkv': nrm(ks[14], (N_NAT, D, 3 * nat_w), D ** -0.5),
        'nat_w_o': nrm(ks[15], (N_NAT, nat_w, D), nat_w ** -0.5),
        'nat_rpb': nrm(ks[16], (N_NAT, NAT_HEADS, 2 * NAT_KH - 1, 2 * NAT_KW - 1), 0.1),
        'pool_w': nrm(ks[17], (N_POOL, POOL_GROUPS, POOL_GROUP_DIM, POOL_GROUP_DIM), POOL_GROUP_DIM ** -0.5),
        'pool_scale': 1.0 + nrm(ks[18], (N_POOL, D), 0.02),
        'swa_w_qkv': nrm(ks[19], (N_SWA, D, swa_in), D ** -0.5),
        'swa_w_o': nrm(ks[20], (N_SWA, swa_o, D), swa_o ** -0.5),
        'swa_sink': nrm(ks[21], (N_SWA, SWA_Q_HEADS), 0.5),
        'final_norm_g': 1.0 + nrm(ks[22], (D,), 0.02),
    }


def reference(x, c, ctx, c_ctx, w_mod, b_mod, norm_g, ffn_w_in, ffn_w_out,
              ret_w_in, ret_w_out, ret_gn_g, ret_decay_f, ret_decay_b,
              nat_w_qkv, nat_w_o, nat_rpb, pool_w, pool_scale,
              swa_w_qkv, swa_w_o, swa_sink, final_norm_g):
    D = x.shape[-1]
    h, hc = x, ctx
    s_c = jax.nn.silu(c)
    s_cc = jax.nn.silu(c_ctx)[None]
    for i in range(DEPTH):
        kind, occ = i % N_MIXERS, i // N_MIXERS
        last = i == DEPTH - 1
        ctx_live = (not last) or kind != MIX_POOL
        ml = (s_c @ w_mod[i] + b_mod[i]).reshape(-1, 3, 3, D)
        h = gated_residual(h, ml, 0, swiglu(ada_in(h, norm_g[i, 0], ml, 0), ffn_w_in[i, 0], ffn_w_out[i, 0]), 0.5)
        xl = ada_in(h, norm_g[i, 1], ml, 1)
        xc = None
        if ctx_live:
            mc = (s_cc @ w_mod[i] + b_mod[i]).reshape(-1, 3, 3, D)
            hc = gated_residual(hc, mc, 0, swiglu(ada_in(hc, norm_g[i, 0], mc, 0), ffn_w_in[i, 0], ffn_w_out[i, 0]), 0.5)
            xc = ada_in(hc, norm_g[i, 1], mc, 1)
        if kind == MIX_RET:
            yl, yc = retention_mixer(xl, xc, ret_w_in[occ], ret_w_out[occ], ret_gn_g[occ],
                                     ret_decay_f[occ], ret_decay_b[occ], not last)
        elif kind == MIX_NAT:
            yl, yc = nat_mixer(xl, xc, nat_w_qkv[occ], nat_w_o[occ], nat_rpb[occ], not last)
        elif kind == MIX_POOL:
            yl = pool_mixer(xl, pool_w[occ], pool_scale[occ])
            yc = None if last else pool_mixer(xc, pool_w[occ], pool_scale[occ])
        else:
            yl, yc = swa_mixer(xl, xc, swa_w_qkv[occ], swa_w_o[occ], swa_sink[occ], not last)
        h = gated_residual(h, ml, 1, yl, 1.0)
        if not last:
            hc = gated_residual(hc, mc, 1, yc, 1.0)
            hc = gated_residual(hc, mc, 2, swiglu(ada_in(hc, norm_g[i, 2], mc, 2), ffn_w_in[i, 1], ffn_w_out[i, 1]), 0.5)
        h = gated_residual(h, ml, 2, swiglu(ada_in(h, norm_g[i, 2], ml, 2), ffn_w_in[i, 1], ffn_w_out[i, 1]), 0.5)
    return rms_norm(h, final_norm_g)
```

```python
import functools

import jax
import jax.numpy as jnp
from jax import lax
from jax.experimental import pallas as pl
from jax.experimental.pallas import tpu as pltpu

F32 = jnp.float32
BF16 = jnp.bfloat16

EPS = 1e-6
NEG_INF = -1e30
ROPE_BASE = 10000.0
GRID_W = 64
N_MIXERS = 4
FFN_HIDDEN = 2816
RET_HEADS = 4
RET_QK_DIM = 256
RET_V_DIM = 512
RET_CHUNK = 128
NAT_HEADS = 16
NAT_HEAD_DIM = 64
NAT_KH = 8
NAT_KW = 16
POOL_WINDOWS = (2, 4, 8, 16)
SWA_Q_HEADS = 16
SWA_KV_HEADS = 4
SWA_HEAD_DIM = 64
SWA_WINDOW = 128
SWA_BLOCK = 128

LANES = 128
SUBLANES = 8
VMEM_LIMIT = 56 * 1024 * 1024


def _params(*sem):
    return pltpu.CompilerParams(dimension_semantics=sem, vmem_limit_bytes=VMEM_LIMIT)


def _resident(shape):
    nd = len(shape)
    return pl.BlockSpec(shape, lambda *_: (0,) * nd, pipeline_mode=pl.Buffered(1))


def _silu(x):
    return x * jax.nn.sigmoid(x)


def _ada(x, g, shift, scale):
    var = jnp.mean(x * x, axis=-1, keepdims=True)
    y = x * lax.rsqrt(var + EPS) * g
    return y * (1.0 + scale) + shift


def _mod_rows(mod_ref, j):
    return (mod_ref[0, 3 * j:3 * j + 1, :], mod_ref[0, 3 * j + 1:3 * j + 2, :],
            mod_ref[0, 3 * j + 2:3 * j + 3, :])


def _mod_spec(mod, d):
    if mod.shape[0] == 1:
        return pl.BlockSpec((1, 9, d), lambda b, t: (0, 0, 0))
    return pl.BlockSpec((1, 9, d), lambda b, t: (b, 0, 0))


def _dot(a, b):
    return jnp.dot(a, b, preferred_element_type=F32)


def _dot_nt(a, b):
    return lax.dot_general(a, b, (((1,), (1,)), ((), ())), preferred_element_type=F32)


def _dot_tn(a, b):
    return lax.dot_general(a, b, (((0,), (0,)), ((), ())), preferred_element_type=F32)


def _mod_kernel(c_ref, w_ref, b_ref, o_ref):
    s = _silu(c_ref[...])
    o_ref[0] = jnp.dot(s, w_ref[0], preferred_element_type=F32,
                       precision=lax.Precision.HIGHEST) + b_ref[0]


def _modulation(cc, w_mod, b_mod):
    depth, d, n = w_mod.shape
    tn = 1024
    return pl.pallas_call(
        _mod_kernel,
        grid=(depth, n // tn),
        in_specs=[pl.BlockSpec((SUBLANES, d), lambda l, j: (0, 0)),
                  pl.BlockSpec((1, d, tn), lambda l, j: (l, 0, j)),
                  pl.BlockSpec((1, 1, tn), lambda l, j: (l, 0, j))],
        out_specs=pl.BlockSpec((1, SUBLANES, tn), lambda l, j: (l, 0, j)),
        out_shape=jax.ShapeDtypeStruct((depth, SUBLANES, n), F32),
        compiler_params=_params("arbitrary", "arbitrary"),
        name="modulation",
    )(cc, w_mod, b_mod.reshape(depth, 1, n))


def _ffn_kernel(h_ref, mod_ref, g_ref, win_ref, wout_ref, *rest, j, final):
    o_ref = rest[-1]
    x = h_ref[0]
    shift, scale, gate = _mod_rows(mod_ref, j)
    xb = _ada(x, g_ref[...], shift, scale).astype(BF16)
    f = FFN_HIDDEN
    a = _dot(xb, win_ref[:, :f])
    b = _dot(xb, win_ref[:, f:])
    hid = (_silu(a) * b).astype(BF16)
    y = _dot(hid, wout_ref[...])
    out = x + (0.5 * gate) * y
    if final:
        var = jnp.mean(out * out, axis=-1, keepdims=True)
        out = out * lax.rsqrt(var + EPS) * rest[0][...]
    o_ref[0] = out


def _ffn(h, mod, g, w_in, w_out, j, final_g=None):
    bsz, t, d = h.shape
    tm = 256
    final = final_g is not None
    in_specs = [pl.BlockSpec((1, tm, d), lambda b, i: (b, i, 0)),
                _mod_spec(mod, d),
                pl.BlockSpec((1, d), lambda b, i: (0, 0)),
                _resident(w_in.shape),
                _resident(w_out.shape)]
    args = [h, mod, g.reshape(1, d), w_in, w_out]
    if final:
        in_specs.append(pl.BlockSpec((1, d), lambda b, i: (0, 0)))
        args.append(final_g.reshape(1, d))
    return pl.pallas_call(
        functools.partial(_ffn_kernel, j=j, final=final),
        grid=(bsz, t // tm),
        in_specs=in_specs,
        out_specs=pl.BlockSpec((1, tm, d), lambda b, i: (b, i, 0)),
        out_shape=jax.ShapeDtypeStruct((bsz, t, d), F32),
        compiler_params=_params("parallel", "parallel"),
        name="ffn",
    )(*args)


def _out_proj_kernel(y_ref, w_ref, h_ref, mod_ref, o_ref):
    gate = mod_ref[0, 5:6, :]
    o_ref[0] = h_ref[0] + gate * _dot(y_ref[0], w_ref[...])


def _out_proj(y, w, h, mod):
    bsz, t, d = h.shape
    tm = min(t, 512)
    k = y.shape[-1]
    return pl.pallas_call(
        _out_proj_kernel,
        grid=(bsz, t // tm),
        in_specs=[pl.BlockSpec((1, tm, k), lambda b, i: (b, i, 0)),
                  _resident(w.shape),
                  pl.BlockSpec((1, tm, d), lambda b, i: (b, i, 0)),
                  _mod_spec(mod, d)],
        out_specs=pl.BlockSpec((1, tm, d), lambda b, i: (b, i, 0)),
        out_shape=jax.ShapeDtypeStruct((bsz, t, d), F32),
        compiler_params=_params("parallel", "parallel"),
        name="out_proj",
    )(y, w, h, mod)


def _ret_proj_kernel(h_ref, mod_ref, g_ref, w_ref, *rest, rope):
    if rope:
        cos_ref, sin_ref, q_ref, k_ref, v_ref, gate_ref = rest
    else:
        q_ref, k_ref, v_ref, gate_ref = rest
    shift, scale, _ = _mod_rows(mod_ref, 1)
    xb = _ada(h_ref[0], g_ref[...], shift, scale).astype(BF16)
    dk, nh = RET_QK_DIM, RET_HEADS
    half = dk // 2

    def rot(x):
        if not rope:
            return x.astype(BF16)
        c, s = cos_ref[...], sin_ref[...]
        x1, x2 = x[:, :half], x[:, half:]
        return jnp.concatenate([x1 * c - x2 * s, x2 * c + x1 * s], axis=-1).astype(BF16)

    for hd in range(nh):
        q_ref[0, :, hd * dk:(hd + 1) * dk] = rot(_dot(xb, w_ref[:, hd * dk:(hd + 1) * dk]))
        kcol = nh * dk + hd * dk
        k_ref[0, :, hd * dk:(hd + 1) * dk] = rot(_dot(xb, w_ref[:, kcol:kcol + dk]) * dk ** -0.5)
    v0 = 2 * nh * dk
    nv = nh * RET_V_DIM
    v_ref[0] = _dot(xb, w_ref[:, v0:v0 + nv]).astype(BF16)
    gate_ref[0] = _dot(xb, w_ref[:, v0 + nv:v0 + 2 * nv])


def _ret_proj(h, mod, g, w, cos=None, sin=None):
    bsz, t, d = h.shape
    tm = 256
    rope = cos is not None
    nqk = RET_HEADS * RET_QK_DIM
    nv = RET_HEADS * RET_V_DIM
    row = lambda n: pl.BlockSpec((1, tm, n), lambda b, i: (b, i, 0))
    in_specs = [row(d), _mod_spec(mod, d), pl.BlockSpec((1, d), lambda b, i: (0, 0)), _resident(w.shape)]
    args = [h, mod, g.reshape(1, d), w]
    if rope:
        in_specs += [pl.BlockSpec((tm, RET_QK_DIM // 2), lambda b, i: (i, 0))] * 2
        args += [cos, sin]
    return pl.pallas_call(
        functools.partial(_ret_proj_kernel, rope=rope),
        grid=(bsz, t // tm),
        in_specs=in_specs,
        out_specs=[row(nqk), row(nqk), row(nv), row(nv)],
        out_shape=[jax.ShapeDtypeStruct((bsz, t, nqk), BF16), jax.ShapeDtypeStruct((bsz, t, nqk), BF16),
                   jax.ShapeDtypeStruct((bsz, t, nv), BF16), jax.ShapeDtypeStruct((bsz, t, nv), F32)],
        compiler_params=_params("parallel", "parallel"),
        name="ret_proj",
    )(*args)


def _log_sigmoid(x):
    return jnp.minimum(x, 0.0) - jnp.log(1.0 + jnp.exp(-jnp.abs(x)))


def _ret_scan_kernel(dec_ref, q_ref, k_ref, v_ref, s0_ref, o_ref, sfin_ref, s_scr, *, reverse):
    n = pl.program_id(2)
    c = RET_CHUNK

    @pl.when(n == 0)
    def _():
        s_scr[...] = s0_ref[0, 0]

    lg = _log_sigmoid(dec_ref[0])
    lg1 = lg[:, :1]
    ii = lax.broadcasted_iota(jnp.int32, (c, c), 0)
    jj = lax.broadcasted_iota(jnp.int32, (c, c), 1)
    pos = lax.broadcasted_iota(jnp.int32, (c, 1), 0).astype(F32)
    if reverse:
        dist = jj - ii
        keep = dist > 0
        q_dec = jnp.exp((c - pos) * lg1)
        k_dec = jnp.exp(pos * lg1)
    else:
        dist = ii - jj
        keep = dist >= 0
        q_dec = jnp.exp((pos + 1.0) * lg1)
        k_dec = jnp.exp((c - 1.0 - pos) * lg1)
    intra = jnp.where(keep, jnp.exp(jnp.maximum(dist, 0).astype(F32) * lg), 0.0)
    c_dec = jnp.exp(c * lg1)

    q, k, v = q_ref[0], k_ref[0], v_ref[0]
    att = (_dot_nt(q, k) * intra).astype(BF16)
    qd = (q.astype(F32) * q_dec).astype(BF16)
    kd = (k.astype(F32) * k_dec).astype(BF16)
    s = s_scr[...]
    o_ref[0] = _dot(att, v) + _dot(qd, s.astype(BF16))
    s_new = s * c_dec + _dot_tn(kd, v)
    s_scr[...] = s_new

    @pl.when(n == pl.num_programs(2) - 1)
    def _():
        sfin_ref[0, 0] = s_new


def _ret_scan(dec, q, k, v, s0, reverse):
    bsz, t, _ = q.shape
    c, nh, dk, dv = RET_CHUNK, RET_HEADS, RET_QK_DIM, RET_V_DIM
    nc = t // c
    chunk = (lambda n: nc - 1 - n) if reverse else (lambda n: n)
    return pl.pallas_call(
        functools.partial(_ret_scan_kernel, reverse=reverse),
        grid=(bsz, nh, nc),
        in_specs=[pl.BlockSpec((1, 1, LANES), lambda b, h, n: (h, 0, 0)),
                  pl.BlockSpec((1, c, dk), lambda b, h, n: (b, chunk(n), h)),
                  pl.BlockSpec((1, c, dk), lambda b, h, n: (b, chunk(n), h)),
                  pl.BlockSpec((1, c, dv), lambda b, h, n: (b, chunk(n), h)),
                  pl.BlockSpec((1, 1, dk, dv), lambda b, h, n: (b, h, 0, 0))],
        out_specs=[pl.BlockSpec((1, c, dv), lambda b, h, n: (b, chunk(n), h)),
                   pl.BlockSpec((1, 1, dk, dv), lambda b, h, n: (b, h, 0, 0))],
        out_shape=[jax.ShapeDtypeStruct((bsz, t, nh * dv), F32),
                   jax.ShapeDtypeStruct((bsz, nh, dk, dv), F32)],
        scratch_shapes=[pltpu.VMEM((dk, dv), F32)],
        compiler_params=_params("parallel", "parallel", "arbitrary"),
        name="ret_scan_bwd" if reverse else "ret_scan_fwd",
    )(dec, q, k, v, s0)


def _ret_out_kernel(of_ref, ob_ref, gate_ref, gn_ref, w_ref, h_ref, mod_ref, o_ref):
    o = of_ref[0] + ob_ref[0]
    dv = RET_V_DIM
    parts = []
    for hd in range(RET_HEADS):
        oh = o[:, hd * dv:(hd + 1) * dv]
        mu = jnp.mean(oh, axis=-1, keepdims=True)
        ctr = oh - mu
        var = jnp.mean(ctr * ctr, axis=-1, keepdims=True)
        parts.append(ctr * lax.rsqrt(var + EPS) * gn_ref[:, hd * dv:(hd + 1) * dv])
    on = jnp.concatenate(parts, axis=-1)
    y = _dot((_silu(gate_ref[0]) * on).astype(BF16), w_ref[...])
    o_ref[0] = h_ref[0] + mod_ref[0, 5:6, :] * y


def _ret_out(o_f, o_b, gate, gn_g, w, h, mod):
    bsz, t, d = h.shape
    tm = 256
    nv = o_f.shape[-1]
    row = lambda n: pl.BlockSpec((1, tm, n), lambda b, i: (b, i, 0))
    return pl.pallas_call(
        _ret_out_kernel,
        grid=(bsz, t // tm),
        in_specs=[row(nv), row(nv), row(nv), pl.BlockSpec((1, nv), lambda b, i: (0, 0)),
                  _resident(w.shape), row(d), _mod_spec(mod, d)],
        out_specs=row(d),
        out_shape=jax.ShapeDtypeStruct((bsz, t, d), F32),
        compiler_params=_params("parallel", "parallel"),
        name="ret_out",
    )(o_f, o_b, gate, gn_g.reshape(1, nv), w, h, mod)


def _qkv_proj_kernel(h_ref, mod_ref, g_ref, w_ref, *rest, widths, q_scale, rope):
    if rope:
        cos_ref, sin_ref = rest[:2]
        outs = rest[2:]
    else:
        outs = rest
    shift, scale, _ = _mod_rows(mod_ref, 1)
    xb = _ada(h_ref[0], g_ref[...], shift, scale).astype(BF16)
    col = 0
    for idx, (o_ref, n) in enumerate(zip(outs, widths)):
        for c0 in range(0, n, LANES):
            y = _dot(xb, w_ref[:, col + c0:col + c0 + LANES])
            if rope and idx < 2:
                lane = lax.broadcasted_iota(jnp.int32, y.shape, 1)
                partner = jnp.where((lane & 16) == 0, pltpu.roll(y, LANES - 16, axis=1), pltpu.roll(y, 16, axis=1))
                y = y * cos_ref[...] + partner * sin_ref[...]
            if idx == 0:
                y = y * q_scale
            o_ref[0, :, c0:c0 + LANES] = y.astype(BF16)
        col += n


def _qkv_proj(h, mod, g, w, widths, q_scale, cos=None, sin=None):
    bsz, t, d = h.shape
    tm = 256
    rope = cos is not None
    row = lambda n: pl.BlockSpec((1, tm, n), lambda b, i: (b, i, 0))
    in_specs = [row(d), _mod_spec(mod, d), pl.BlockSpec((1, d), lambda b, i: (0, 0)), _resident(w.shape)]
    args = [h, mod, g.reshape(1, d), w]
    if rope:
        in_specs += [pl.BlockSpec((tm, LANES), lambda b, i: (i, 0))] * 2
        args += [cos, sin]
    return pl.pallas_call(
        functools.partial(_qkv_proj_kernel, widths=widths, q_scale=q_scale, rope=rope),
        grid=(bsz, t // tm),
        in_specs=in_specs,
        out_specs=[row(n) for n in widths],
        out_shape=[jax.ShapeDtypeStruct((bsz, t, n), BF16) for n in widths],
        compiler_params=_params("parallel", "parallel"),
        name="qkv_proj",
    )(*args)


def _split_heads(x):
    lane = lax.broadcasted_iota(jnp.int32, x.shape, 1)
    zero = jnp.zeros_like(x)
    return jnp.concatenate([jnp.where(lane < 64, x, zero), jnp.where(lane >= 64, x, zero)], axis=0)


def _merge_heads(o):
    r = o.shape[0] // 2
    lane = lax.broadcasted_iota(jnp.int32, (r, o.shape[1]), 1)
    return jnp.where(lane < 64, o[:r], o[r:])


def _nat_kernel(q_ref, k_ref, v_ref, kc_ref, vc_ref, bias_ref, o_ref, *, rows):
    w = GRID_W
    kh = min(NAT_KH, rows)
    nk = kh * w
    kc = kc_ref[0]
    vc = vc_ref[0]

    def body(r, carry):
        r0 = jnp.clip(r - kh // 2, 0, rows - kh)
        ty = r - r0
        qs = pl.multiple_of(r * w, w)
        ks = pl.multiple_of(r0 * w, w)
        q2 = _split_heads(q_ref[0, pl.ds(qs, w), :])
        kw = k_ref[0, pl.ds(ks, nk), :]
        vw = v_ref[0, pl.ds(ks, nk), :]
        bias = jnp.concatenate([bias_ref[0, ty], bias_ref[1, ty]], axis=0)
        s_nb = _dot_nt(q2, kw) + bias
        s_cx = _dot_nt(q2, kc)
        m = jnp.maximum(jnp.max(s_nb, axis=-1, keepdims=True), jnp.max(s_cx, axis=-1, keepdims=True))
        p_nb = jnp.exp(s_nb - m)
        p_cx = jnp.exp(s_cx - m)
        l = jnp.sum(p_nb, axis=-1, keepdims=True) + jnp.sum(p_cx, axis=-1, keepdims=True)
        o = (_dot(p_nb.astype(BF16), vw) + _dot(p_cx.astype(BF16), vc)) / l
        o_ref[0, pl.ds(qs, w), :] = _merge_heads(o).astype(BF16)
        return carry

    lax.fori_loop(0, rows, body, 0)


def _nat_attention(q, k, v, kc, vc, bias):
    bsz, t, d = q.shape
    l = kc.shape[1]
    rows = t // GRID_W
    hp = d // LANES
    lat = pl.BlockSpec((1, t, LANES), lambda p, b: (b, 0, p))
    ctx = pl.BlockSpec((1, l, LANES), lambda p, b: (b, 0, p))
    return pl.pallas_call(
        functools.partial(_nat_kernel, rows=rows),
        grid=(hp, bsz),
        in_specs=[lat, lat, lat, ctx, ctx,
                  pl.BlockSpec((2,) + bias.shape[1:], lambda p, b: (p, 0, 0, 0))],
        out_specs=lat,
        out_shape=jax.ShapeDtypeStruct((bsz, t, d), BF16),
        compiler_params=_params("parallel", "parallel"),
        name="nat_attention",
    )(q, k, v, kc, vc, bias)


def _nat_bias(rpb, rows):
    kh = min(NAT_KH, rows)
    w = GRID_W
    ty = jnp.arange(kh)[:, None, None, None]
    c = jnp.arange(w)[None, :, None, None]
    a = jnp.arange(kh)[None, None, :, None]
    cc = jnp.arange(w)[None, None, None, :]
    c0 = jnp.clip(c - NAT_KW // 2, 0, w - NAT_KW)
    inside = (cc >= c0) & (cc < c0 + NAT_KW)
    ridx = jnp.broadcast_to(a - ty + NAT_KH - 1, (kh, w, kh, w))
    cidx = jnp.broadcast_to(jnp.clip(cc - c + NAT_KW - 1, 0, 2 * NAT_KW - 2), (kh, w, kh, w))
    vals = rpb.astype(F32)[:, jnp.clip(ridx, 0, 2 * NAT_KH - 2), cidx]
    return jnp.where(inside[None], vals, NEG_INF).reshape(rpb.shape[0], kh, w, kh * w)


def _ctx_attn_kernel(q_ref, k_ref, v_ref, o_ref):
    q2 = _split_heads(q_ref[0])
    s = _dot_nt(q2, k_ref[0])
    p = jnp.exp(s - jnp.max(s, axis=-1, keepdims=True))
    l = jnp.sum(p, axis=-1, keepdims=True)
    o_ref[0] = _merge_heads(_dot(p.astype(BF16), v_ref[0]) / l).astype(BF16)


def _ctx_attention(q, k, v):
    bsz, l, d = q.shape
    blk = pl.BlockSpec((1, l, LANES), lambda b, p: (b, 0, p))
    return pl.pallas_call(
        _ctx_attn_kernel,
        grid=(bsz, d // LANES),
        in_specs=[blk, blk, blk],
        out_specs=blk,
        out_shape=jax.ShapeDtypeStruct((bsz, l, d), BF16),
        compiler_params=_params("parallel", "parallel"),
        name="ctx_attention",
    )(q, k, v)


POOL_HALO = SUBLANES
POOL_K = 384


def _pool_kernel(h_ref, prev_ref, next_ref, mod_ref, g_ref, pw_ref, ps_ref, o_ref, *, tm, t_total):
    i = pl.program_id(1)
    x = h_ref[0]
    d = x.shape[-1]
    shift, scale, gate = _mod_rows(mod_ref, 1)
    pad = jnp.zeros((POOL_K - tm - 2 * POOL_HALO, d), F32)
    xe = _ada(jnp.concatenate([prev_ref[0], x, next_ref[0]], axis=0), g_ref[...], shift, scale)
    xn = xe[POOL_HALO:POOL_HALO + tm]
    xe = jnp.concatenate([xe, pad], axis=0)
    x_hi = xe.astype(BF16)
    r1 = xe - x_hi.astype(F32)
    x_mid = r1.astype(BF16)
    x_lo = (r1 - x_mid.astype(F32)).astype(BF16)
    tpos = i * tm + lax.broadcasted_iota(jnp.int32, (tm, POOL_K), 0)
    upos = i * tm - POOL_HALO + lax.broadcasted_iota(jnp.int32, (tm, POOL_K), 1)
    trow = i * tm + lax.broadcasted_iota(jnp.int32, (tm, 1), 0)
    gd = d // len(POOL_WINDOWS)
    parts = []
    for gi, win in enumerate(POOL_WINDOWS):
        half = win // 2
        lo = jnp.maximum(tpos - half, 0)
        hi = jnp.minimum(tpos + half, t_total)
        band = jnp.where((upos >= lo) & (upos < hi), 1.0, 0.0).astype(BF16)
        cnt = (jnp.minimum(trow + half, t_total) - jnp.maximum(trow - half, 0)).astype(F32)
        cols = slice(gi * gd, (gi + 1) * gd)
        tot = _dot(band, x_hi[:, cols]) + _dot(band, x_mid[:, cols]) + _dot(band, x_lo[:, cols])
        pooled = (tot / cnt - xn[:, cols]).astype(BF16)
        parts.append(_dot(pooled, pw_ref[gi]))
    y = jnp.concatenate(parts, axis=-1) * ps_ref[...]
    o_ref[0] = x + gate * y


def _pool(h, mod, g, pw, ps):
    bsz, t, d = h.shape
    tm = 256
    per = tm // POOL_HALO
    last = t // POOL_HALO - 1
    return pl.pallas_call(
        functools.partial(_pool_kernel, tm=tm, t_total=t),
        grid=(bsz, t // tm),
        in_specs=[pl.BlockSpec((1, tm, d), lambda b, i: (b, i, 0)),
                  pl.BlockSpec((1, POOL_HALO, d), lambda b, i: (b, jnp.maximum(i * per - 1, 0), 0)),
                  pl.BlockSpec((1, POOL_HALO, d), lambda b, i: (b, jnp.minimum((i + 1) * per, last), 0)),
                  _mod_spec(mod, d),
                  pl.BlockSpec((1, d), lambda b, i: (0, 0)),
                  _resident(pw.shape),
                  pl.BlockSpec((1, d), lambda b, i: (0, 0))],
        out_specs=pl.BlockSpec((1, tm, d), lambda b, i: (b, i, 0)),
        out_shape=jax.ShapeDtypeStruct((bsz, t, d), F32),
        compiler_params=_params("parallel", "parallel"),
        name="pool",
    )(h, h, h, mod, g.reshape(1, d), pw, ps.reshape(1, d))


def _swa_kernel(sink_ref, q_ref, k_ref, v_ref, kc_ref, vc_ref, o_ref, *, t_total):
    kv = pl.program_id(1)
    blk = SWA_BLOCK
    span = 2 * SWA_WINDOW + blk
    grp = SWA_Q_HEADS // SWA_KV_HEADS
    kc = kc_ref[0]
    vc = vc_ref[0]
    row = lax.broadcasted_iota(jnp.int32, (grp * blk, 1), 0)
    sink = jnp.zeros((grp * blk, 1), F32)
    for gi in range(grp):
        sink = jnp.where((row >= gi * blk) & (row < (gi + 1) * blk), sink_ref[kv * grp + gi], sink)
    qoff = lax.broadcasted_iota(jnp.int32, (grp * blk, span), 0) & (blk - 1)
    koff = lax.broadcasted_iota(jnp.int32, (grp * blk, span), 1)

    def body(n, carry):
        qs = pl.multiple_of(n * blk, blk)
        ks = pl.multiple_of(jnp.clip(n * blk - SWA_WINDOW, 0, t_total - span), blk)
        qb = q_ref[0, pl.ds(qs, blk), :]
        q4 = jnp.concatenate([_split_heads(qb[:, :LANES]), _split_heads(qb[:, LANES:])], axis=0)
        kw = k_ref[0, pl.ds(ks, span), :]
        vw = v_ref[0, pl.ds(ks, span), :]
        allowed = jnp.abs((qs + qoff) - (ks + koff)) <= SWA_WINDOW
        s_loc = jnp.where(allowed, _dot_nt(q4, kw), NEG_INF)
        s_cx = _dot_nt(q4, kc)
        m = jnp.maximum(jnp.maximum(jnp.max(s_loc, axis=-1, keepdims=True),
                                    jnp.max(s_cx, axis=-1, keepdims=True)), sink)
        p_loc = jnp.exp(s_loc - m)
        p_cx = jnp.exp(s_cx - m)
        l = jnp.sum(p_loc, axis=-1, keepdims=True) + jnp.sum(p_cx, axis=-1, keepdims=True) + jnp.exp(sink - m)
        o = (_dot(p_loc.astype(BF16), vw) + _dot(p_cx.astype(BF16), vc)) / l
        out = jnp.concatenate([_merge_heads(o[:2 * blk]), _merge_heads(o[2 * blk:])], axis=-1)
        o_ref[0, pl.ds(qs, blk), :] = out.astype(BF16)
        return carry

    lax.fori_loop(0, t_total // blk, body, 0)


def _swa_attention(sink, q, k, v, kc, vc):
    bsz, t, d = q.shape
    l = kc.shape[1]
    qw = d // SWA_KV_HEADS
    grid_spec = pltpu.PrefetchScalarGridSpec(
        num_scalar_prefetch=1,
        grid=(bsz, SWA_KV_HEADS),
        in_specs=[pl.BlockSpec((1, t, qw), lambda b, h, s: (b, 0, h)),
                  pl.BlockSpec((1, t, LANES), lambda b, h, s: (b, 0, h)),
                  pl.BlockSpec((1, t, LANES), lambda b, h, s: (b, 0, h)),
                  pl.BlockSpec((1, l, LANES), lambda b, h, s: (b, 0, h)),
                  pl.BlockSpec((1, l, LANES), lambda b, h, s: (b, 0, h))],
        out_specs=pl.BlockSpec((1, t, qw), lambda b, h, s: (b, 0, h)),
    )
    return pl.pallas_call(
        functools.partial(_swa_kernel, t_total=t),
        grid_spec=grid_spec,
        out_shape=jax.ShapeDtypeStruct((bsz, t, d), BF16),
        compiler_params=_params("parallel", "parallel"),
        name="swa_attention",
    )(sink, q, k, v, kc, vc)


def _rope_angles(positions, dim):
    seg = dim // len(positions)
    inv = ROPE_BASE ** (-jnp.arange(0, seg, 2, dtype=F32) / seg)
    return jnp.concatenate([jnp.tile(p.astype(F32)[:, None] * inv, (1, 2)) for p in positions], axis=-1)


def _dup_heads(w, heads, dh):
    d = w.shape[0]
    return jnp.broadcast_to(w.reshape(d, heads, 1, dh), (d, heads, 2, dh)).reshape(d, heads * 2 * dh)


def kernel(x, c, ctx, c_ctx, w_mod, b_mod, norm_g, ffn_w_in, ffn_w_out, ret_w_in, ret_w_out, ret_gn_g, ret_decay_f, ret_decay_b, nat_w_qkv, nat_w_o, nat_rpb, pool_w, pool_scale, swa_w_qkv, swa_w_o, swa_sink, final_norm_g):
    bsz, t, d = x.shape
    depth = w_mod.shape[0]
    cc = jnp.concatenate([c, c_ctx[None], jnp.zeros((SUBLANES - bsz - 1, d), F32)], axis=0)
    mods = _modulation(cc, w_mod, b_mod)
    h, hc = x, ctx
    for i in range(depth):
        kind, occ = i % N_MIXERS, i // N_MIXERS
        last = i == depth - 1
        ctx_live = (not last) or kind != 2
        ml = mods[i, :bsz].reshape(bsz, 9, d)
        mc = mods[i, bsz:bsz + 1].reshape(1, 9, d)
        w_in = [ffn_w_in[i, s].astype(BF16) for s in range(2)]
        w_out = [ffn_w_out[i, s].astype(BF16) for s in range(2)]
        h = _ffn(h, ml, norm_g[i, 0], w_in[0], w_out[0], 0)
        if ctx_live:
            hc = _ffn(hc, mc, norm_g[i, 0], w_in[0], w_out[0], 0)
        g1 = norm_g[i, 1]
        if kind == 0:
            w = ret_w_in[occ].astype(BF16)
            wo = ret_w_out[occ].astype(BF16)
            ang = _rope_angles([jnp.arange(t)], RET_QK_DIM)[:, :RET_QK_DIM // 2]
            dec_f = jnp.broadcast_to(ret_decay_f[occ].astype(F32)[:, None, None], (RET_HEADS, 1, LANES))
            dec_b = jnp.broadcast_to(ret_decay_b[occ].astype(F32)[:, None, None], (RET_HEADS, 1, LANES))
            qc, kc, vc, gc = _ret_proj(hc, mc, g1, w)
            zeros = jnp.zeros((bsz, RET_HEADS, RET_QK_DIM, RET_V_DIM), F32)
            ocf, s_f = _ret_scan(dec_f, qc, kc, vc, zeros, False)
            ocb, s_b = _ret_scan(dec_b, qc, kc, vc, zeros, True)
            ql, kl, vl, gl = _ret_proj(h, ml, g1, w, jnp.cos(ang), jnp.sin(ang))
            olf, _ = _ret_scan(dec_f, ql, kl, vl, s_f, False)
            olb, _ = _ret_scan(dec_b, ql, kl, vl, s_b, True)
            h = _ret_out(olf, olb, gl, ret_gn_g[occ], wo, h, ml)
            if not last:
                hc = _ret_out(ocf, ocb, gc, ret_gn_g[occ], wo, hc, mc)
        elif kind == 1:
            w = nat_w_qkv[occ].astype(BF16)
            wo = nat_w_o[occ].astype(BF16)
            widths = (d, d, d)
            qc, kc, vc = _qkv_proj(hc, mc, g1, w, widths, NAT_HEAD_DIM ** -0.5)
            ql, kl, vl = _qkv_proj(h, ml, g1, w, widths, NAT_HEAD_DIM ** -0.5)
            ol = _nat_attention(ql, kl, vl, kc, vc, _nat_bias(nat_rpb[occ], t // GRID_W))
            h = _out_proj(ol, wo, h, ml)
            if not last:
                hc = _out_proj(_ctx_attention(qc, kc, vc), wo, hc, mc)
        elif kind == 2:
            pw = pool_w[occ].astype(BF16)
            h_new = _pool(h, ml, g1, pw, pool_scale[occ])
            if not last:
                hc = _pool(hc, mc, g1, pw, pool_scale[occ])
            h = h_new
        else:
            nq = SWA_Q_HEADS * SWA_HEAD_DIM
            nkv = SWA_KV_HEADS * SWA_HEAD_DIM
            wq = swa_w_qkv[occ][:, :nq]
            wk = _dup_heads(swa_w_qkv[occ][:, nq:nq + nkv], SWA_KV_HEADS, SWA_HEAD_DIM)
            wv = _dup_heads(swa_w_qkv[occ][:, nq + nkv:], SWA_KV_HEADS, SWA_HEAD_DIM)
            w = jnp.concatenate([wq, wk, wv], axis=-1).astype(BF16)
            wo = swa_w_o[occ].astype(BF16)
            widths = (nq, 2 * nkv, 2 * nkv)
            tt = jnp.arange(t)
            ang = _rope_angles([tt // GRID_W, tt % GRID_W], SWA_HEAD_DIM)
            ang = jnp.tile(ang, (1, LANES // SWA_HEAD_DIM))
            lane = jnp.arange(LANES)
            sin = jnp.where(lane % 32 < 16, -jnp.sin(ang), jnp.sin(ang))
            if last:
                _, kc, vc = _qkv_proj(hc, mc, g1, w, widths, SWA_HEAD_DIM ** -0.5)
            else:
                raise NotImplementedError("windowed attention with live context outputs")
            ql, kl, vl = _qkv_proj(h, ml, g1, w, widths, SWA_HEAD_DIM ** -0.5, jnp.cos(ang), sin)
            ol = _swa_attention(swa_sink[occ].astype(F32), ql, kl, vl, kc, vc)
            h = _out_proj(ol, wo, h, ml)
        if not last:
            hc = _ffn(hc, mc, norm_g[i, 2], w_in[1], w_out[1], 2)
        h = _ffn(h, ml, norm_g[i, 2], w_in[1], w_out[1], 2, final_norm_g if last else None)
    return h
```

```python
import functools

import jax
import jax.numpy as jnp
from jax import lax
from jax.experimental import pallas as pl
from jax.experimental.pallas import tpu as pltpu

F32 = jnp.float32
BF16 = jnp.bfloat16

EPS = 1e-6
NEG_INF = -1e30
ROPE_BASE = 10000.0
GRID_W = 64
N_MIXERS = 4
FFN_HIDDEN = 2816
RET_HEADS = 4
RET_QK_DIM = 256
RET_V_DIM = 512
NAT_HEADS = 16
NAT_HEAD_DIM = 64
NAT_KH = 8
NAT_KW = 16
POOL_WINDOWS = (2, 4, 8, 16)
SWA_Q_HEADS = 16
SWA_KV_HEADS = 4
SWA_HEAD_DIM = 64
SWA_WINDOW = 128
SWA_BLOCK = 128

LANES = 128
SUBLANES = 8
VMEM_LIMIT = 56 * 1024 * 1024
TOKEN_TILE = 256
RET_CHUNK = 256
XPOSE_CHUNK = 512


def _params(*sem):
    return pltpu.CompilerParams(dimension_semantics=sem, vmem_limit_bytes=VMEM_LIMIT)


def _resident(shape):
    nd = len(shape)
    return pl.BlockSpec(shape, lambda *_: (0,) * nd, pipeline_mode=pl.Buffered(1))


def _silu(x):
    return x * jax.nn.sigmoid(x)


def _ada(x, g, shift, scale):
    var = jnp.mean(x * x, axis=-1, keepdims=True)
    y = x * lax.rsqrt(var + EPS) * g
    return y * (1.0 + scale) + shift


def _mod_rows(mod_ref, j):
    return (mod_ref[0, 3 * j:3 * j + 1, :], mod_ref[0, 3 * j + 1:3 * j + 2, :],
            mod_ref[0, 3 * j + 2:3 * j + 3, :])


def _mod_spec(mod, d):
    if mod.shape[0] == 1:
        return pl.BlockSpec((1, 9, d), lambda b, t: (0, 0, 0))
    return pl.BlockSpec((1, 9, d), lambda b, t: (b, 0, 0))


def _dot(a, b):
    return jnp.dot(a, b, preferred_element_type=F32)


def _dot_nt(a, b):
    return lax.dot_general(a, b, (((1,), (1,)), ((), ())), preferred_element_type=F32)


def _dot_tn(a, b):
    return lax.dot_general(a, b, (((0,), (0,)), ((), ())), preferred_element_type=F32)


def _mod_kernel(c_ref, w_ref, b_ref, o_ref):
    s = _silu(c_ref[...])
    o_ref[0] = jnp.dot(s, w_ref[0], preferred_element_type=F32,
                       precision=lax.Precision.HIGHEST) + b_ref[0]


def _modulation(cc, w_mod, b_mod):
    depth, d, n = w_mod.shape
    tn = 1024
    return pl.pallas_call(
        _mod_kernel,
        grid=(depth, n // tn),
        in_specs=[pl.BlockSpec((SUBLANES, d), lambda l, j: (0, 0)),
                  pl.BlockSpec((1, d, tn), lambda l, j: (l, 0, j)),
                  pl.BlockSpec((1, 1, tn), lambda l, j: (l, 0, j))],
        out_specs=pl.BlockSpec((1, SUBLANES, tn), lambda l, j: (l, 0, j)),
        out_shape=jax.ShapeDtypeStruct((depth, SUBLANES, n), F32),
        compiler_params=_params("arbitrary", "arbitrary"),
        name="modulation",
    )(cc, w_mod, b_mod.reshape(depth, 1, n))


def _ffn_kernel(h_ref, mod_ref, g_ref, win_ref, wout_ref, *rest, j, final):
    o_ref = rest[-1]
    x = h_ref[0]
    shift, scale, gate = _mod_rows(mod_ref, j)
    xb = _ada(x, g_ref[...], shift, scale).astype(BF16)
    f = FFN_HIDDEN
    a = _dot(xb, win_ref[:, :f])
    b = _dot(xb, win_ref[:, f:])
    hid = (_silu(a) * b).astype(BF16)
    y = _dot(hid, wout_ref[...])
    out = x + (0.5 * gate) * y
    if final:
        var = jnp.mean(out * out, axis=-1, keepdims=True)
        out = out * lax.rsqrt(var + EPS) * rest[0][...]
    o_ref[0] = out


def _ffn(h, mod, g, w_in, w_out, j, final_g=None):
    bsz, t, d = h.shape
    tm = TOKEN_TILE
    final = final_g is not None
    in_specs = [pl.BlockSpec((1, tm, d), lambda b, i: (b, i, 0)),
                _mod_spec(mod, d),
                pl.BlockSpec((1, d), lambda b, i: (0, 0)),
                _resident(w_in.shape),
                _resident(w_out.shape)]
    args = [h, mod, g.reshape(1, d), w_in, w_out]
    if final:
        in_specs.append(pl.BlockSpec((1, d), lambda b, i: (0, 0)))
        args.append(final_g.reshape(1, d))
    return pl.pallas_call(
        functools.partial(_ffn_kernel, j=j, final=final),
        grid=(bsz, t // tm),
        in_specs=in_specs,
        out_specs=pl.BlockSpec((1, tm, d), lambda b, i: (b, i, 0)),
        out_shape=jax.ShapeDtypeStruct((bsz, t, d), F32),
        compiler_params=_params("parallel", "parallel"),
        name="ffn",
    )(*args)


def _out_proj_kernel(y_ref, w_ref, h_ref, mod_ref, o_ref):
    gate = mod_ref[0, 5:6, :]
    o_ref[0] = h_ref[0] + gate * _dot(y_ref[0], w_ref[...])


def _out_proj(y, w, h, mod):
    bsz, t, d = h.shape
    tm = min(t, 2 * TOKEN_TILE)
    k = y.shape[-1]
    return pl.pallas_call(
        _out_proj_kernel,
        grid=(bsz, t // tm),
        in_specs=[pl.BlockSpec((1, tm, k), lambda b, i: (b, i, 0)),
                  _resident(w.shape),
                  pl.BlockSpec((1, tm, d), lambda b, i: (b, i, 0)),
                  _mod_spec(mod, d)],
        out_specs=pl.BlockSpec((1, tm, d), lambda b, i: (b, i, 0)),
        out_shape=jax.ShapeDtypeStruct((bsz, t, d), F32),
        compiler_params=_params("parallel", "parallel"),
        name="out_proj",
    )(y, w, h, mod)


def _ret_proj_kernel(h_ref, mod_ref, g_ref, w_ref, *rest, rope):
    if rope:
        cos_ref, sin_ref, q_ref, k_ref, v_ref, gate_ref = rest
    else:
        q_ref, k_ref, v_ref, gate_ref = rest
    shift, scale, _ = _mod_rows(mod_ref, 1)
    xb = _ada(h_ref[0], g_ref[...], shift, scale).astype(BF16)
    dk, nh = RET_QK_DIM, RET_HEADS
    half = dk // 2

    def rot(x):
        if not rope:
            return x.astype(BF16)
        c, s = cos_ref[...], sin_ref[...]
        x1, x2 = x[:, :half], x[:, half:]
        return jnp.concatenate([x1 * c - x2 * s, x2 * c + x1 * s], axis=-1).astype(BF16)

    for hd in range(nh):
        q_ref[0, :, hd * dk:(hd + 1) * dk] = rot(_dot(xb, w_ref[:, hd * dk:(hd + 1) * dk]))
        kcol = nh * dk + hd * dk
        k_ref[0, :, hd * dk:(hd + 1) * dk] = rot(_dot(xb, w_ref[:, kcol:kcol + dk]) * dk ** -0.5)
    v0 = 2 * nh * dk
    nv = nh * RET_V_DIM
    v_ref[0] = _dot(xb, w_ref[:, v0:v0 + nv]).astype(BF16)
    gate_ref[0] = _silu(_dot(xb, w_ref[:, v0 + nv:v0 + 2 * nv])).astype(BF16)


def _ret_proj(h, mod, g, w, cos=None, sin=None):
    bsz, t, d = h.shape
    tm = TOKEN_TILE
    rope = cos is not None
    nqk = RET_HEADS * RET_QK_DIM
    nv = RET_HEADS * RET_V_DIM
    row = lambda n: pl.BlockSpec((1, tm, n), lambda b, i: (b, i, 0))
    in_specs = [row(d), _mod_spec(mod, d), pl.BlockSpec((1, d), lambda b, i: (0, 0)), _resident(w.shape)]
    args = [h, mod, g.reshape(1, d), w]
    if rope:
        in_specs += [pl.BlockSpec((tm, RET_QK_DIM // 2), lambda b, i: (i, 0))] * 2
        args += [cos, sin]
    return pl.pallas_call(
        functools.partial(_ret_proj_kernel, rope=rope),
        grid=(bsz, t // tm),
        in_specs=in_specs,
        out_specs=[row(nqk), row(nqk), row(nv), row(nv)],
        out_shape=[jax.ShapeDtypeStruct((bsz, t, nqk), BF16), jax.ShapeDtypeStruct((bsz, t, nqk), BF16),
                   jax.ShapeDtypeStruct((bsz, t, nv), BF16), jax.ShapeDtypeStruct((bsz, t, nv), BF16)],
        compiler_params=_params("parallel", "parallel"),
        name="ret_proj",
    )(*args)


def _log_sigmoid(x):
    return jnp.minimum(x, 0.0) - jnp.log(1.0 + jnp.exp(-jnp.abs(x)))


def _ret_mix_kernel(decf_ref, decb_ref, q_ref, k_ref, v_ref, g_ref, gn_ref, s0f_ref, s0b_ref,
                    y_ref, sf_ref, sb_ref, s_scr, sb_scr, *, chunk):
    c = chunk
    nc = q_ref.shape[1] // c
    lgf = _log_sigmoid(decf_ref[0])[:, :1]
    lgb = _log_sigmoid(decb_ref[0])[:, :1]
    dist = lax.broadcasted_iota(jnp.int32, (c, c), 0) - lax.broadcasted_iota(jnp.int32, (c, c), 1)
    decay = jnp.where(dist >= 0,
                      jnp.exp(jnp.maximum(dist, 0).astype(F32) * lgf),
                      jnp.exp(jnp.maximum(-dist, 0).astype(F32) * lgb))
    pos = lax.broadcasted_iota(jnp.int32, (c, 1), 0).astype(F32)
    qd_f, kd_f, cd_f = jnp.exp((pos + 1.0) * lgf), jnp.exp((c - 1.0 - pos) * lgf), jnp.exp(c * lgf)
    qd_b, kd_b, cd_b = jnp.exp((c - pos) * lgb), jnp.exp(pos * lgb), jnp.exp(c * lgb)

    def rows(n):
        return pl.ds(pl.multiple_of(n * c, c), c)

    s_scr[...] = s0b_ref[0, 0]

    def bwd(i, carry):
        n = nc - 1 - i
        s = s_scr[...]
        sb_scr[n] = s.astype(BF16)
        kd = (k_ref[0, rows(n), :].astype(F32) * kd_b).astype(BF16)
        s_scr[...] = s * cd_b + _dot_tn(kd, v_ref[0, rows(n), :])
        return carry

    lax.fori_loop(0, nc, bwd, 0)
    sb_ref[0, 0] = s_scr[...]
    s_scr[...] = s0f_ref[0, 0]

    def fwd(n, carry):
        q, k, v = q_ref[0, rows(n), :], k_ref[0, rows(n), :], v_ref[0, rows(n), :]
        qf, kf = q.astype(F32), k.astype(F32)
        s = s_scr[...]
        att = (_dot(q, k.T) * decay).astype(BF16)
        o = (_dot(att, v) + _dot((qf * qd_f).astype(BF16), s.astype(BF16))
             + _dot((qf * qd_b).astype(BF16), sb_scr[n]))
        mu = jnp.mean(o, axis=-1, keepdims=True)
        ctr = o - mu
        var = jnp.mean(ctr * ctr, axis=-1, keepdims=True)
        on = ctr * lax.rsqrt(var + EPS) * gn_ref[...]
        y_ref[0, rows(n), :] = (g_ref[0, rows(n), :].astype(F32) * on).astype(BF16)
        s_scr[...] = s * cd_f + _dot_tn((kf * kd_f).astype(BF16), v)
        return carry

    lax.fori_loop(0, nc, fwd, 0)
    sf_ref[0, 0] = s_scr[...]


def _ret_mix(dec_f, dec_b, q, k, v, gate, gn_g, s0_f, s0_b):
    bsz, t, _ = q.shape
    nh, dk, dv = RET_HEADS, RET_QK_DIM, RET_V_DIM
    chunk = min(RET_CHUNK, t)
    seq = lambda n: pl.BlockSpec((1, t, n), lambda b, h: (b, 0, h))
    dec = pl.BlockSpec((1, 1, LANES), lambda b, h: (h, 0, 0))
    state = pl.BlockSpec((1, 1, dk, dv), lambda b, h: (b, h, 0, 0))
    return pl.pallas_call(
        functools.partial(_ret_mix_kernel, chunk=chunk),
        grid=(bsz, nh),
        in_specs=[dec, dec, seq(dk), seq(dk), seq(dv), seq(dv),
                  pl.BlockSpec((1, dv), lambda b, h: (0, h)), state, state],
        out_specs=[seq(dv), state, state],
        out_shape=[jax.ShapeDtypeStruct((bsz, t, nh * dv), BF16),
                   jax.ShapeDtypeStruct((bsz, nh, dk, dv), F32),
                   jax.ShapeDtypeStruct((bsz, nh, dk, dv), F32)],
        scratch_shapes=[pltpu.VMEM((dk, dv), F32), pltpu.VMEM((t // chunk, dk, dv), BF16)],
        compiler_params=_params("parallel", "parallel"),
        name="ret_mix",
    )(dec_f, dec_b, q, k, v, gate, gn_g.reshape(1, nh * dv), s0_f, s0_b)


def _qkv_proj_kernel(h_ref, mod_ref, g_ref, w_ref, *rest, widths, q_scale, rope):
    if rope:
        cos_ref, sin_ref = rest[:2]
        outs = rest[2:]
    else:
        outs = rest
    shift, scale, _ = _mod_rows(mod_ref, 1)
    xb = _ada(h_ref[0], g_ref[...], shift, scale).astype(BF16)
    col = 0
    for idx, (o_ref, n) in enumerate(zip(outs, widths)):
        if o_ref is not None:
            y = _dot(xb, w_ref[:, col:col + n])
            for c0 in range(0, n, LANES):
                yc = y[:, c0:c0 + LANES]
                if rope and idx < 2:
                    lane = lax.broadcasted_iota(jnp.int32, yc.shape, 1)
                    partner = jnp.where((lane & 16) == 0, pltpu.roll(yc, LANES - 16, axis=1),
                                        pltpu.roll(yc, 16, axis=1))
                    yc = yc * cos_ref[...] + partner * sin_ref[...]
                if idx == 0:
                    yc = yc * q_scale
                o_ref[0, :, c0:c0 + LANES] = yc.astype(BF16)
        col += n


def _qkv_proj(h, mod, g, w, widths, q_scale, cos=None, sin=None, want_q=True):
    bsz, t, d = h.shape
    tm = TOKEN_TILE
    rope = cos is not None
    row = lambda n: pl.BlockSpec((1, tm, n), lambda b, i: (b, i, 0))
    in_specs = [row(d), _mod_spec(mod, d), pl.BlockSpec((1, d), lambda b, i: (0, 0)), _resident(w.shape)]
    args = [h, mod, g.reshape(1, d), w]
    if rope:
        in_specs += [pl.BlockSpec((tm, LANES), lambda b, i: (i, 0))] * 2
        args += [cos, sin]
    keep = [want_q, True, True]
    out_widths = [n for n, kp in zip(widths, keep) if kp]

    def body(*refs):
        n_in = len(args)
        outs = list(refs[n_in:])
        full = [outs.pop(0) if kp else None for kp in keep]
        _qkv_proj_kernel(*refs[:n_in], *full, widths=widths, q_scale=q_scale, rope=rope)

    res = pl.pallas_call(
        body,
        grid=(bsz, t // tm),
        in_specs=in_specs,
        out_specs=[row(n) for n in out_widths],
        out_shape=[jax.ShapeDtypeStruct((bsz, t, n), BF16) for n in out_widths],
        compiler_params=_params("parallel", "parallel"),
        name="qkv_proj",
    )(*args)
    return res if want_q else [None] + list(res)


def _split_heads(x):
    lane = lax.broadcasted_iota(jnp.int32, x.shape, 1)
    zero = jnp.zeros_like(x)
    return jnp.concatenate([jnp.where(lane < 64, x, zero), jnp.where(lane >= 64, x, zero)], axis=0)


def _merge_heads(o):
    r = o.shape[0] // 2
    lane = lax.broadcasted_iota(jnp.int32, (r, o.shape[1]), 1)
    return jnp.where(lane < 64, o[:r], o[r:])


def _stage_transposed(dst_ref, src_ref, row0, width):
    for c0 in range(0, width, XPOSE_CHUNK):
        n = min(XPOSE_CHUNK, width - c0)
        dst_ref[:, c0:c0 + n] = src_ref[0, row0 + c0:row0 + c0 + n, :].T


def _nat_kernel(q_ref, k_ref, v_ref, kc_ref, vc_ref, bias_ref, o_ref, kt_scr, *, rows):
    w = GRID_W
    kh = min(NAT_KH, rows)
    nk = kh * w
    t = rows * w
    _stage_transposed(kt_scr.at[0], k_ref, 0, t)
    _stage_transposed(kt_scr.at[1], k_ref, w, t - 2 * w)
    kct = kc_ref[0].T
    vc = vc_ref[0]

    def body(r, carry):
        r0 = jnp.clip(r - kh // 2, 0, rows - kh)
        ty = r - r0
        par = r0 & 1
        qs = pl.multiple_of(r * w, w)
        ks = pl.multiple_of(r0 * w, w)
        kts = pl.multiple_of((r0 - par) * w, 2 * w)
        q2 = _split_heads(q_ref[0, pl.ds(qs, w), :])
        kwt = kt_scr[par, :, pl.ds(kts, nk)]
        vw = v_ref[0, pl.ds(ks, nk), :]
        bias = jnp.concatenate([bias_ref[0, ty], bias_ref[1, ty]], axis=0)
        s_nb = _dot(q2, kwt) + bias
        s_cx = _dot(q2, kct)
        m = jnp.maximum(jnp.max(s_nb, axis=-1, keepdims=True), jnp.max(s_cx, axis=-1, keepdims=True))
        p_nb = jnp.exp(s_nb - m)
        p_cx = jnp.exp(s_cx - m)
        l = jnp.sum(p_nb, axis=-1, keepdims=True) + jnp.sum(p_cx, axis=-1, keepdims=True)
        o = (_dot(p_nb.astype(BF16), vw) + _dot(p_cx.astype(BF16), vc)) / l
        o_ref[0, pl.ds(qs, w), :] = _merge_heads(o).astype(BF16)
        return carry

    lax.fori_loop(0, rows, body, 0, unroll=4)


def _nat_attention(q, k, v, kc, vc, bias):
    bsz, t, d = q.shape
    l = kc.shape[1]
    rows = t // GRID_W
    hp = d // LANES
    lat = pl.BlockSpec((1, t, LANES), lambda p, b: (b, 0, p))
    ctx = pl.BlockSpec((1, l, LANES), lambda p, b: (b, 0, p))
    return pl.pallas_call(
        functools.partial(_nat_kernel, rows=rows),
        grid=(hp, bsz),
        in_specs=[lat, lat, lat, ctx, ctx,
                  pl.BlockSpec((2,) + bias.shape[1:], lambda p, b: (p, 0, 0, 0))],
        out_specs=lat,
        out_shape=jax.ShapeDtypeStruct((bsz, t, d), BF16),
        scratch_shapes=[pltpu.VMEM((2, LANES, t), BF16)],
        compiler_params=_params("parallel", "parallel"),
        name="nat_attention",
    )(q, k, v, kc, vc, bias)


def _nat_bias(rpb, rows):
    kh = min(NAT_KH, rows)
    w = GRID_W
    rpb = rpb.astype(F32)
    by_row = jnp.stack([rpb[:, NAT_KH - 1 - ty:NAT_KH - 1 - ty + kh, :] for ty in range(kh)], axis=1)
    padded = jnp.pad(by_row, ((0, 0), (0, 0), (0, 0), (w - NAT_KW, w - NAT_KW)))
    vals = jnp.stack([padded[..., w - 1 - c:2 * w - 1 - c] for c in range(w)], axis=2)
    c = jnp.arange(w)[:, None, None]
    cc = jnp.arange(w)[None, None, :]
    c0 = jnp.clip(c - NAT_KW // 2, 0, w - NAT_KW)
    inside = (cc >= c0) & (cc < c0 + NAT_KW)
    return jnp.where(inside[None, None], vals, NEG_INF).reshape(rpb.shape[0], kh, w, kh * w)


def _ctx_attn_kernel(q_ref, k_ref, v_ref, o_ref):
    q2 = _split_heads(q_ref[0])
    s = _dot(q2, k_ref[0].T)
    p = jnp.exp(s - jnp.max(s, axis=-1, keepdims=True))
    l = jnp.sum(p, axis=-1, keepdims=True)
    o_ref[0] = _merge_heads(_dot(p.astype(BF16), v_ref[0]) / l).astype(BF16)


def _ctx_attention(q, k, v):
    bsz, l, d = q.shape
    blk = pl.BlockSpec((1, l, LANES), lambda b, p: (b, 0, p))
    return pl.pallas_call(
        _ctx_attn_kernel,
        grid=(bsz, d // LANES),
        in_specs=[blk, blk, blk],
        out_specs=blk,
        out_shape=jax.ShapeDtypeStruct((bsz, l, d), BF16),
        compiler_params=_params("parallel", "parallel"),
        name="ctx_attention",
    )(q, k, v)


POOL_HALO = SUBLANES
POOL_K = 384


def _pool_kernel(h_ref, prev_ref, next_ref, mod_ref, g_ref, pw_ref, ps_ref, o_ref, *, tm, t_total):
    i = pl.program_id(1)
    x = h_ref[0]
    d = x.shape[-1]
    shift, scale, gate = _mod_rows(mod_ref, 1)
    pad = jnp.zeros((POOL_K - tm - 2 * POOL_HALO, d), F32)
    xe = _ada(jnp.concatenate([prev_ref[0], x, next_ref[0]], axis=0), g_ref[...], shift, scale)
    xn = xe[POOL_HALO:POOL_HALO + tm]
    xe = jnp.concatenate([xe, pad], axis=0)
    x_hi = xe.astype(BF16)
    r1 = xe - x_hi.astype(F32)
    x_mid = r1.astype(BF16)
    x_lo = (r1 - x_mid.astype(F32)).astype(BF16)
    tpos = i * tm + lax.broadcasted_iota(jnp.int32, (tm, POOL_K), 0)
    upos = i * tm - POOL_HALO + lax.broadcasted_iota(jnp.int32, (tm, POOL_K), 1)
    trow = i * tm + lax.broadcasted_iota(jnp.int32, (tm, 1), 0)
    gd = d // len(POOL_WINDOWS)
    parts = []
    for gi, win in enumerate(POOL_WINDOWS):
        half = win // 2
        lo = jnp.maximum(tpos - half, 0)
        hi = jnp.minimum(tpos + half, t_total)
        band = jnp.where((upos >= lo) & (upos < hi), 1.0, 0.0).astype(BF16)
        cnt = (jnp.minimum(trow + half, t_total) - jnp.maximum(trow - half, 0)).astype(F32)
        cols = slice(gi * gd, (gi + 1) * gd)
        tot = _dot(band, x_hi[:, cols]) + _dot(band, x_mid[:, cols]) + _dot(band, x_lo[:, cols])
        pooled = (tot / cnt - xn[:, cols]).astype(BF16)
        parts.append(_dot(pooled, pw_ref[gi]))
    y = jnp.concatenate(parts, axis=-1) * ps_ref[...]
    o_ref[0] = x + gate * y


def _pool(h, mod, g, pw, ps):
    bsz, t, d = h.shape
    tm = TOKEN_TILE
    per = tm // POOL_HALO
    last = t // POOL_HALO - 1
    return pl.pallas_call(
        functools.partial(_pool_kernel, tm=tm, t_total=t),
        grid=(bsz, t // tm),
        in_specs=[pl.BlockSpec((1, tm, d), lambda b, i: (b, i, 0)),
                  pl.BlockSpec((1, POOL_HALO, d), lambda b, i: (b, jnp.maximum(i * per - 1, 0), 0)),
                  pl.BlockSpec((1, POOL_HALO, d), lambda b, i: (b, jnp.minimum((i + 1) * per, last), 0)),
                  _mod_spec(mod, d),
                  pl.BlockSpec((1, d), lambda b, i: (0, 0)),
                  _resident(pw.shape),
                  pl.BlockSpec((1, d), lambda b, i: (0, 0))],
        out_specs=pl.BlockSpec((1, tm, d), lambda b, i: (b, i, 0)),
        out_shape=jax.ShapeDtypeStruct((bsz, t, d), F32),
        compiler_params=_params("parallel", "parallel"),
        name="pool",
    )(h, h, h, mod, g.reshape(1, d), pw, ps.reshape(1, d))


def _swa_kernel(sink_ref, q_ref, k_ref, v_ref, kc_ref, vc_ref, o_ref, kt_scr, *, t_total):
    kv = pl.program_id(1)
    blk = SWA_BLOCK
    nb = t_total // blk
    grp = SWA_Q_HEADS // SWA_KV_HEADS
    nrow = grp * blk
    _stage_transposed(kt_scr, k_ref, 0, t_total)
    kct = kc_ref[0].T
    vc = vc_ref[0]
    row = lax.broadcasted_iota(jnp.int32, (nrow, 1), 0)
    sink = jnp.zeros((nrow, 1), F32)
    for gi in range(grp):
        sink = jnp.where((row >= gi * blk) & (row < (gi + 1) * blk), sink_ref[kv * grp + gi], sink)
    qi = lax.broadcasted_iota(jnp.int32, (nrow, blk), 0) & (blk - 1)
    kj = lax.broadcasted_iota(jnp.int32, (nrow, blk), 1)
    open_blk = jnp.zeros((nrow, blk), F32)
    prev_blk = jnp.where(kj >= qi, 0.0, NEG_INF)
    next_blk = jnp.where(kj <= qi, 0.0, NEG_INF)

    def step(n, first, last):
        qs = pl.multiple_of(n * blk, blk)
        if first:
            ks, mask = qs, jnp.concatenate([open_blk, next_blk], axis=1)
        elif last:
            ks, mask = qs - blk, jnp.concatenate([prev_blk, open_blk], axis=1)
        else:
            ks, mask = qs - blk, mask_mid
        ks = pl.multiple_of(ks, blk)
        span = mask.shape[1]
        qb = q_ref[0, pl.ds(qs, blk), :]
        q4 = jnp.concatenate([_split_heads(qb[:, :LANES]), _split_heads(qb[:, LANES:])], axis=0)
        s_loc = _dot(q4, kt_scr[:, pl.ds(ks, span)]) + mask
        s_cx = _dot(q4, kct)
        m = jnp.maximum(jnp.maximum(jnp.max(s_loc, axis=-1, keepdims=True),
                                    jnp.max(s_cx, axis=-1, keepdims=True)), sink)
        p_loc = jnp.exp(s_loc - m)
        p_cx = jnp.exp(s_cx - m)
        l = jnp.sum(p_loc, axis=-1, keepdims=True) + jnp.sum(p_cx, axis=-1, keepdims=True) + jnp.exp(sink - m)
        o = (_dot(p_loc.astype(BF16), v_ref[0, pl.ds(ks, span), :]) + _dot(p_cx.astype(BF16), vc)) / l
        out = jnp.concatenate([_merge_heads(o[:2 * blk]), _merge_heads(o[2 * blk:])], axis=-1)
        o_ref[0, pl.ds(qs, blk), :] = out.astype(BF16)

    step(0, True, False)
    mask_mid = jnp.concatenate([prev_blk, open_blk, next_blk], axis=1)

    def body(n, carry):
        step(n, False, False)
        return carry

    lax.fori_loop(1, nb - 1, body, 0)
    step(nb - 1, False, True)


def _swa_attention(sink, q, k, v, kc, vc):
    bsz, t, d = q.shape
    l = kc.shape[1]
    assert t // SWA_BLOCK >= 2 and SWA_WINDOW == SWA_BLOCK
    qw = d // SWA_KV_HEADS
    grid_spec = pltpu.PrefetchScalarGridSpec(
        num_scalar_prefetch=1,
        grid=(bsz, SWA_KV_HEADS),
        in_specs=[pl.BlockSpec((1, t, qw), lambda b, h, s: (b, 0, h)),
                  pl.BlockSpec((1, t, LANES), lambda b, h, s: (b, 0, h)),
                  pl.BlockSpec((1, t, LANES), lambda b, h, s: (b, 0, h)),
                  pl.BlockSpec((1, l, LANES), lambda b, h, s: (b, 0, h)),
                  pl.BlockSpec((1, l, LANES), lambda b, h, s: (b, 0, h))],
        out_specs=pl.BlockSpec((1, t, qw), lambda b, h, s: (b, 0, h)),
        scratch_shapes=[pltpu.VMEM((LANES, t), BF16)],
    )
    return pl.pallas_call(
        functools.partial(_swa_kernel, t_total=t),
        grid_spec=grid_spec,
        out_shape=jax.ShapeDtypeStruct((bsz, t, d), BF16),
        compiler_params=_params("parallel", "parallel"),
        name="swa_attention",
    )(sink, q, k, v, kc, vc)


def _rope_angles(positions, dim):
    seg = dim // len(positions)
    inv = ROPE_BASE ** (-jnp.arange(0, seg, 2, dtype=F32) / seg)
    return jnp.concatenate([jnp.tile(p.astype(F32)[:, None] * inv, (1, 2)) for p in positions], axis=-1)


def _dup_heads(w, heads, dh):
    d = w.shape[0]
    return jnp.broadcast_to(w.reshape(d, heads, 1, dh), (d, heads, 2, dh)).reshape(d, heads * 2 * dh)


def kernel(x, c, ctx, c_ctx, w_mod, b_mod, norm_g, ffn_w_in, ffn_w_out, ret_w_in, ret_w_out, ret_gn_g, ret_decay_f, ret_decay_b, nat_w_qkv, nat_w_o, nat_rpb, pool_w, pool_scale, swa_w_qkv, swa_w_o, swa_sink, final_norm_g):
    bsz, t, d = x.shape
    depth = w_mod.shape[0]
    cc = jnp.concatenate([c, c_ctx[None], jnp.zeros((SUBLANES - bsz - 1, d), F32)], axis=0)
    mods = _modulation(cc, w_mod, b_mod)
    h, hc = x, ctx
    for i in range(depth):
        kind, occ = i % N_MIXERS, i // N_MIXERS
        last = i == depth - 1
        ctx_live = (not last) or kind != 2
        ml = mods[i, :bsz].reshape(bsz, 9, d)
        mc = mods[i, bsz:bsz + 1].reshape(1, 9, d)
        w_in = [ffn_w_in[i, s].astype(BF16) for s in range(2)]
        w_out = [ffn_w_out[i, s].astype(BF16) for s in range(2)]
        h = _ffn(h, ml, norm_g[i, 0], w_in[0], w_out[0], 0)
        if ctx_live:
            hc = _ffn(hc, mc, norm_g[i, 0], w_in[0], w_out[0], 0)
        g1 = norm_g[i, 1]
        if kind == 0:
            assert not last, "retention as the last layer is not wired up"
            w = ret_w_in[occ].astype(BF16)
            wo = ret_w_out[occ].astype(BF16)
            ang = _rope_angles([jnp.arange(t)], RET_QK_DIM)[:, :RET_QK_DIM // 2]
            dec_f = jnp.broadcast_to(ret_decay_f[occ].astype(F32)[:, None, None], (RET_HEADS, 1, LANES))
            dec_b = jnp.broadcast_to(ret_decay_b[occ].astype(F32)[:, None, None], (RET_HEADS, 1, LANES))
            qc, kc, vc, gc = _ret_proj(hc, mc, g1, w)
            zeros = jnp.zeros((bsz, RET_HEADS, RET_QK_DIM, RET_V_DIM), F32)
            yc, s_f, s_b = _ret_mix(dec_f, dec_b, qc, kc, vc, gc, ret_gn_g[occ], zeros, zeros)
            ql, kl, vl, gl = _ret_proj(h, ml, g1, w, jnp.cos(ang), jnp.sin(ang))
            yl, _, _ = _ret_mix(dec_f, dec_b, ql, kl, vl, gl, ret_gn_g[occ], s_f, s_b)
            h = _out_proj(yl, wo, h, ml)
            hc = _out_proj(yc, wo, hc, mc)
        elif kind == 1:
            w = nat_w_qkv[occ].astype(BF16)
            wo = nat_w_o[occ].astype(BF16)
            widths = (d, d, d)
            qc, kc, vc = _qkv_proj(hc, mc, g1, w, widths, NAT_HEAD_DIM ** -0.5, want_q=not last)
            ql, kl, vl = _qkv_proj(h, ml, g1, w, widths, NAT_HEAD_DIM ** -0.5)
            ol = _nat_attention(ql, kl, vl, kc, vc, _nat_bias(nat_rpb[occ], t // GRID_W))
            h = _out_proj(ol, wo, h, ml)
            if not last:
                hc = _out_proj(_ctx_attention(qc, kc, vc), wo, hc, mc)
        elif kind == 2:
            pw = pool_w[occ].astype(BF16)
            h_new = _pool(h, ml, g1, pw, pool_scale[occ])
            if not last:
                hc = _pool(hc, mc, g1, pw, pool_scale[occ])
            h = h_new
        else:
            assert last, "windowed attention with live context outputs is not wired up"
            nq = SWA_Q_HEADS * SWA_HEAD_DIM
            nkv = SWA_KV_HEADS * SWA_HEAD_DIM
            wq = swa_w_qkv[occ][:, :nq]
            wk = _dup_heads(swa_w_qkv[occ][:, nq:nq + nkv], SWA_KV_HEADS, SWA_HEAD_DIM)
            wv = _dup_heads(swa_w_qkv[occ][:, nq + nkv:], SWA_KV_HEADS, SWA_HEAD_DIM)
            w = jnp.concatenate([wq, wk, wv], axis=-1).astype(BF16)
            wo = swa_w_o[occ].astype(BF16)
            widths = (nq, 2 * nkv, 2 * nkv)
            tt = jnp.arange(t)
            ang = _rope_angles([tt // GRID_W, tt % GRID_W], SWA_HEAD_DIM)
            ang = jnp.tile(ang, (1, LANES // SWA_HEAD_DIM))
            lane = jnp.arange(LANES)
            sin = jnp.where(lane % 32 < 16, -jnp.sin(ang), jnp.sin(ang))
            _, kc, vc = _qkv_proj(hc, mc, g1, w, widths, SWA_HEAD_DIM ** -0.5, want_q=False)
            ql, kl, vl = _qkv_proj(h, ml, g1, w, widths, SWA_HEAD_DIM ** -0.5, jnp.cos(ang), sin)
            ol = _swa_attention(swa_sink[occ].astype(F32), ql, kl, vl, kc, vc)
            h = _out_proj(ol, wo, h, ml)
        if not last:
            hc = _ffn(hc, mc, norm_g[i, 2], w_in[1], w_out[1], 2)
        h = _ffn(h, ml, norm_g[i, 2], w_in[1], w_out[1], 2, final_norm_g if last else None)
    return h
```

```python
import functools

import jax
import jax.numpy as jnp
from jax import lax
from jax.experimental import pallas as pl
from jax.experimental.pallas import tpu as pltpu

F32 = jnp.float32
BF16 = jnp.bfloat16

EPS = 1e-6
NEG_INF = -1e30
ROPE_BASE = 10000.0
GRID_W = 64
N_MIXERS = 4
FFN_HIDDEN = 2816
RET_HEADS = 4
RET_QK_DIM = 256
RET_V_DIM = 512
NAT_HEADS = 16
NAT_HEAD_DIM = 64
NAT_KH = 8
NAT_KW = 16
POOL_WINDOWS = (2, 4, 8, 16)
SWA_Q_HEADS = 16
SWA_KV_HEADS = 4
SWA_HEAD_DIM = 64
SWA_WINDOW = 128
SWA_BLOCK = 128

LANES = 128
SUBLANES = 8
VMEM_LIMIT = 56 * 1024 * 1024
TOKEN_TILE = 256
FFN_TILE = 512
RET_CHUNK = 256
XPOSE_CHUNK = 512


def _params(*sem):
    return pltpu.CompilerParams(dimension_semantics=sem, vmem_limit_bytes=VMEM_LIMIT)


def _resident(shape):
    nd = len(shape)
    return pl.BlockSpec(shape, lambda *_: (0,) * nd, pipeline_mode=pl.Buffered(1))


def _silu(x):
    return x * jax.nn.sigmoid(x)


def _ada(x, g, shift, scale):
    var = jnp.mean(x * x, axis=-1, keepdims=True)
    y = x * lax.rsqrt(var + EPS) * g
    return y * (1.0 + scale) + shift


def _mod_rows(mod_ref, j):
    return (mod_ref[0, 3 * j:3 * j + 1, :], mod_ref[0, 3 * j + 1:3 * j + 2, :],
            mod_ref[0, 3 * j + 2:3 * j + 3, :])


def _mod_spec(mod, d):
    if mod.shape[0] == 1:
        return pl.BlockSpec((1, 9, d), lambda b, t: (0, 0, 0))
    return pl.BlockSpec((1, 9, d), lambda b, t: (b, 0, 0))


def _dot(a, b):
    return jnp.dot(a, b, preferred_element_type=F32)


def _dot_nt(a, b):
    return lax.dot_general(a, b, (((1,), (1,)), ((), ())), preferred_element_type=F32)


def _dot_tn(a, b):
    return lax.dot_general(a, b, (((0,), (0,)), ((), ())), preferred_element_type=F32)


def _mod_kernel(c_ref, w_ref, b_ref, o_ref):
    s = _silu(c_ref[...])
    o_ref[0] = jnp.dot(s, w_ref[0], preferred_element_type=F32,
                       precision=lax.Precision.HIGHEST) + b_ref[0]


def _modulation(cc, w_mod, b_mod):
    depth, d, n = w_mod.shape
    tn = 2304
    return pl.pallas_call(
        _mod_kernel,
        grid=(depth, n // tn),
        in_specs=[pl.BlockSpec((SUBLANES, d), lambda l, j: (0, 0)),
                  pl.BlockSpec((1, d, tn), lambda l, j: (l, 0, j)),
                  pl.BlockSpec((1, 1, tn), lambda l, j: (l, 0, j))],
        out_specs=pl.BlockSpec((1, SUBLANES, tn), lambda l, j: (l, 0, j)),
        out_shape=jax.ShapeDtypeStruct((depth, SUBLANES, n), F32),
        compiler_params=_params("arbitrary", "arbitrary"),
        name="modulation",
    )(cc, w_mod, b_mod.reshape(depth, 1, n))


def _ffn_kernel(h_ref, mod_ref, g_ref, win_ref, wout_ref, *rest, j, final):
    o_ref = rest[-1]
    x = h_ref[0]
    shift, scale, gate = _mod_rows(mod_ref, j)
    xb = _ada(x, g_ref[...], shift, scale).astype(BF16)
    f = FFN_HIDDEN
    a = _dot(xb, win_ref[:, :f])
    b = _dot(xb, win_ref[:, f:])
    hid = (_silu(a) * b).astype(BF16)
    y = _dot(hid, wout_ref[...])
    out = x + (0.5 * gate) * y
    if final:
        var = jnp.mean(out * out, axis=-1, keepdims=True)
        out = out * lax.rsqrt(var + EPS) * rest[0][...]
    o_ref[0] = out


def _ffn(h, mod, g, w_in, w_out, j, final_g=None):
    bsz, t, d = h.shape
    tm = min(FFN_TILE, t)
    final = final_g is not None
    in_specs = [pl.BlockSpec((1, tm, d), lambda b, i: (b, i, 0)),
                _mod_spec(mod, d),
                pl.BlockSpec((1, d), lambda b, i: (0, 0)),
                _resident(w_in.shape),
                _resident(w_out.shape)]
    args = [h, mod, g.reshape(1, d), w_in, w_out]
    if final:
        in_specs.append(pl.BlockSpec((1, d), lambda b, i: (0, 0)))
        args.append(final_g.reshape(1, d))
    return pl.pallas_call(
        functools.partial(_ffn_kernel, j=j, final=final),
        grid=(bsz, t // tm),
        in_specs=in_specs,
        out_specs=pl.BlockSpec((1, tm, d), lambda b, i: (b, i, 0)),
        out_shape=jax.ShapeDtypeStruct((bsz, t, d), F32),
        compiler_params=_params("parallel", "parallel"),
        name="ffn",
    )(*args)


def _out_proj_kernel(y_ref, w_ref, h_ref, mod_ref, o_ref):
    gate = mod_ref[0, 5:6, :]
    o_ref[0] = h_ref[0] + gate * _dot(y_ref[0], w_ref[...])


def _out_proj(y, w, h, mod):
    bsz, t, d = h.shape
    tm = min(t, 2 * TOKEN_TILE)
    k = y.shape[-1]
    return pl.pallas_call(
        _out_proj_kernel,
        grid=(bsz, t // tm),
        in_specs=[pl.BlockSpec((1, tm, k), lambda b, i: (b, i, 0)),
                  _resident(w.shape),
                  pl.BlockSpec((1, tm, d), lambda b, i: (b, i, 0)),
                  _mod_spec(mod, d)],
        out_specs=pl.BlockSpec((1, tm, d), lambda b, i: (b, i, 0)),
        out_shape=jax.ShapeDtypeStruct((bsz, t, d), F32),
        compiler_params=_params("parallel", "parallel"),
        name="out_proj",
    )(y, w, h, mod)


def _ret_proj_kernel(h_ref, mod_ref, g_ref, w_ref, *rest, rope):
    if rope:
        cos_ref, sin_ref, q_ref, k_ref, v_ref, gate_ref = rest
    else:
        q_ref, k_ref, v_ref, gate_ref = rest
    shift, scale, _ = _mod_rows(mod_ref, 1)
    xb = _ada(h_ref[0], g_ref[...], shift, scale).astype(BF16)
    dk, nh = RET_QK_DIM, RET_HEADS
    half = dk // 2

    def rot(x):
        if not rope:
            return x.astype(BF16)
        c, s = cos_ref[...], sin_ref[...]
        x1, x2 = x[:, :half], x[:, half:]
        return jnp.concatenate([x1 * c - x2 * s, x2 * c + x1 * s], axis=-1).astype(BF16)

    for hd in range(nh):
        q_ref[0, :, hd * dk:(hd + 1) * dk] = rot(_dot(xb, w_ref[:, hd * dk:(hd + 1) * dk]))
        kcol = nh * dk + hd * dk
        k_ref[0, :, hd * dk:(hd + 1) * dk] = rot(_dot(xb, w_ref[:, kcol:kcol + dk]) * dk ** -0.5)
    v0 = 2 * nh * dk
    nv = nh * RET_V_DIM
    v_ref[0] = _dot(xb, w_ref[:, v0:v0 + nv]).astype(BF16)
    gate_ref[0] = _silu(_dot(xb, w_ref[:, v0 + nv:v0 + 2 * nv])).astype(BF16)


def _ret_proj(h, mod, g, w, cos=None, sin=None):
    bsz, t, d = h.shape
    tm = TOKEN_TILE
    rope = cos is not None
    nqk = RET_HEADS * RET_QK_DIM
    nv = RET_HEADS * RET_V_DIM
    row = lambda n: pl.BlockSpec((1, tm, n), lambda b, i: (b, i, 0))
    in_specs = [row(d), _mod_spec(mod, d), pl.BlockSpec((1, d), lambda b, i: (0, 0)), _resident(w.shape)]
    args = [h, mod, g.reshape(1, d), w]
    if rope:
        in_specs += [pl.BlockSpec((tm, RET_QK_DIM // 2), lambda b, i: (i, 0))] * 2
        args += [cos, sin]
    return pl.pallas_call(
        functools.partial(_ret_proj_kernel, rope=rope),
        grid=(bsz, t // tm),
        in_specs=in_specs,
        out_specs=[row(nqk), row(nqk), row(nv), row(nv)],
        out_shape=[jax.ShapeDtypeStruct((bsz, t, nqk), BF16), jax.ShapeDtypeStruct((bsz, t, nqk), BF16),
                   jax.ShapeDtypeStruct((bsz, t, nv), BF16), jax.ShapeDtypeStruct((bsz, t, nv), BF16)],
        compiler_params=_params("parallel", "parallel"),
        name="ret_proj",
    )(*args)


def _log_sigmoid(x):
    return jnp.minimum(x, 0.0) - jnp.log(1.0 + jnp.exp(-jnp.abs(x)))


def _ret_mix_kernel(decf_ref, decb_ref, q_ref, k_ref, v_ref, g_ref, gn_ref, s0f_ref, s0b_ref,
                    y_ref, sf_ref, sb_ref, s_scr, sb_scr, *, chunk):
    c = chunk
    nc = q_ref.shape[1] // c
    lgf = _log_sigmoid(decf_ref[0])[:, :1]
    lgb = _log_sigmoid(decb_ref[0])[:, :1]
    dist = lax.broadcasted_iota(jnp.int32, (c, c), 0) - lax.broadcasted_iota(jnp.int32, (c, c), 1)
    decay = jnp.where(dist >= 0,
                      jnp.exp(jnp.maximum(dist, 0).astype(F32) * lgf),
                      jnp.exp(jnp.maximum(-dist, 0).astype(F32) * lgb))
    pos = lax.broadcasted_iota(jnp.int32, (c, 1), 0).astype(F32)
    qd_f, kd_f, cd_f = jnp.exp((pos + 1.0) * lgf), jnp.exp((c - 1.0 - pos) * lgf), jnp.exp(c * lgf)
    qd_b, kd_b, cd_b = jnp.exp((c - pos) * lgb), jnp.exp(pos * lgb), jnp.exp(c * lgb)

    def rows(n):
        return pl.ds(pl.multiple_of(n * c, c), c)

    s_scr[...] = s0b_ref[0, 0]

    def bwd(i, carry):
        n = nc - 1 - i
        s = s_scr[...]
        sb_scr[n] = s.astype(BF16)
        kd = (k_ref[0, rows(n), :].astype(F32) * kd_b).astype(BF16)
        s_scr[...] = s * cd_b + _dot_tn(kd, v_ref[0, rows(n), :])
        return carry

    lax.fori_loop(0, nc, bwd, 0, unroll=min(4, nc))
    sb_ref[0, 0] = s_scr[...]
    s_scr[...] = s0f_ref[0, 0]

    def fwd(n, carry):
        q, k, v = q_ref[0, rows(n), :], k_ref[0, rows(n), :], v_ref[0, rows(n), :]
        qf, kf = q.astype(F32), k.astype(F32)
        s = s_scr[...]
        att = (_dot(q, k.T) * decay).astype(BF16)
        o = (_dot(att, v) + _dot((qf * qd_f).astype(BF16), s.astype(BF16))
             + _dot((qf * qd_b).astype(BF16), sb_scr[n]))
        mu = jnp.mean(o, axis=-1, keepdims=True)
        ctr = o - mu
        var = jnp.mean(ctr * ctr, axis=-1, keepdims=True)
        on = ctr * lax.rsqrt(var + EPS) * gn_ref[...]
        y_ref[0, rows(n), :] = (g_ref[0, rows(n), :].astype(F32) * on).astype(BF16)
        s_scr[...] = s * cd_f + _dot_tn((kf * kd_f).astype(BF16), v)
        return carry

    lax.fori_loop(0, nc, fwd, 0, unroll=min(4, nc))
    sf_ref[0, 0] = s_scr[...]


def _ret_mix(dec_f, dec_b, q, k, v, gate, gn_g, s0_f, s0_b):
    bsz, t, _ = q.shape
    nh, dk, dv = RET_HEADS, RET_QK_DIM, RET_V_DIM
    chunk = min(RET_CHUNK, t)
    seq = lambda n: pl.BlockSpec((1, t, n), lambda b, h: (b, 0, h))
    dec = pl.BlockSpec((1, 1, LANES), lambda b, h: (h, 0, 0))
    state = pl.BlockSpec((1, 1, dk, dv), lambda b, h: (b, h, 0, 0))
    return pl.pallas_call(
        functools.partial(_ret_mix_kernel, chunk=chunk),
        grid=(bsz, nh),
        in_specs=[dec, dec, seq(dk), seq(dk), seq(dv), seq(dv),
                  pl.BlockSpec((1, dv), lambda b, h: (0, h)), state, state],
        out_specs=[seq(dv), state, state],
        out_shape=[jax.ShapeDtypeStruct((bsz, t, nh * dv), BF16),
                   jax.ShapeDtypeStruct((bsz, nh, dk, dv), F32),
                   jax.ShapeDtypeStruct((bsz, nh, dk, dv), F32)],
        scratch_shapes=[pltpu.VMEM((dk, dv), F32), pltpu.VMEM((t // chunk, dk, dv), BF16)],
        compiler_params=_params("parallel", "parallel"),
        name="ret_mix",
    )(dec_f, dec_b, q, k, v, gate, gn_g.reshape(1, nh * dv), s0_f, s0_b)


def _qkv_proj_kernel(h_ref, mod_ref, g_ref, w_ref, *rest, widths, q_scale, rope):
    if rope:
        cos_ref, sin_ref = rest[:2]
        outs = rest[2:]
    else:
        outs = rest
    shift, scale, _ = _mod_rows(mod_ref, 1)
    xb = _ada(h_ref[0], g_ref[...], shift, scale).astype(BF16)
    col = 0
    for idx, (o_ref, n) in enumerate(zip(outs, widths)):
        if o_ref is not None:
            y = _dot(xb, w_ref[:, col:col + n])
            for c0 in range(0, n, LANES):
                yc = y[:, c0:c0 + LANES]
                if rope and idx < 2:
                    lane = lax.broadcasted_iota(jnp.int32, yc.shape, 1)
                    partner = jnp.where((lane & 16) == 0, pltpu.roll(yc, LANES - 16, axis=1),
                                        pltpu.roll(yc, 16, axis=1))
                    yc = yc * cos_ref[...] + partner * sin_ref[...]
                if idx == 0:
                    yc = yc * q_scale
                o_ref[0, :, c0:c0 + LANES] = yc.astype(BF16)
        col += n


def _qkv_proj(h, mod, g, w, widths, q_scale, cos=None, sin=None, want_q=True):
    bsz, t, d = h.shape
    tm = TOKEN_TILE
    rope = cos is not None
    row = lambda n: pl.BlockSpec((1, tm, n), lambda b, i: (b, i, 0))
    in_specs = [row(d), _mod_spec(mod, d), pl.BlockSpec((1, d), lambda b, i: (0, 0)), _resident(w.shape)]
    args = [h, mod, g.reshape(1, d), w]
    if rope:
        in_specs += [pl.BlockSpec((tm, LANES), lambda b, i: (i, 0))] * 2
        args += [cos, sin]
    keep = [want_q, True, True]
    out_widths = [n for n, kp in zip(widths, keep) if kp]

    def body(*refs):
        n_in = len(args)
        outs = list(refs[n_in:])
        full = [outs.pop(0) if kp else None for kp in keep]
        _qkv_proj_kernel(*refs[:n_in], *full, widths=widths, q_scale=q_scale, rope=rope)

    res = pl.pallas_call(
        body,
        grid=(bsz, t // tm),
        in_specs=in_specs,
        out_specs=[row(n) for n in out_widths],
        out_shape=[jax.ShapeDtypeStruct((bsz, t, n), BF16) for n in out_widths],
        compiler_params=_params("parallel", "parallel"),
        name="qkv_proj",
    )(*args)
    return res if want_q else [None] + list(res)


def _split_heads(x):
    lane = lax.broadcasted_iota(jnp.int32, x.shape, 1)
    zero = jnp.zeros_like(x)
    return jnp.concatenate([jnp.where(lane < 64, x, zero), jnp.where(lane >= 64, x, zero)], axis=0)


def _merge_heads(o):
    r = o.shape[0] // 2
    lane = lax.broadcasted_iota(jnp.int32, (r, o.shape[1]), 1)
    return jnp.where(lane < 64, o[:r], o[r:])


def _stage_transposed(dst_ref, src_ref, row0, width):
    for c0 in range(0, width, XPOSE_CHUNK):
        n = min(XPOSE_CHUNK, width - c0)
        dst_ref[:, c0:c0 + n] = src_ref[0, row0 + c0:row0 + c0 + n, :].T


def _nat_kernel(q_ref, k_ref, v_ref, kc_ref, vc_ref, bias_ref, o_ref, kt_scr, *, rows):
    w = GRID_W
    kh = min(NAT_KH, rows)
    nk = kh * w
    t = rows * w
    _stage_transposed(kt_scr.at[0], k_ref, 0, t)
    _stage_transposed(kt_scr.at[1], k_ref, w, t - 2 * w)
    kct = kc_ref[0].T
    vc = vc_ref[0]

    def body(r, carry):
        r0 = jnp.clip(r - kh // 2, 0, rows - kh)
        ty = r - r0
        par = r0 & 1
        qs = pl.multiple_of(r * w, w)
        ks = pl.multiple_of(r0 * w, w)
        kts = pl.multiple_of((r0 - par) * w, 2 * w)
        q2 = _split_heads(q_ref[0, pl.ds(qs, w), :])
        kwt = kt_scr[par, :, pl.ds(kts, nk)]
        vw = v_ref[0, pl.ds(ks, nk), :]
        bias = jnp.concatenate(
            [jnp.concatenate([bias_ref[hd, 2 * m - ty + NAT_KH - 1] for m in range(kh // 2)], axis=1)
             for hd in range(2)], axis=0)
        s_nb = _dot(q2, kwt) + bias
        s_cx = _dot(q2, kct)
        m = jnp.maximum(jnp.max(s_nb, axis=-1, keepdims=True), jnp.max(s_cx, axis=-1, keepdims=True))
        p_nb = jnp.exp(s_nb - m)
        p_cx = jnp.exp(s_cx - m)
        l = jnp.sum(p_nb, axis=-1, keepdims=True) + jnp.sum(p_cx, axis=-1, keepdims=True)
        o = (_dot(p_nb.astype(BF16), vw) + _dot(p_cx.astype(BF16), vc)) / l
        o_ref[0, pl.ds(qs, w), :] = _merge_heads(o).astype(BF16)
        return carry

    lax.fori_loop(0, rows, body, 0, unroll=4)


def _nat_attention(q, k, v, kc, vc, bias):
    bsz, t, d = q.shape
    l = kc.shape[1]
    rows = t // GRID_W
    hp = d // LANES
    lat = pl.BlockSpec((1, t, LANES), lambda p, b: (b, 0, p))
    ctx = pl.BlockSpec((1, l, LANES), lambda p, b: (b, 0, p))
    return pl.pallas_call(
        functools.partial(_nat_kernel, rows=rows),
        grid=(hp, bsz),
        in_specs=[lat, lat, lat, ctx, ctx,
                  pl.BlockSpec((2,) + bias.shape[1:], lambda p, b: (p, 0, 0, 0))],
        out_specs=lat,
        out_shape=jax.ShapeDtypeStruct((bsz, t, d), BF16),
        scratch_shapes=[pltpu.VMEM((2, LANES, t), BF16)],
        compiler_params=_params("parallel", "parallel"),
        name="nat_attention",
    )(q, k, v, kc, vc, bias)


def _nat_bias(rpb):
    w = GRID_W
    nh, nr, ncol = rpb.shape
    left = w - NAT_KW
    v = jnp.pad(rpb.astype(F32), ((0, 0), (0, 0), (left, 2 * w - ncol - left)))
    toep = jnp.tile(v, (1, 1, w))[..., :w * (2 * w - 1)].reshape(nh, nr, w, 2 * w - 1)[..., w - 1:]
    c = jnp.arange(w)[:, None]
    cc = jnp.arange(w)[None, :]
    c0 = jnp.clip(c - NAT_KW // 2, 0, w - NAT_KW)
    blocks = jnp.where((cc >= c0) & (cc < c0 + NAT_KW), toep, NEG_INF)
    return jnp.concatenate([blocks[:, :-1], blocks[:, 1:]], axis=-1)


def _ctx_attn_kernel(q_ref, k_ref, v_ref, o_ref):
    q2 = _split_heads(q_ref[0])
    s = _dot(q2, k_ref[0].T)
    p = jnp.exp(s - jnp.max(s, axis=-1, keepdims=True))
    l = jnp.sum(p, axis=-1, keepdims=True)
    o_ref[0] = _merge_heads(_dot(p.astype(BF16), v_ref[0]) / l).astype(BF16)


def _ctx_attention(q, k, v):
    bsz, l, d = q.shape
    blk = pl.BlockSpec((1, l, LANES), lambda b, p: (b, 0, p))
    return pl.pallas_call(
        _ctx_attn_kernel,
        grid=(bsz, d // LANES),
        in_specs=[blk, blk, blk],
        out_specs=blk,
        out_shape=jax.ShapeDtypeStruct((bsz, l, d), BF16),
        compiler_params=_params("parallel", "parallel"),
        name="ctx_attention",
    )(q, k, v)


POOL_HALO = SUBLANES
POOL_K = 384


def _pool_kernel(h_ref, prev_ref, next_ref, mod_ref, g_ref, pw_ref, ps_ref, o_ref, *, tm, t_total):
    i = pl.program_id(1)
    x = h_ref[0]
    d = x.shape[-1]
    shift, scale, gate = _mod_rows(mod_ref, 1)
    pad = jnp.zeros((POOL_K - tm - 2 * POOL_HALO, d), F32)
    xe = _ada(jnp.concatenate([prev_ref[0], x, next_ref[0]], axis=0), g_ref[...], shift, scale)
    xn = xe[POOL_HALO:POOL_HALO + tm]
    xe = jnp.concatenate([xe, pad], axis=0)
    x_hi = xe.astype(BF16)
    r1 = xe - x_hi.astype(F32)
    x_mid = r1.astype(BF16)
    x_lo = (r1 - x_mid.astype(F32)).astype(BF16)
    tpos = i * tm + lax.broadcasted_iota(jnp.int32, (tm, POOL_K), 0)
    upos = i * tm - POOL_HALO + lax.broadcasted_iota(jnp.int32, (tm, POOL_K), 1)
    trow = i * tm + lax.broadcasted_iota(jnp.int32, (tm, 1), 0)
    gd = d // len(POOL_WINDOWS)
    parts = []
    for gi, win in enumerate(POOL_WINDOWS):
        half = win // 2
        lo = jnp.maximum(tpos - half, 0)
        hi = jnp.minimum(tpos + half, t_total)
        band = jnp.where((upos >= lo) & (upos < hi), 1.0, 0.0).astype(BF16)
        cnt = (jnp.minimum(trow + half, t_total) - jnp.maximum(trow - half, 0)).astype(F32)
        cols = slice(gi * gd, (gi + 1) * gd)
        tot = _dot(band, x_hi[:, cols]) + _dot(band, x_mid[:, cols]) + _dot(band, x_lo[:, cols])
        pooled = (tot / cnt - xn[:, cols]).astype(BF16)
        parts.append(_dot(pooled, pw_ref[gi]))
    y = jnp.concatenate(parts, axis=-1) * ps_ref[...]
    o_ref[0] = x + gate * y


def _pool(h, mod, g, pw, ps):
    bsz, t, d = h.shape
    tm = TOKEN_TILE
    per = tm // POOL_HALO
    last = t // POOL_HALO - 1
    return pl.pallas_call(
        functools.partial(_pool_kernel, tm=tm, t_total=t),
        grid=(bsz, t // tm),
        in_specs=[pl.BlockSpec((1, tm, d), lambda b, i: (b, i, 0)),
                  pl.BlockSpec((1, POOL_HALO, d), lambda b, i: (b, jnp.maximum(i * per - 1, 0), 0)),
                  pl.BlockSpec((1, POOL_HALO, d), lambda b, i: (b, jnp.minimum((i + 1) * per, last), 0)),
                  _mod_spec(mod, d),
                  pl.BlockSpec((1, d), lambda b, i: (0, 0)),
                  _resident(pw.shape),
                  pl.BlockSpec((1, d), lambda b, i: (0, 0))],
        out_specs=pl.BlockSpec((1, tm, d), lambda b, i: (b, i, 0)),
        out_shape=jax.ShapeDtypeStruct((bsz, t, d), F32),
        compiler_params=_params("parallel", "parallel"),
        name="pool",
    )(h, h, h, mod, g.reshape(1, d), pw, ps.reshape(1, d))


def _swa_kernel(sink_ref, q_ref, k_ref, v_ref, kc_ref, vc_ref, o_ref, kt_scr, *, t_total):
    kv = pl.program_id(1)
    blk = SWA_BLOCK
    nb = t_total // blk
    grp = SWA_Q_HEADS // SWA_KV_HEADS
    nrow = grp * blk
    _stage_transposed(kt_scr, k_ref, 0, t_total)
    kct = kc_ref[0].T
    vc = vc_ref[0]
    row = lax.broadcasted_iota(jnp.int32, (nrow, 1), 0)
    sink = jnp.zeros((nrow, 1), F32)
    for gi in range(grp):
        sink = jnp.where((row >= gi * blk) & (row < (gi + 1) * blk), sink_ref[kv * grp + gi], sink)
    qi = lax.broadcasted_iota(jnp.int32, (nrow, blk), 0) & (blk - 1)
    kj = lax.broadcasted_iota(jnp.int32, (nrow, blk), 1)
    open_blk = jnp.zeros((nrow, blk), F32)
    prev_blk = jnp.where(kj >= qi, 0.0, NEG_INF)
    next_blk = jnp.where(kj <= qi, 0.0, NEG_INF)

    def step(n, first, last):
        qs = pl.multiple_of(n * blk, blk)
        if first:
            ks, mask = qs, jnp.concatenate([open_blk, next_blk], axis=1)
        elif last:
            ks, mask = qs - blk, jnp.concatenate([prev_blk, open_blk], axis=1)
        else:
            ks, mask = qs - blk, mask_mid
        ks = pl.multiple_of(ks, blk)
        span = mask.shape[1]
        qb = q_ref[0, pl.ds(qs, blk), :]
        q4 = jnp.concatenate([_split_heads(qb[:, :LANES]), _split_heads(qb[:, LANES:])], axis=0)
        s_loc = _dot(q4, kt_scr[:, pl.ds(ks, span)]) + mask
        s_cx = _dot(q4, kct)
        m = jnp.maximum(jnp.maximum(jnp.max(s_loc, axis=-1, keepdims=True),
                                    jnp.max(s_cx, axis=-1, keepdims=True)), sink)
        p_loc = jnp.exp(s_loc - m)
        p_cx = jnp.exp(s_cx - m)
        l = jnp.sum(p_loc, axis=-1, keepdims=True) + jnp.sum(p_cx, axis=-1, keepdims=True) + jnp.exp(sink - m)
        o = (_dot(p_loc.astype(BF16), v_ref[0, pl.ds(ks, span), :]) + _dot(p_cx.astype(BF16), vc)) / l
        out = jnp.concatenate([_merge_heads(o[:2 * blk]), _merge_heads(o[2 * blk:])], axis=-1)
        o_ref[0, pl.ds(qs, blk), :] = out.astype(BF16)

    step(0, True, False)
    mask_mid = jnp.concatenate([prev_blk, open_blk, next_blk], axis=1)

    def body(n, carry):
        step(n, False, False)
        return carry

    lax.fori_loop(1, nb - 1, body, 0, unroll=3)
    step(nb - 1, False, True)


def _swa_attention(sink, q, k, v, kc, vc):
    bsz, t, d = q.shape
    l = kc.shape[1]
    assert t // SWA_BLOCK >= 2 and SWA_WINDOW == SWA_BLOCK
    qw = d // SWA_KV_HEADS
    grid_spec = pltpu.PrefetchScalarGridSpec(
        num_scalar_prefetch=1,
        grid=(bsz, SWA_KV_HEADS),
        in_specs=[pl.BlockSpec((1, t, qw), lambda b, h, s: (b, 0, h)),
                  pl.BlockSpec((1, t, LANES), lambda b, h, s: (b, 0, h)),
                  pl.BlockSpec((1, t, LANES), lambda b, h, s: (b, 0, h)),
                  pl.BlockSpec((1, l, LANES), lambda b, h, s: (b, 0, h)),
                  pl.BlockSpec((1, l, LANES), lambda b, h, s: (b, 0, h))],
        out_specs=pl.BlockSpec((1, t, qw), lambda b, h, s: (b, 0, h)),
        scratch_shapes=[pltpu.VMEM((LANES, t), BF16)],
    )
    return pl.pallas_call(
        functools.partial(_swa_kernel, t_total=t),
        grid_spec=grid_spec,
        out_shape=jax.ShapeDtypeStruct((bsz, t, d), BF16),
        compiler_params=_params("parallel", "parallel"),
        name="swa_attention",
    )(sink, q, k, v, kc, vc)


def _rope_angles(positions, dim):
    seg = dim // len(positions)
    inv = ROPE_BASE ** (-jnp.arange(0, seg, 2, dtype=F32) / seg)
    return jnp.concatenate([jnp.tile(p.astype(F32)[:, None] * inv, (1, 2)) for p in positions], axis=-1)


def _dup_heads(w, heads, dh):
    d = w.shape[0]
    return jnp.broadcast_to(w.reshape(d, heads, 1, dh), (d, heads, 2, dh)).reshape(d, heads * 2 * dh)


def kernel(x, c, ctx, c_ctx, w_mod, b_mod, norm_g, ffn_w_in, ffn_w_out, ret_w_in, ret_w_out, ret_gn_g, ret_decay_f, ret_decay_b, nat_w_qkv, nat_w_o, nat_rpb, pool_w, pool_scale, swa_w_qkv, swa_w_o, swa_sink, final_norm_g):
    bsz, t, d = x.shape
    depth = w_mod.shape[0]
    cc = jnp.concatenate([c, c_ctx[None], jnp.zeros((SUBLANES - bsz - 1, d), F32)], axis=0)
    mods = _modulation(cc, w_mod, b_mod)
    h, hc = x, ctx
    for i in range(depth):
        kind, occ = i % N_MIXERS, i // N_MIXERS
        last = i == depth - 1
        ctx_live = (not last) or kind != 2
        ml = mods[i, :bsz].reshape(bsz, 9, d)
        mc = mods[i, bsz:bsz + 1].reshape(1, 9, d)
        w_in = [ffn_w_in[i, s].astype(BF16) for s in range(2)]
        w_out = [ffn_w_out[i, s].astype(BF16) for s in range(2)]
        h = _ffn(h, ml, norm_g[i, 0], w_in[0], w_out[0], 0)
        if ctx_live:
            hc = _ffn(hc, mc, norm_g[i, 0], w_in[0], w_out[0], 0)
        g1 = norm_g[i, 1]
        if kind == 0:
            assert not last, "retention as the last layer is not wired up"
            w = ret_w_in[occ].astype(BF16)
            wo = ret_w_out[occ].astype(BF16)
            ang = _rope_angles([jnp.arange(t)], RET_QK_DIM)[:, :RET_QK_DIM // 2]
            dec_f = jnp.broadcast_to(ret_decay_f[occ].astype(F32)[:, None, None], (RET_HEADS, 1, LANES))
            dec_b = jnp.broadcast_to(ret_decay_b[occ].astype(F32)[:, None, None], (RET_HEADS, 1, LANES))
            qc, kc, vc, gc = _ret_proj(hc, mc, g1, w)
            zeros = jnp.zeros((bsz, RET_HEADS, RET_QK_DIM, RET_V_DIM), F32)
            yc, s_f, s_b = _ret_mix(dec_f, dec_b, qc, kc, vc, gc, ret_gn_g[occ], zeros, zeros)
            ql, kl, vl, gl = _ret_proj(h, ml, g1, w, jnp.cos(ang), jnp.sin(ang))
            yl, _, _ = _ret_mix(dec_f, dec_b, ql, kl, vl, gl, ret_gn_g[occ], s_f, s_b)
            h = _out_proj(yl, wo, h, ml)
            hc = _out_proj(yc, wo, hc, mc)
        elif kind == 1:
            w = nat_w_qkv[occ].astype(BF16)
            wo = nat_w_o[occ].astype(BF16)
            widths = (d, d, d)
            qc, kc, vc = _qkv_proj(hc, mc, g1, w, widths, NAT_HEAD_DIM ** -0.5, want_q=not last)
            ql, kl, vl = _qkv_proj(h, ml, g1, w, widths, NAT_HEAD_DIM ** -0.5)
            ol = _nat_attention(ql, kl, vl, kc, vc, _nat_bias(nat_rpb[occ]))
            h = _out_proj(ol, wo, h, ml)
            if not last:
                hc = _out_proj(_ctx_attention(qc, kc, vc), wo, hc, mc)
        elif kind == 2:
            pw = pool_w[occ].astype(BF16)
            h_new = _pool(h, ml, g1, pw, pool_scale[occ])
            if not last:
                hc = _pool(hc, mc, g1, pw, pool_scale[occ])
            h = h_new
        else:
            assert last, "windowed attention with live context outputs is not wired up"
            nq = SWA_Q_HEADS * SWA_HEAD_DIM
            nkv = SWA_KV_HEADS * SWA_HEAD_DIM
            wq = swa_w_qkv[occ][:, :nq]
            wk = _dup_heads(swa_w_qkv[occ][:, nq:nq + nkv], SWA_KV_HEADS, SWA_HEAD_DIM)
            wv = _dup_heads(swa_w_qkv[occ][:, nq + nkv:], SWA_KV_HEADS, SWA_HEAD_DIM)
            w = jnp.concatenate([wq, wk, wv], axis=-1).astype(BF16)
            wo = swa_w_o[occ].astype(BF16)
            widths = (nq, 2 * nkv, 2 * nkv)
            tt = jnp.arange(t)
            ang = _rope_angles([tt // GRID_W, tt % GRID_W], SWA_HEAD_DIM)
            ang = jnp.tile(ang, (1, LANES // SWA_HEAD_DIM))
            lane = jnp.arange(LANES)
            sin = jnp.where(lane % 32 < 16, -jnp.sin(ang), jnp.sin(ang))
            _, kc, vc = _qkv_proj(hc, mc, g1, w, widths, SWA_HEAD_DIM ** -0.5, want_q=False)
            ql, kl, vl = _qkv_proj(h, ml, g1, w, widths, SWA_HEAD_DIM ** -0.5, jnp.cos(ang), sin)
            ol = _swa_attention(swa_sink[occ].astype(F32), ql, kl, vl, kc, vc)
            h = _out_proj(ol, wo, h, ml)
        if not last:
            hc = _ffn(hc, mc, norm_g[i, 2], w_in[1], w_out[1], 2)
        h = _ffn(h, ml, norm_g[i, 2], w_in[1], w_out[1], 2, final_norm_g if last else None)
    return h
```

```python
import functools

import jax
import jax.numpy as jnp
from jax import lax
from jax.experimental import pallas as pl
from jax.experimental.pallas import tpu as pltpu

F32 = jnp.float32
BF16 = jnp.bfloat16

EPS = 1e-6
NEG_INF = -1e30
ROPE_BASE = 10000.0
GRID_W = 64
N_MIXERS = 4
FFN_HIDDEN = 2816
RET_HEADS = 4
RET_QK_DIM = 256
RET_V_DIM = 512
NAT_HEADS = 16
NAT_HEAD_DIM = 64
NAT_KH = 8
NAT_KW = 16
POOL_WINDOWS = (2, 4, 8, 16)
SWA_Q_HEADS = 16
SWA_KV_HEADS = 4
SWA_HEAD_DIM = 64
SWA_WINDOW = 128
SWA_BLOCK = 128

LANES = 128
SUBLANES = 8
VMEM_LIMIT = 56 * 1024 * 1024
TOKEN_TILE = 256
FFN_TILE = 512
RET_CHUNK = 256
XPOSE_CHUNK = 512


def _params(*sem):
    return pltpu.CompilerParams(dimension_semantics=sem, vmem_limit_bytes=VMEM_LIMIT)


def _resident(shape):
    nd = len(shape)
    return pl.BlockSpec(shape, lambda *_: (0,) * nd, pipeline_mode=pl.Buffered(1))


def _silu(x):
    return x * jax.nn.sigmoid(x)


def _ada(x, g, shift, scale):
    var = jnp.mean(x * x, axis=-1, keepdims=True)
    y = x * lax.rsqrt(var + EPS) * g
    return y * (1.0 + scale) + shift


def _mod_rows(mod_ref, j):
    return (mod_ref[0, 3 * j:3 * j + 1, :], mod_ref[0, 3 * j + 1:3 * j + 2, :],
            mod_ref[0, 3 * j + 2:3 * j + 3, :])


def _mod_spec(mod, d):
    if mod.shape[0] == 1:
        return pl.BlockSpec((1, 9, d), lambda b, t: (0, 0, 0))
    return pl.BlockSpec((1, 9, d), lambda b, t: (b, 0, 0))


def _dot(a, b):
    return jnp.dot(a, b, preferred_element_type=F32)


def _dot_nt(a, b):
    return lax.dot_general(a, b, (((1,), (1,)), ((), ())), preferred_element_type=F32)


def _dot_tn(a, b):
    return lax.dot_general(a, b, (((0,), (0,)), ((), ())), preferred_element_type=F32)


def _mod_kernel(c_ref, w_ref, b_ref, o_ref):
    s = _silu(c_ref[...])
    o_ref[0] = jnp.dot(s, w_ref[0], preferred_element_type=F32,
                       precision=lax.Precision.HIGHEST) + b_ref[0]


def _modulation(cc, w_mod, b_mod):
    depth, d, n = w_mod.shape
    tn = 2304
    return pl.pallas_call(
        _mod_kernel,
        grid=(depth, n // tn),
        in_specs=[pl.BlockSpec((SUBLANES, d), lambda l, j: (0, 0)),
                  pl.BlockSpec((1, d, tn), lambda l, j: (l, 0, j)),
                  pl.BlockSpec((1, 1, tn), lambda l, j: (l, 0, j))],
        out_specs=pl.BlockSpec((1, SUBLANES, tn), lambda l, j: (l, 0, j)),
        out_shape=jax.ShapeDtypeStruct((depth, SUBLANES, n), F32),
        compiler_params=_params("arbitrary", "arbitrary"),
        name="modulation",
    )(cc, w_mod, b_mod.reshape(depth, 1, n))


def _ffn_kernel(h_ref, mod_ref, g_ref, win_ref, wout_ref, *rest, j, mixer_out, final):
    o_ref = rest[-1]
    x = h_ref[0]
    if mixer_out:
        y_ref, wo_ref = rest[:2]
        x = x + mod_ref[0, 5:6, :] * _dot(y_ref[0], wo_ref[...])
    shift, scale, gate = _mod_rows(mod_ref, j)
    xb = _ada(x, g_ref[...], shift, scale).astype(BF16)
    f = FFN_HIDDEN
    a = _dot(xb, win_ref[0, 0, :, :f])
    b = _dot(xb, win_ref[0, 0, :, f:])
    hid = (_silu(a) * b).astype(BF16)
    y = _dot(hid, wout_ref[0, 0])
    out = x + (0.5 * gate) * y
    if final:
        var = jnp.mean(out * out, axis=-1, keepdims=True)
        out = out * lax.rsqrt(var + EPS) * rest[-2][...]
    o_ref[0] = out


def _ffn(h, mod, g, w_in, w_out, layer, half, mixer_out=None, final_g=None):
    bsz, t, d = h.shape
    tm = min(FFN_TILE, t)
    row = lambda n: pl.BlockSpec((1, tm, n), lambda b, i: (b, i, 0))
    vec = pl.BlockSpec((1, d), lambda b, i: (0, 0))
    pick = lambda w: pl.BlockSpec((1, 1) + w.shape[2:], lambda b, i: (layer, half, 0, 0),
                                  pipeline_mode=pl.Buffered(1))
    in_specs = [row(d), _mod_spec(mod, d), vec, pick(w_in), pick(w_out)]
    args = [h, mod, g.reshape(1, d), w_in, w_out]
    if mixer_out is not None:
        y, wo = mixer_out
        in_specs += [row(y.shape[-1]), _resident(wo.shape)]
        args += [y, wo]
    if final_g is not None:
        in_specs.append(vec)
        args.append(final_g.reshape(1, d))
    return pl.pallas_call(
        functools.partial(_ffn_kernel, j=2 * half, mixer_out=mixer_out is not None, final=final_g is not None),
        grid=(bsz, t // tm),
        in_specs=in_specs,
        out_specs=row(d),
        out_shape=jax.ShapeDtypeStruct((bsz, t, d), F32),
        compiler_params=_params("parallel", "parallel"),
        name="ffn",
    )(*args)


def _ret_proj_kernel(h_ref, mod_ref, g_ref, w_ref, *rest, rope):
    if rope:
        cos_ref, sin_ref, q_ref, k_ref, v_ref, gate_ref = rest
    else:
        q_ref, k_ref, v_ref, gate_ref = rest
    shift, scale, _ = _mod_rows(mod_ref, 1)
    xb = _ada(h_ref[0], g_ref[...], shift, scale).astype(BF16)
    dk, nh = RET_QK_DIM, RET_HEADS
    half = dk // 2

    def rot(x):
        if not rope:
            return x.astype(BF16)
        c, s = cos_ref[...], sin_ref[...]
        x1, x2 = x[:, :half], x[:, half:]
        return jnp.concatenate([x1 * c - x2 * s, x2 * c + x1 * s], axis=-1).astype(BF16)

    for hd in range(nh):
        q_ref[0, :, hd * dk:(hd + 1) * dk] = rot(_dot(xb, w_ref[:, hd * dk:(hd + 1) * dk]))
        kcol = nh * dk + hd * dk
        k_ref[0, :, hd * dk:(hd + 1) * dk] = rot(_dot(xb, w_ref[:, kcol:kcol + dk]) * dk ** -0.5)
    v0 = 2 * nh * dk
    nv = nh * RET_V_DIM
    v_ref[0] = _dot(xb, w_ref[:, v0:v0 + nv]).astype(BF16)
    gate_ref[0] = _silu(_dot(xb, w_ref[:, v0 + nv:v0 + 2 * nv])).astype(BF16)


def _ret_proj(h, mod, g, w, cos=None, sin=None):
    bsz, t, d = h.shape
    tm = TOKEN_TILE
    rope = cos is not None
    nqk = RET_HEADS * RET_QK_DIM
    nv = RET_HEADS * RET_V_DIM
    row = lambda n: pl.BlockSpec((1, tm, n), lambda b, i: (b, i, 0))
    in_specs = [row(d), _mod_spec(mod, d), pl.BlockSpec((1, d), lambda b, i: (0, 0)), _resident(w.shape)]
    args = [h, mod, g.reshape(1, d), w]
    if rope:
        in_specs += [pl.BlockSpec((tm, RET_QK_DIM // 2), lambda b, i: (i, 0))] * 2
        args += [cos, sin]
    return pl.pallas_call(
        functools.partial(_ret_proj_kernel, rope=rope),
        grid=(bsz, t // tm),
        in_specs=in_specs,
        out_specs=[row(nqk), row(nqk), row(nv), row(nv)],
        out_shape=[jax.ShapeDtypeStruct((bsz, t, nqk), BF16), jax.ShapeDtypeStruct((bsz, t, nqk), BF16),
                   jax.ShapeDtypeStruct((bsz, t, nv), BF16), jax.ShapeDtypeStruct((bsz, t, nv), BF16)],
        compiler_params=_params("parallel", "parallel"),
        name="ret_proj",
    )(*args)


def _log_sigmoid(x):
    return jnp.minimum(x, 0.0) - jnp.log(1.0 + jnp.exp(-jnp.abs(x)))


def _ret_mix_kernel(decf_ref, decb_ref, q_ref, k_ref, v_ref, g_ref, gn_ref, s0f_ref, s0b_ref,
                    y_ref, sf_ref, sb_ref, s_scr, sb_scr, *, chunk):
    c = chunk
    nc = q_ref.shape[1] // c
    lgf = _log_sigmoid(decf_ref[0])[:, :1]
    lgb = _log_sigmoid(decb_ref[0])[:, :1]
    dist = lax.broadcasted_iota(jnp.int32, (c, c), 0) - lax.broadcasted_iota(jnp.int32, (c, c), 1)
    decay = jnp.where(dist >= 0,
                      jnp.exp(jnp.maximum(dist, 0).astype(F32) * lgf),
                      jnp.exp(jnp.maximum(-dist, 0).astype(F32) * lgb))
    pos = lax.broadcasted_iota(jnp.int32, (c, 1), 0).astype(F32)
    qd_f, kd_f, cd_f = jnp.exp((pos + 1.0) * lgf), jnp.exp((c - 1.0 - pos) * lgf), jnp.exp(c * lgf)
    qd_b, kd_b, cd_b = jnp.exp((c - pos) * lgb), jnp.exp(pos * lgb), jnp.exp(c * lgb)

    def rows(n):
        return pl.ds(pl.multiple_of(n * c, c), c)

    s_scr[...] = s0b_ref[0, 0]

    def bwd(i, carry):
        n = nc - 1 - i
        s = s_scr[...]
        sb_scr[n] = s.astype(BF16)
        kd = (k_ref[0, rows(n), :].astype(F32) * kd_b).astype(BF16)
        s_scr[...] = s * cd_b + _dot_tn(kd, v_ref[0, rows(n), :])
        return carry

    lax.fori_loop(0, nc, bwd, 0, unroll=min(4, nc))
    sb_ref[0, 0] = s_scr[...]
    s_scr[...] = s0f_ref[0, 0]

    def fwd(n, carry):
        q, k, v = q_ref[0, rows(n), :], k_ref[0, rows(n), :], v_ref[0, rows(n), :]
        qf, kf = q.astype(F32), k.astype(F32)
        s = s_scr[...]
        att = (_dot(q, k.T) * decay).astype(BF16)
        o = (_dot(att, v) + _dot((qf * qd_f).astype(BF16), s.astype(BF16))
             + _dot((qf * qd_b).astype(BF16), sb_scr[n]))
        mu = jnp.mean(o, axis=-1, keepdims=True)
        ctr = o - mu
        var = jnp.mean(ctr * ctr, axis=-1, keepdims=True)
        on = ctr * lax.rsqrt(var + EPS) * gn_ref[...]
        y_ref[0, rows(n), :] = (g_ref[0, rows(n), :].astype(F32) * on).astype(BF16)
        s_scr[...] = s * cd_f + _dot_tn((kf * kd_f).astype(BF16), v)
        return carry

    lax.fori_loop(0, nc, fwd, 0, unroll=min(4, nc))
    sf_ref[0, 0] = s_scr[...]


def _ret_mix(dec_f, dec_b, q, k, v, gate, gn_g, s0_f, s0_b):
    bsz, t, _ = q.shape
    nh, dk, dv = RET_HEADS, RET_QK_DIM, RET_V_DIM
    chunk = min(RET_CHUNK, t)
    seq = lambda n: pl.BlockSpec((1, t, n), lambda b, h: (b, 0, h))
    dec = pl.BlockSpec((1, 1, LANES), lambda b, h: (h, 0, 0))
    state = pl.BlockSpec((1, 1, dk, dv), lambda b, h: (b, h, 0, 0))
    return pl.pallas_call(
        functools.partial(_ret_mix_kernel, chunk=chunk),
        grid=(bsz, nh),
        in_specs=[dec, dec, seq(dk), seq(dk), seq(dv), seq(dv),
                  pl.BlockSpec((1, dv), lambda b, h: (0, h)), state, state],
        out_specs=[seq(dv), state, state],
        out_shape=[jax.ShapeDtypeStruct((bsz, t, nh * dv), BF16),
                   jax.ShapeDtypeStruct((bsz, nh, dk, dv), F32),
                   jax.ShapeDtypeStruct((bsz, nh, dk, dv), F32)],
        scratch_shapes=[pltpu.VMEM((dk, dv), F32), pltpu.VMEM((t // chunk, dk, dv), BF16)],
        compiler_params=_params("parallel", "parallel"),
        name="ret_mix",
    )(dec_f, dec_b, q, k, v, gate, gn_g.reshape(1, nh * dv), s0_f, s0_b)


def _qkv_proj_kernel(h_ref, mod_ref, g_ref, w_ref, *rest, widths, q_scale, rope):
    if rope:
        cos_ref, sin_ref = rest[:2]
        outs = rest[2:]
    else:
        outs = rest
    shift, scale, _ = _mod_rows(mod_ref, 1)
    xb = _ada(h_ref[0], g_ref[...], shift, scale).astype(BF16)
    col = 0
    for idx, (o_ref, n) in enumerate(zip(outs, widths)):
        if o_ref is not None:
            y = _dot(xb, w_ref[:, col:col + n])
            for c0 in range(0, n, LANES):
                yc = y[:, c0:c0 + LANES]
                if rope and idx < 2:
                    lane = lax.broadcasted_iota(jnp.int32, yc.shape, 1)
                    partner = jnp.where((lane & 16) == 0, pltpu.roll(yc, LANES - 16, axis=1),
                                        pltpu.roll(yc, 16, axis=1))
                    yc = yc * cos_ref[...] + partner * sin_ref[...]
                if idx == 0:
                    yc = yc * q_scale
                o_ref[0, :, c0:c0 + LANES] = yc.astype(BF16)
        col += n


def _qkv_proj(h, mod, g, w, widths, q_scale, cos=None, sin=None, want_q=True):
    bsz, t, d = h.shape
    tm = TOKEN_TILE
    rope = cos is not None
    row = lambda n: pl.BlockSpec((1, tm, n), lambda b, i: (b, i, 0))
    in_specs = [row(d), _mod_spec(mod, d), pl.BlockSpec((1, d), lambda b, i: (0, 0)), _resident(w.shape)]
    args = [h, mod, g.reshape(1, d), w]
    if rope:
        in_specs += [pl.BlockSpec((tm, LANES), lambda b, i: (i, 0))] * 2
        args += [cos, sin]
    keep = [want_q, True, True]
    out_widths = [n for n, kp in zip(widths, keep) if kp]

    def body(*refs):
        n_in = len(args)
        outs = list(refs[n_in:])
        full = [outs.pop(0) if kp else None for kp in keep]
        _qkv_proj_kernel(*refs[:n_in], *full, widths=widths, q_scale=q_scale, rope=rope)

    res = pl.pallas_call(
        body,
        grid=(bsz, t // tm),
        in_specs=in_specs,
        out_specs=[row(n) for n in out_widths],
        out_shape=[jax.ShapeDtypeStruct((bsz, t, n), BF16) for n in out_widths],
        compiler_params=_params("parallel", "parallel"),
        name="qkv_proj",
    )(*args)
    return res if want_q else [None] + list(res)


def _split_heads(x):
    lane = lax.broadcasted_iota(jnp.int32, x.shape, 1)
    zero = jnp.zeros_like(x)
    return jnp.concatenate([jnp.where(lane < 64, x, zero), jnp.where(lane >= 64, x, zero)], axis=0)


def _merge_heads(o):
    r = o.shape[0] // 2
    lane = lax.broadcasted_iota(jnp.int32, (r, o.shape[1]), 1)
    return jnp.where(lane < 64, o[:r], o[r:])


def _stage_transposed(dst_ref, src_ref, row0, width):
    for c0 in range(0, width, XPOSE_CHUNK):
        n = min(XPOSE_CHUNK, width - c0)
        dst_ref[:, c0:c0 + n] = src_ref[0, row0 + c0:row0 + c0 + n, :].T


def _nat_kernel(q_ref, k_ref, v_ref, kc_ref, vc_ref, bias_ref, o_ref, kt_scr, snb_a, scx_a, snb_b, scx_b, *,
                rows):
    w = GRID_W
    kh = min(NAT_KH, rows)
    nk = kh * w
    t = rows * w
    _stage_transposed(kt_scr.at[0], k_ref, 0, t)
    _stage_transposed(kt_scr.at[1], k_ref, w, t - 2 * w)
    kct = kc_ref[0].T
    vc = vc_ref[0]

    def window(r):
        r0 = jnp.clip(r - kh // 2, 0, rows - kh)
        return r0, pl.multiple_of(r * w, w)

    def scores(r, snb_ref, scx_ref):
        r0, qs = window(r)
        ty = r - r0
        par = r0 & 1
        kts = pl.multiple_of((r0 - par) * w, 2 * w)
        q2 = _split_heads(q_ref[0, pl.ds(qs, w), :])
        bias = jnp.concatenate(
            [jnp.concatenate([bias_ref[hd, 2 * m - ty + NAT_KH - 1] for m in range(kh // 2)], axis=1)
             for hd in range(2)], axis=0)
        snb_ref[...] = _dot(q2, kt_scr[par, :, pl.ds(kts, nk)]) + bias
        scx_ref[...] = _dot(q2, kct)

    def finish(r, snb_ref, scx_ref):
        r0, qs = window(r)
        ks = pl.multiple_of(r0 * w, w)
        s_nb, s_cx = snb_ref[...], scx_ref[...]
        m = jnp.maximum(jnp.max(s_nb, axis=-1, keepdims=True), jnp.max(s_cx, axis=-1, keepdims=True))
        p_nb = jnp.exp(s_nb - m)
        p_cx = jnp.exp(s_cx - m)
        l = jnp.sum(p_nb, axis=-1, keepdims=True) + jnp.sum(p_cx, axis=-1, keepdims=True)
        o = (_dot(p_nb.astype(BF16), v_ref[0, pl.ds(ks, nk), :]) + _dot(p_cx.astype(BF16), vc)) / l
        o_ref[0, pl.ds(qs, w), :] = _merge_heads(o).astype(BF16)

    scores(0, snb_a, scx_a)

    def body(i, carry):
        r = 2 * i
        scores(r + 1, snb_b, scx_b)
        finish(r, snb_a, scx_a)
        scores(jnp.minimum(r + 2, rows - 1), snb_a, scx_a)
        finish(r + 1, snb_b, scx_b)
        return carry

    lax.fori_loop(0, rows // 2, body, 0, unroll=4)


def _nat_attention(q, k, v, kc, vc, bias):
    bsz, t, d = q.shape
    l = kc.shape[1]
    rows = t // GRID_W
    hp = d // LANES
    lat = pl.BlockSpec((1, t, LANES), lambda p, b: (b, 0, p))
    ctx = pl.BlockSpec((1, l, LANES), lambda p, b: (b, 0, p))
    return pl.pallas_call(
        functools.partial(_nat_kernel, rows=rows),
        grid=(hp, bsz),
        in_specs=[lat, lat, lat, ctx, ctx,
                  pl.BlockSpec((2,) + bias.shape[1:], lambda p, b: (p, 0, 0, 0))],
        out_specs=lat,
        out_shape=jax.ShapeDtypeStruct((bsz, t, d), BF16),
        scratch_shapes=[pltpu.VMEM((2, LANES, t), BF16)]
        + [pltpu.VMEM((2 * GRID_W, n), F32) for n in (min(NAT_KH, rows) * GRID_W, l)] * 2,
        compiler_params=_params("parallel", "parallel"),
        name="nat_attention",
    )(q, k, v, kc, vc, bias)


def _nat_bias(rpb):
    w = GRID_W
    nh, nr, ncol = rpb.shape
    left = w - NAT_KW
    v = jnp.pad(rpb.astype(F32), ((0, 0), (0, 0), (left, 2 * w - ncol - left)))
    toep = jnp.tile(v, (1, 1, w))[..., :w * (2 * w - 1)].reshape(nh, nr, w, 2 * w - 1)[..., w - 1:]
    c = jnp.arange(w)[:, None]
    cc = jnp.arange(w)[None, :]
    c0 = jnp.clip(c - NAT_KW // 2, 0, w - NAT_KW)
    blocks = jnp.where((cc >= c0) & (cc < c0 + NAT_KW), toep, NEG_INF)
    return jnp.concatenate([blocks[:, :-1], blocks[:, 1:]], axis=-1)


def _ctx_attn_kernel(q_ref, k_ref, v_ref, o_ref):
    q2 = _split_heads(q_ref[0])
    s = _dot(q2, k_ref[0].T)
    p = jnp.exp(s - jnp.max(s, axis=-1, keepdims=True))
    l = jnp.sum(p, axis=-1, keepdims=True)
    o_ref[0] = _merge_heads(_dot(p.astype(BF16), v_ref[0]) / l).astype(BF16)


def _ctx_attention(q, k, v):
    bsz, l, d = q.shape
    blk = pl.BlockSpec((1, l, LANES), lambda b, p: (b, 0, p))
    return pl.pallas_call(
        _ctx_attn_kernel,
        grid=(bsz, d // LANES),
        in_specs=[blk, blk, blk],
        out_specs=blk,
        out_shape=jax.ShapeDtypeStruct((bsz, l, d), BF16),
        compiler_params=_params("parallel", "parallel"),
        name="ctx_attention",
    )(q, k, v)


POOL_HALO = SUBLANES
POOL_K = 384


def _pool_kernel(h_ref, prev_ref, next_ref, mod_ref, g_ref, pw_ref, ps_ref, o_ref, *, tm, t_total):
    i = pl.program_id(1)
    x = h_ref[0]
    d = x.shape[-1]
    shift, scale, gate = _mod_rows(mod_ref, 1)
    pad = jnp.zeros((POOL_K - tm - 2 * POOL_HALO, d), F32)
    xe = _ada(jnp.concatenate([prev_ref[0], x, next_ref[0]], axis=0), g_ref[...], shift, scale)
    xn = xe[POOL_HALO:POOL_HALO + tm]
    xe = jnp.concatenate([xe, pad], axis=0)
    x_hi = xe.astype(BF16)
    r1 = xe - x_hi.astype(F32)
    x_mid = r1.astype(BF16)
    x_lo = (r1 - x_mid.astype(F32)).astype(BF16)
    tpos = i * tm + lax.broadcasted_iota(jnp.int32, (tm, POOL_K), 0)
    upos = i * tm - POOL_HALO + lax.broadcasted_iota(jnp.int32, (tm, POOL_K), 1)
    trow = i * tm + lax.broadcasted_iota(jnp.int32, (tm, 1), 0)
    gd = d // len(POOL_WINDOWS)
    parts = []
    for gi, win in enumerate(POOL_WINDOWS):
        half = win // 2
        lo = jnp.maximum(tpos - half, 0)
        hi = jnp.minimum(tpos + half, t_total)
        band = jnp.where((upos >= lo) & (upos < hi), 1.0, 0.0).astype(BF16)
        cnt = (jnp.minimum(trow + half, t_total) - jnp.maximum(trow - half, 0)).astype(F32)
        cols = slice(gi * gd, (gi + 1) * gd)
        tot = _dot(band, x_hi[:, cols]) + _dot(band, x_mid[:, cols]) + _dot(band, x_lo[:, cols])
        pooled = (tot / cnt - xn[:, cols]).astype(BF16)
        parts.append(_dot(pooled, pw_ref[gi]))
    y = jnp.concatenate(parts, axis=-1) * ps_ref[...]
    o_ref[0] = x + gate * y


def _pool(h, mod, g, pw, ps):
    bsz, t, d = h.shape
    tm = TOKEN_TILE
    per = tm // POOL_HALO
    last = t // POOL_HALO - 1
    return pl.pallas_call(
        functools.partial(_pool_kernel, tm=tm, t_total=t),
        grid=(bsz, t // tm),
        in_specs=[pl.BlockSpec((1, tm, d), lambda b, i: (b, i, 0)),
                  pl.BlockSpec((1, POOL_HALO, d), lambda b, i: (b, jnp.maximum(i * per - 1, 0), 0)),
                  pl.BlockSpec((1, POOL_HALO, d), lambda b, i: (b, jnp.minimum((i + 1) * per, last), 0)),
                  _mod_spec(mod, d),
                  pl.BlockSpec((1, d), lambda b, i: (0, 0)),
                  _resident(pw.shape),
                  pl.BlockSpec((1, d), lambda b, i: (0, 0))],
        out_specs=pl.BlockSpec((1, tm, d), lambda b, i: (b, i, 0)),
        out_shape=jax.ShapeDtypeStruct((bsz, t, d), F32),
        compiler_params=_params("parallel", "parallel"),
        name="pool",
    )(h, h, h, mod, g.reshape(1, d), pw, ps.reshape(1, d))


def _swa_kernel(sink_ref, q_ref, k_ref, v_ref, kc_ref, vc_ref, o_ref, kt_scr, sloc_a, scx_a, sloc_b, scx_b, *,
                t_total):
    kv = pl.program_id(1)
    blk = SWA_BLOCK
    nb = t_total // blk
    grp = SWA_Q_HEADS // SWA_KV_HEADS
    nrow = grp * blk
    _stage_transposed(kt_scr, k_ref, 0, t_total)
    kct = kc_ref[0].T
    vc = vc_ref[0]
    row = lax.broadcasted_iota(jnp.int32, (nrow, 1), 0)
    sink = jnp.zeros((nrow, 1), F32)
    for gi in range(grp):
        sink = jnp.where((row >= gi * blk) & (row < (gi + 1) * blk), sink_ref[kv * grp + gi], sink)
    qi = lax.broadcasted_iota(jnp.int32, (nrow, blk), 0) & (blk - 1)
    kj = lax.broadcasted_iota(jnp.int32, (nrow, blk), 1)
    open_blk = jnp.zeros((nrow, blk), F32)
    prev_blk = jnp.where(kj >= qi, 0.0, NEG_INF)
    next_blk = jnp.where(kj <= qi, 0.0, NEG_INF)

    shut_blk = jnp.full((nrow, blk), NEG_INF, F32)
    span = 3 * blk
    mask_first = jnp.concatenate([open_blk, next_blk, shut_blk], axis=1)
    mask_mid = jnp.concatenate([prev_blk, open_blk, next_blk], axis=1)
    mask_last = jnp.concatenate([shut_blk, prev_blk, open_blk], axis=1)

    def offsets(n):
        qs = pl.multiple_of(n * blk, blk)
        return qs, pl.multiple_of(jnp.clip(qs - blk, 0, t_total - span), blk)

    def scores(n, mask, sloc_ref, scx_ref):
        qs, ks = offsets(n)
        qb = q_ref[0, pl.ds(qs, blk), :]
        q4 = jnp.concatenate([_split_heads(qb[:, :LANES]), _split_heads(qb[:, LANES:])], axis=0)
        sloc_ref[...] = _dot(q4, kt_scr[:, pl.ds(ks, span)]) + mask
        scx_ref[...] = _dot(q4, kct)

    def finish(n, sloc_ref, scx_ref):
        qs, ks = offsets(n)
        s_loc, s_cx = sloc_ref[...], scx_ref[...]
        m = jnp.maximum(jnp.maximum(jnp.max(s_loc, axis=-1, keepdims=True),
                                    jnp.max(s_cx, axis=-1, keepdims=True)), sink)
        p_loc = jnp.exp(s_loc - m)
        p_cx = jnp.exp(s_cx - m)
        l = jnp.sum(p_loc, axis=-1, keepdims=True) + jnp.sum(p_cx, axis=-1, keepdims=True) + jnp.exp(sink - m)
        o = (_dot(p_loc.astype(BF16), v_ref[0, pl.ds(ks, span), :]) + _dot(p_cx.astype(BF16), vc)) / l
        out = jnp.concatenate([_merge_heads(o[:2 * blk]), _merge_heads(o[2 * blk:])], axis=-1)
        o_ref[0, pl.ds(qs, blk), :] = out.astype(BF16)

    slot_a, slot_b = (sloc_a, scx_a), (sloc_b, scx_b)
    scores(0, mask_first, *slot_a)
    scores(1, mask_mid, *slot_b)
    finish(0, *slot_a)

    def body(i, carry):
        n = 1 + 2 * i
        scores(n + 1, mask_mid, *slot_a)
        finish(n, *slot_b)
        scores(n + 2, mask_mid, *slot_b)
        finish(n + 1, *slot_a)
        return carry

    lax.fori_loop(0, (nb - 4) // 2, body, 0, unroll=2)
    scores(nb - 2, mask_mid, *slot_a)
    finish(nb - 3, *slot_b)
    scores(nb - 1, mask_last, *slot_b)
    finish(nb - 2, *slot_a)
    finish(nb - 1, *slot_b)


def _swa_attention(sink, q, k, v, kc, vc):
    bsz, t, d = q.shape
    l = kc.shape[1]
    nb = t // SWA_BLOCK
    assert nb >= 4 and nb % 2 == 0 and SWA_WINDOW == SWA_BLOCK
    qw = d // SWA_KV_HEADS
    grid_spec = pltpu.PrefetchScalarGridSpec(
        num_scalar_prefetch=1,
        grid=(bsz, SWA_KV_HEADS),
        in_specs=[pl.BlockSpec((1, t, qw), lambda b, h, s: (b, 0, h)),
                  pl.BlockSpec((1, t, LANES), lambda b, h, s: (b, 0, h)),
                  pl.BlockSpec((1, t, LANES), lambda b, h, s: (b, 0, h)),
                  pl.BlockSpec((1, l, LANES), lambda b, h, s: (b, 0, h)),
                  pl.BlockSpec((1, l, LANES), lambda b, h, s: (b, 0, h))],
        out_specs=pl.BlockSpec((1, t, qw), lambda b, h, s: (b, 0, h)),
        scratch_shapes=[pltpu.VMEM((LANES, t), BF16)]
        + [pltpu.VMEM((SWA_Q_HEADS // SWA_KV_HEADS * SWA_BLOCK, n), F32) for n in (3 * SWA_BLOCK, l)] * 2,
    )
    return pl.pallas_call(
        functools.partial(_swa_kernel, t_total=t),
        grid_spec=grid_spec,
        out_shape=jax.ShapeDtypeStruct((bsz, t, d), BF16),
        compiler_params=_params("parallel", "parallel"),
        name="swa_attention",
    )(sink, q, k, v, kc, vc)


def _rope_angles(positions, dim):
    seg = dim // len(positions)
    inv = ROPE_BASE ** (-jnp.arange(0, seg, 2, dtype=F32) / seg)
    return jnp.concatenate([jnp.tile(p.astype(F32)[:, None] * inv, (1, 2)) for p in positions], axis=-1)


def _dup_heads(w, heads, dh):
    d = w.shape[0]
    return jnp.broadcast_to(w.reshape(d, heads, 1, dh), (d, heads, 2, dh)).reshape(d, heads * 2 * dh)


def kernel(x, c, ctx, c_ctx, w_mod, b_mod, norm_g, ffn_w_in, ffn_w_out, ret_w_in, ret_w_out, ret_gn_g, ret_decay_f, ret_decay_b, nat_w_qkv, nat_w_o, nat_rpb, pool_w, pool_scale, swa_w_qkv, swa_w_o, swa_sink, final_norm_g):
    bsz, t, d = x.shape
    depth = w_mod.shape[0]
    cc = jnp.concatenate([c, c_ctx[None], jnp.zeros((SUBLANES - bsz - 1, d), F32)], axis=0)
    mods = _modulation(cc, w_mod, b_mod)
    w_in = ffn_w_in.astype(BF16)
    w_out = ffn_w_out.astype(BF16)
    h, hc = x, ctx
    for i in range(depth):
        kind, occ = i % N_MIXERS, i // N_MIXERS
        last = i == depth - 1
        ctx_live = (not last) or kind != 2
        ml = mods[i, :bsz].reshape(bsz, 9, d)
        mc = mods[i, bsz:bsz + 1].reshape(1, 9, d)
        h = _ffn(h, ml, norm_g[i, 0], w_in, w_out, i, 0)
        if ctx_live:
            hc = _ffn(hc, mc, norm_g[i, 0], w_in, w_out, i, 0)
        g1 = norm_g[i, 1]
        yl = yc = None
        if kind == 0:
            assert not last, "retention as the last layer is not wired up"
            w = ret_w_in[occ].astype(BF16)
            wo = ret_w_out[occ].astype(BF16)
            ang = _rope_angles([jnp.arange(t)], RET_QK_DIM)[:, :RET_QK_DIM // 2]
            dec_f = jnp.broadcast_to(ret_decay_f[occ].astype(F32)[:, None, None], (RET_HEADS, 1, LANES))
            dec_b = jnp.broadcast_to(ret_decay_b[occ].astype(F32)[:, None, None], (RET_HEADS, 1, LANES))
            qc, kc, vc, gc = _ret_proj(hc, mc, g1, w)
            zeros = jnp.zeros((bsz, RET_HEADS, RET_QK_DIM, RET_V_DIM), F32)
            oc, s_f, s_b = _ret_mix(dec_f, dec_b, qc, kc, vc, gc, ret_gn_g[occ], zeros, zeros)
            ql, kl, vl, gl = _ret_proj(h, ml, g1, w, jnp.cos(ang), jnp.sin(ang))
            ol, _, _ = _ret_mix(dec_f, dec_b, ql, kl, vl, gl, ret_gn_g[occ], s_f, s_b)
            yl, yc = (ol, wo), (oc, wo)
        elif kind == 1:
            w = nat_w_qkv[occ].astype(BF16)
            wo = nat_w_o[occ].astype(BF16)
            widths = (d, d, d)
            qc, kc, vc = _qkv_proj(hc, mc, g1, w, widths, NAT_HEAD_DIM ** -0.5, want_q=not last)
            ql, kl, vl = _qkv_proj(h, ml, g1, w, widths, NAT_HEAD_DIM ** -0.5)
            yl = (_nat_attention(ql, kl, vl, kc, vc, _nat_bias(nat_rpb[occ])), wo)
            if not last:
                yc = (_ctx_attention(qc, kc, vc), wo)
        elif kind == 2:
            pw = pool_w[occ].astype(BF16)
            h_new = _pool(h, ml, g1, pw, pool_scale[occ])
            if not last:
                hc = _pool(hc, mc, g1, pw, pool_scale[occ])
            h = h_new
        else:
            assert last, "windowed attention with live context outputs is not wired up"
            nq = SWA_Q_HEADS * SWA_HEAD_DIM
            nkv = SWA_KV_HEADS * SWA_HEAD_DIM
            wq = swa_w_qkv[occ][:, :nq]
            wk = _dup_heads(swa_w_qkv[occ][:, nq:nq + nkv], SWA_KV_HEADS, SWA_HEAD_DIM)
            wv = _dup_heads(swa_w_qkv[occ][:, nq + nkv:], SWA_KV_HEADS, SWA_HEAD_DIM)
            w = jnp.concatenate([wq, wk, wv], axis=-1).astype(BF16)
            wo = swa_w_o[occ].astype(BF16)
            widths = (nq, 2 * nkv, 2 * nkv)
            tt = jnp.arange(t)
            ang = _rope_angles([tt // GRID_W, tt % GRID_W], SWA_HEAD_DIM)
            ang = jnp.tile(ang, (1, LANES // SWA_HEAD_DIM))
            lane = jnp.arange(LANES)
            sin = jnp.where(lane % 32 < 16, -jnp.sin(ang), jnp.sin(ang))
            _, kc, vc = _qkv_proj(hc, mc, g1, w, widths, SWA_HEAD_DIM ** -0.5, want_q=False)
            ql, kl, vl = _qkv_proj(h, ml, g1, w, widths, SWA_HEAD_DIM ** -0.5, jnp.cos(ang), sin)
            yl = (_swa_attention(swa_sink[occ].astype(F32), ql, kl, vl, kc, vc), wo)
        if not last:
            hc = _ffn(hc, mc, norm_g[i, 2], w_in, w_out, i, 1, yc)
        h = _ffn(h, ml, norm_g[i, 2], w_in, w_out, i, 1, yl, final_norm_g if last else None)
    return h
```

```python
import functools

import jax
import jax.numpy as jnp
from jax import lax
from jax.experimental import pallas as pl
from jax.experimental.pallas import tpu as pltpu

F32 = jnp.float32
BF16 = jnp.bfloat16

EPS = 1e-6
NEG_INF = -1e30
ROPE_BASE = 10000.0
GRID_W = 64
N_MIXERS = 4
FFN_HIDDEN = 2816
RET_HEADS = 4
RET_QK_DIM = 256
RET_V_DIM = 512
NAT_HEADS = 16
NAT_HEAD_DIM = 64
NAT_KH = 8
NAT_KW = 16
POOL_WINDOWS = (2, 4, 8, 16)
SWA_Q_HEADS = 16
SWA_KV_HEADS = 4
SWA_HEAD_DIM = 64
SWA_WINDOW = 128
SWA_BLOCK = 128

LANES = 128
SUBLANES = 8
VMEM_LIMIT = 56 * 1024 * 1024
TOKEN_TILE = 256
PROJ_TILE = 512
FFN_TILE = 512
FFN_SUBTILE = 256
RET_CHUNK = 256
XPOSE_CHUNK = 512


def _params(*sem):
    return pltpu.CompilerParams(dimension_semantics=sem, vmem_limit_bytes=VMEM_LIMIT)


def _resident(shape):
    nd = len(shape)
    return pl.BlockSpec(shape, lambda *_: (0,) * nd, pipeline_mode=pl.Buffered(1))


def _silu(x):
    return x * jax.nn.sigmoid(x)


def _ada(x, g, shift, scale):
    var = jnp.mean(x * x, axis=-1, keepdims=True)
    y = x * lax.rsqrt(var + EPS) * g
    return y * (1.0 + scale) + shift


def _mod_rows(mod_ref, j):
    return (mod_ref[0, 3 * j:3 * j + 1, :], mod_ref[0, 3 * j + 1:3 * j + 2, :],
            mod_ref[0, 3 * j + 2:3 * j + 3, :])


def _mod_spec(mod, d):
    if mod.shape[0] == 1:
        return pl.BlockSpec((1, 9, d), lambda b, t: (0, 0, 0))
    return pl.BlockSpec((1, 9, d), lambda b, t: (b, 0, 0))


def _dot(a, b):
    return jnp.dot(a, b, preferred_element_type=F32)


def _dot_nt(a, b):
    return lax.dot_general(a, b, (((1,), (1,)), ((), ())), preferred_element_type=F32)


def _dot_tn(a, b):
    return lax.dot_general(a, b, (((0,), (0,)), ((), ())), preferred_element_type=F32)


def _mod_kernel(c_ref, w_ref, b_ref, o_ref):
    s = _silu(c_ref[...]).astype(BF16)
    o_ref[0] = _dot(s, w_ref[0].astype(BF16)) + b_ref[0]


def _modulation(cc, w_mod, b_mod):
    depth, d, n = w_mod.shape
    tn = 2304
    return pl.pallas_call(
        _mod_kernel,
        grid=(depth, n // tn),
        in_specs=[pl.BlockSpec((SUBLANES, d), lambda l, j: (0, 0)),
                  pl.BlockSpec((1, d, tn), lambda l, j: (l, 0, j)),
                  pl.BlockSpec((1, 1, tn), lambda l, j: (l, 0, j))],
        out_specs=pl.BlockSpec((1, SUBLANES, tn), lambda l, j: (l, 0, j)),
        out_shape=jax.ShapeDtypeStruct((depth, SUBLANES, n), F32),
        compiler_params=_params("arbitrary", "arbitrary"),
        name="modulation",
    )(cc, w_mod, b_mod.reshape(depth, 1, n))


def _ffn_kernel(h_ref, mod_ref, g_ref, win_ref, wout_ref, *rest, j, mixer_out, final):
    o_ref = rest[-1]
    shift, scale, gate = _mod_rows(mod_ref, j)
    f = FFN_HIDDEN
    tm = h_ref.shape[1]
    sub = min(tm, FFN_SUBTILE)
    for r0 in range(0, tm, sub):
        x = h_ref[0, r0:r0 + sub, :]
        if mixer_out:
            y_ref, wo_ref = rest[:2]
            x = x + mod_ref[0, 5:6, :] * _dot(y_ref[0, r0:r0 + sub, :], wo_ref[...])
        xb = _ada(x, g_ref[...], shift, scale).astype(BF16)
        a = _dot(xb, win_ref[0, 0, :, :f])
        b = _dot(xb, win_ref[0, 0, :, f:])
        hid = (_silu(a) * b).astype(BF16)
        y = _dot(hid, wout_ref[0, 0])
        out = x + (0.5 * gate) * y
        if final:
            var = jnp.mean(out * out, axis=-1, keepdims=True)
            out = out * lax.rsqrt(var + EPS) * rest[-2][...]
        o_ref[0, r0:r0 + sub, :] = out


def _ffn(h, mod, g, w_in, w_out, layer, half, mixer_out=None, final_g=None):
    bsz, t, d = h.shape
    tm = min(FFN_TILE, t)
    row = lambda n: pl.BlockSpec((1, tm, n), lambda b, i: (b, i, 0))
    vec = pl.BlockSpec((1, d), lambda b, i: (0, 0))
    pick = lambda w: pl.BlockSpec((1, 1) + w.shape[2:], lambda b, i: (layer, half, 0, 0),
                                  pipeline_mode=pl.Buffered(1))
    in_specs = [row(d), _mod_spec(mod, d), vec, pick(w_in), pick(w_out)]
    args = [h, mod, g.reshape(1, d), w_in, w_out]
    if mixer_out is not None:
        y, wo = mixer_out
        in_specs += [row(y.shape[-1]), _resident(wo.shape)]
        args += [y, wo]
    if final_g is not None:
        in_specs.append(vec)
        args.append(final_g.reshape(1, d))
    return pl.pallas_call(
        functools.partial(_ffn_kernel, j=2 * half, mixer_out=mixer_out is not None, final=final_g is not None),
        grid=(bsz, t // tm),
        in_specs=in_specs,
        out_specs=row(d),
        out_shape=jax.ShapeDtypeStruct((bsz, t, d), F32),
        compiler_params=_params("parallel", "parallel"),
        name="ffn",
    )(*args)


def _ret_proj_kernel(h_ref, mod_ref, g_ref, w_ref, *rest, rope):
    if rope:
        cos_ref, sin_ref, q_ref, k_ref, v_ref, gate_ref = rest
    else:
        q_ref, k_ref, v_ref, gate_ref = rest
    shift, scale, _ = _mod_rows(mod_ref, 1)
    xb = _ada(h_ref[0], g_ref[...], shift, scale).astype(BF16)
    dk, nh = RET_QK_DIM, RET_HEADS
    half = dk // 2

    def rot(x):
        if not rope:
            return x.astype(BF16)
        c, s = cos_ref[...], sin_ref[...]
        x1, x2 = x[:, :half], x[:, half:]
        return jnp.concatenate([x1 * c - x2 * s, x2 * c + x1 * s], axis=-1).astype(BF16)

    for hd in range(nh):
        q_ref[0, :, hd * dk:(hd + 1) * dk] = rot(_dot(xb, w_ref[:, hd * dk:(hd + 1) * dk]))
        kcol = nh * dk + hd * dk
        k_ref[0, :, hd * dk:(hd + 1) * dk] = rot(_dot(xb, w_ref[:, kcol:kcol + dk]) * dk ** -0.5)
    v0 = 2 * nh * dk
    nv = nh * RET_V_DIM
    v_ref[0] = _dot(xb, w_ref[:, v0:v0 + nv]).astype(BF16)
    gate_ref[0] = _silu(_dot(xb, w_ref[:, v0 + nv:v0 + 2 * nv])).astype(BF16)


def _ret_proj(h, mod, g, w, cos=None, sin=None):
    bsz, t, d = h.shape
    tm = min(PROJ_TILE, t)
    rope = cos is not None
    nqk = RET_HEADS * RET_QK_DIM
    nv = RET_HEADS * RET_V_DIM
    row = lambda n: pl.BlockSpec((1, tm, n), lambda b, i: (b, i, 0))
    in_specs = [row(d), _mod_spec(mod, d), pl.BlockSpec((1, d), lambda b, i: (0, 0)), _resident(w.shape)]
    args = [h, mod, g.reshape(1, d), w]
    if rope:
        in_specs += [pl.BlockSpec((tm, RET_QK_DIM // 2), lambda b, i: (i, 0))] * 2
        args += [cos, sin]
    return pl.pallas_call(
        functools.partial(_ret_proj_kernel, rope=rope),
        grid=(bsz, t // tm),
        in_specs=in_specs,
        out_specs=[row(nqk), row(nqk), row(nv), row(nv)],
        out_shape=[jax.ShapeDtypeStruct((bsz, t, nqk), BF16), jax.ShapeDtypeStruct((bsz, t, nqk), BF16),
                   jax.ShapeDtypeStruct((bsz, t, nv), BF16), jax.ShapeDtypeStruct((bsz, t, nv), BF16)],
        compiler_params=_params("parallel", "parallel"),
        name="ret_proj",
    )(*args)


def _log_sigmoid(x):
    return jnp.minimum(x, 0.0) - jnp.log(1.0 + jnp.exp(-jnp.abs(x)))


def _ret_mix_kernel(decf_ref, decb_ref, q_ref, k_ref, v_ref, g_ref, gn_ref, s0f_ref, s0b_ref,
                    y_ref, sf_ref, sb_ref, s_scr, sb_scr, *, chunk):
    c = chunk
    nc = q_ref.shape[1] // c
    lgf = _log_sigmoid(decf_ref[0])[:, :1]
    lgb = _log_sigmoid(decb_ref[0])[:, :1]
    dist = lax.broadcasted_iota(jnp.int32, (c, c), 0) - lax.broadcasted_iota(jnp.int32, (c, c), 1)
    decay = jnp.where(dist >= 0,
                      jnp.exp(jnp.maximum(dist, 0).astype(F32) * lgf),
                      jnp.exp(jnp.maximum(-dist, 0).astype(F32) * lgb))
    pos = lax.broadcasted_iota(jnp.int32, (c, 1), 0).astype(F32)
    qd_f, kd_f, cd_f = jnp.exp((pos + 1.0) * lgf), jnp.exp((c - 1.0 - pos) * lgf), jnp.exp(c * lgf)
    qd_b, kd_b, cd_b = jnp.exp((c - pos) * lgb), jnp.exp(pos * lgb), jnp.exp(c * lgb)

    def rows(n):
        return pl.ds(pl.multiple_of(n * c, c), c)

    s_scr[...] = s0b_ref[0, 0]

    def bwd(i, carry):
        n = nc - 1 - i
        s = s_scr[...]
        sb_scr[n] = s.astype(BF16)
        kd = (k_ref[0, rows(n), :].astype(F32) * kd_b).astype(BF16)
        s_scr[...] = s * cd_b + _dot_tn(kd, v_ref[0, rows(n), :])
        return carry

    lax.fori_loop(0, nc, bwd, 0, unroll=min(4, nc))
    sb_ref[0, 0] = s_scr[...]
    s_scr[...] = s0f_ref[0, 0]

    def fwd(n, carry):
        q, k, v = q_ref[0, rows(n), :], k_ref[0, rows(n), :], v_ref[0, rows(n), :]
        qf, kf = q.astype(F32), k.astype(F32)
        s = s_scr[...]
        att = (_dot(q, k.T) * decay).astype(BF16)
        o = (_dot(att, v) + _dot((qf * qd_f).astype(BF16), s.astype(BF16))
             + _dot((qf * qd_b).astype(BF16), sb_scr[n]))
        mu = jnp.mean(o, axis=-1, keepdims=True)
        ctr = o - mu
        var = jnp.mean(ctr * ctr, axis=-1, keepdims=True)
        on = ctr * lax.rsqrt(var + EPS) * gn_ref[...]
        y_ref[0, rows(n), :] = (g_ref[0, rows(n), :].astype(F32) * on).astype(BF16)
        s_scr[...] = s * cd_f + _dot_tn((kf * kd_f).astype(BF16), v)
        return carry

    lax.fori_loop(0, nc, fwd, 0, unroll=min(4, nc))
    sf_ref[0, 0] = s_scr[...]


def _ret_mix(dec_f, dec_b, q, k, v, gate, gn_g, s0_f, s0_b):
    bsz, t, _ = q.shape
    nh, dk, dv = RET_HEADS, RET_QK_DIM, RET_V_DIM
    chunk = min(RET_CHUNK, t)
    seq = lambda n: pl.BlockSpec((1, t, n), lambda b, h: (b, 0, h))
    dec = pl.BlockSpec((1, 1, LANES), lambda b, h: (h, 0, 0))
    state = pl.BlockSpec((1, 1, dk, dv), lambda b, h: (b, h, 0, 0))
    return pl.pallas_call(
        functools.partial(_ret_mix_kernel, chunk=chunk),
        grid=(bsz, nh),
        in_specs=[dec, dec, seq(dk), seq(dk), seq(dv), seq(dv),
                  pl.BlockSpec((1, dv), lambda b, h: (0, h)), state, state],
        out_specs=[seq(dv), state, state],
        out_shape=[jax.ShapeDtypeStruct((bsz, t, nh * dv), BF16),
                   jax.ShapeDtypeStruct((bsz, nh, dk, dv), F32),
                   jax.ShapeDtypeStruct((bsz, nh, dk, dv), F32)],
        scratch_shapes=[pltpu.VMEM((dk, dv), F32), pltpu.VMEM((t // chunk, dk, dv), BF16)],
        compiler_params=_params("parallel", "parallel"),
        name="ret_mix",
    )(dec_f, dec_b, q, k, v, gate, gn_g.reshape(1, nh * dv), s0_f, s0_b)


def _qkv_proj_kernel(h_ref, mod_ref, g_ref, w_ref, *rest, widths, q_scale, rope):
    if rope:
        cos_ref, sin_ref = rest[:2]
        outs = rest[2:]
    else:
        outs = rest
    shift, scale, _ = _mod_rows(mod_ref, 1)
    xb = _ada(h_ref[0], g_ref[...], shift, scale).astype(BF16)
    col = 0
    for idx, (o_ref, n) in enumerate(zip(outs, widths)):
        if o_ref is not None:
            y = _dot(xb, w_ref[:, col:col + n])
            for c0 in range(0, n, LANES):
                yc = y[:, c0:c0 + LANES]
                if rope and idx < 2:
                    lane = lax.broadcasted_iota(jnp.int32, yc.shape, 1)
                    partner = jnp.where((lane & 16) == 0, pltpu.roll(yc, LANES - 16, axis=1),
                                        pltpu.roll(yc, 16, axis=1))
                    yc = yc * cos_ref[...] + partner * sin_ref[...]
                if idx == 0:
                    yc = yc * q_scale
                o_ref[0, :, c0:c0 + LANES] = yc.astype(BF16)
        col += n


def _qkv_proj(h, mod, g, w, widths, q_scale, cos=None, sin=None, want_q=True):
    bsz, t, d = h.shape
    tm = min(PROJ_TILE, t)
    rope = cos is not None
    row = lambda n: pl.BlockSpec((1, tm, n), lambda b, i: (b, i, 0))
    in_specs = [row(d), _mod_spec(mod, d), pl.BlockSpec((1, d), lambda b, i: (0, 0)), _resident(w.shape)]
    args = [h, mod, g.reshape(1, d), w]
    if rope:
        in_specs += [pl.BlockSpec((tm, LANES), lambda b, i: (i, 0))] * 2
        args += [cos, sin]
    keep = [want_q, True, True]
    out_widths = [n for n, kp in zip(widths, keep) if kp]

    def body(*refs):
        n_in = len(args)
        outs = list(refs[n_in:])
        full = [outs.pop(0) if kp else None for kp in keep]
        _qkv_proj_kernel(*refs[:n_in], *full, widths=widths, q_scale=q_scale, rope=rope)

    res = pl.pallas_call(
        body,
        grid=(bsz, t // tm),
        in_specs=in_specs,
        out_specs=[row(n) for n in out_widths],
        out_shape=[jax.ShapeDtypeStruct((bsz, t, n), BF16) for n in out_widths],
        compiler_params=_params("parallel", "parallel"),
        name="qkv_proj",
    )(*args)
    return res if want_q else [None] + list(res)


def _split_heads(x):
    lane = lax.broadcasted_iota(jnp.int32, x.shape, 1)
    zero = jnp.zeros_like(x)
    return jnp.concatenate([jnp.where(lane < 64, x, zero), jnp.where(lane >= 64, x, zero)], axis=0)


def _merge_heads(o):
    r = o.shape[0] // 2
    lane = lax.broadcasted_iota(jnp.int32, (r, o.shape[1]), 1)
    return jnp.where(lane < 64, o[:r], o[r:])


def _stage_transposed(dst_ref, src_ref, row0, width):
    for c0 in range(0, width, XPOSE_CHUNK):
        n = min(XPOSE_CHUNK, width - c0)
        dst_ref[:, c0:c0 + n] = src_ref[0, row0 + c0:row0 + c0 + n, :].T


def _nat_kernel(q_ref, k_ref, v_ref, kc_ref, vc_ref, bias_ref, o_ref, kt_scr, snb_a, scx_a, snb_b, scx_b, *,
                rows):
    w = GRID_W
    kh = min(NAT_KH, rows)
    nk = kh * w
    t = rows * w
    _stage_transposed(kt_scr.at[0], k_ref, 0, t)
    _stage_transposed(kt_scr.at[1], k_ref, w, t - 2 * w)
    kct = kc_ref[0].T
    vc = vc_ref[0]

    def window(r):
        r0 = jnp.clip(r - kh // 2, 0, rows - kh)
        return r0, pl.multiple_of(r * w, w)

    def scores(r, snb_ref, scx_ref):
        r0, qs = window(r)
        ty = r - r0
        par = r0 & 1
        kts = pl.multiple_of((r0 - par) * w, 2 * w)
        q2 = _split_heads(q_ref[0, pl.ds(qs, w), :])
        bias = jnp.concatenate(
            [jnp.concatenate([bias_ref[hd, 2 * m - ty + NAT_KH - 1] for m in range(kh // 2)], axis=1)
             for hd in range(2)], axis=0)
        snb_ref[...] = _dot(q2, kt_scr[par, :, pl.ds(kts, nk)]) + bias
        scx_ref[...] = _dot(q2, kct)

    def finish(r, snb_ref, scx_ref):
        r0, qs = window(r)
        ks = pl.multiple_of(r0 * w, w)
        s_nb, s_cx = snb_ref[...], scx_ref[...]
        m = jnp.maximum(jnp.max(s_nb, axis=-1, keepdims=True), jnp.max(s_cx, axis=-1, keepdims=True))
        p_nb = jnp.exp(s_nb - m)
        p_cx = jnp.exp(s_cx - m)
        l = jnp.sum(p_nb, axis=-1, keepdims=True) + jnp.sum(p_cx, axis=-1, keepdims=True)
        o = (_dot(p_nb.astype(BF16), v_ref[0, pl.ds(ks, nk), :]) + _dot(p_cx.astype(BF16), vc)) / l
        o_ref[0, pl.ds(qs, w), :] = _merge_heads(o).astype(BF16)

    scores(0, snb_a, scx_a)

    def body(i, carry):
        r = 2 * i
        scores(r + 1, snb_b, scx_b)
        finish(r, snb_a, scx_a)
        scores(jnp.minimum(r + 2, rows - 1), snb_a, scx_a)
        finish(r + 1, snb_b, scx_b)
        return carry

    lax.fori_loop(0, rows // 2, body, 0, unroll=4)


def _nat_attention(q, k, v, kc, vc, bias):
    bsz, t, d = q.shape
    l = kc.shape[1]
    rows = t // GRID_W
    hp = d // LANES
    lat = pl.BlockSpec((1, t, LANES), lambda p, b: (b, 0, p))
    ctx = pl.BlockSpec((1, l, LANES), lambda p, b: (b, 0, p))
    return pl.pallas_call(
        functools.partial(_nat_kernel, rows=rows),
        grid=(hp, bsz),
        in_specs=[lat, lat, lat, ctx, ctx,
                  pl.BlockSpec((2,) + bias.shape[1:], lambda p, b: (p, 0, 0, 0))],
        out_specs=lat,
        out_shape=jax.ShapeDtypeStruct((bsz, t, d), BF16),
        scratch_shapes=[pltpu.VMEM((2, LANES, t), BF16)]
        + [pltpu.VMEM((2 * GRID_W, n), F32) for n in (min(NAT_KH, rows) * GRID_W, l)] * 2,
        compiler_params=_params("parallel", "parallel"),
        name="nat_attention",
    )(q, k, v, kc, vc, bias)


def _nat_bias(rpb):
    w = GRID_W
    nh, nr, ncol = rpb.shape
    left = w - NAT_KW
    v = jnp.pad(rpb.astype(F32), ((0, 0), (0, 0), (left, 2 * w - ncol - left)))
    toep = jnp.tile(v, (1, 1, w))[..., :w * (2 * w - 1)].reshape(nh, nr, w, 2 * w - 1)[..., w - 1:]
    c = jnp.arange(w)[:, None]
    cc = jnp.arange(w)[None, :]
    c0 = jnp.clip(c - NAT_KW // 2, 0, w - NAT_KW)
    blocks = jnp.where((cc >= c0) & (cc < c0 + NAT_KW), toep, NEG_INF)
    return jnp.concatenate([blocks[:, :-1], blocks[:, 1:]], axis=-1)


def _ctx_attn_kernel(q_ref, k_ref, v_ref, o_ref):
    q2 = _split_heads(q_ref[0])
    s = _dot(q2, k_ref[0].T)
    p = jnp.exp(s - jnp.max(s, axis=-1, keepdims=True))
    l = jnp.sum(p, axis=-1, keepdims=True)
    o_ref[0] = _merge_heads(_dot(p.astype(BF16), v_ref[0]) / l).astype(BF16)


def _ctx_attention(q, k, v):
    bsz, l, d = q.shape
    blk = pl.BlockSpec((1, l, LANES), lambda b, p: (b, 0, p))
    return pl.pallas_call(
        _ctx_attn_kernel,
        grid=(bsz, d // LANES),
        in_specs=[blk, blk, blk],
        out_specs=blk,
        out_shape=jax.ShapeDtypeStruct((bsz, l, d), BF16),
        compiler_params=_params("parallel", "parallel"),
        name="ctx_attention",
    )(q, k, v)


POOL_HALO = SUBLANES


def _split3(x):
    hi = x.astype(BF16)
    r1 = x - hi.astype(F32)
    mid = r1.astype(BF16)
    return hi, mid, (r1 - mid.astype(F32)).astype(BF16)


def _pool_kernel(h_ref, prev_ref, next_ref, mod_ref, g_ref, pw_ref, ps_ref, o_ref, band_scr, *, tm, t_total):
    i = pl.program_id(1)
    nt = pl.num_programs(1)
    hl = POOL_HALO
    x = h_ref[0]
    d = x.shape[-1]
    gd = d // len(POOL_WINDOWS)
    shift, scale, gate = _mod_rows(mod_ref, 1)

    @pl.when((pl.program_id(0) == 0) & (i == 0))
    def _():
        off = lax.broadcasted_iota(jnp.int32, (tm, tm), 1) - lax.broadcasted_iota(jnp.int32, (tm, tm), 0)
        for gi, win in enumerate(POOL_WINDOWS):
            band_scr[gi] = jnp.where((off >= -(win // 2)) & (off < win // 2), 1.0, 0.0).astype(BF16)

    xe = _ada(jnp.concatenate([prev_ref[0], x, next_ref[0]], axis=0), g_ref[...], shift, scale)
    xn = xe[hl:hl + tm]
    halo = jnp.concatenate([jnp.where(i > 0, xe[:hl], 0.0), jnp.where(i < nt - 1, xe[hl + tm:], 0.0)], axis=0)
    x3 = _split3(xn)
    h3 = _split3(halo)
    er = lax.broadcasted_iota(jnp.int32, (2 * hl, 2 * hl), 0)
    hu = lax.broadcasted_iota(jnp.int32, (2 * hl, 2 * hl), 1)
    dist = jnp.where(hu < hl, hu - hl - er, hu - er + hl)
    same_side = (er < hl) == (hu < hl)
    trow = i * tm + lax.broadcasted_iota(jnp.int32, (tm, 1), 0)
    parts = []
    for gi, win in enumerate(POOL_WINDOWS):
        half = win // 2
        cols = slice(gi * gd, (gi + 1) * gd)
        band = band_scr[gi]
        edge = jnp.where(same_side & (dist >= -half) & (dist < half), 1.0, 0.0).astype(BF16)
        tot = _dot(band, x3[0][:, cols]) + _dot(band, x3[1][:, cols]) + _dot(band, x3[2][:, cols])
        fix = _dot(edge, h3[0][:, cols]) + _dot(edge, h3[1][:, cols]) + _dot(edge, h3[2][:, cols])
        tot = jnp.concatenate([tot[:hl] + fix[:hl], tot[hl:tm - hl], tot[tm - hl:] + fix[hl:]], axis=0)
        cnt = (jnp.minimum(trow + half, t_total) - jnp.maximum(trow - half, 0)).astype(F32)
        pooled = (tot / cnt - xn[:, cols]).astype(BF16)
        parts.append(_dot(pooled, pw_ref[gi]))
    y = jnp.concatenate(parts, axis=-1) * ps_ref[...]
    o_ref[0] = x + gate * y


def _pool(h, mod, g, pw, ps):
    bsz, t, d = h.shape
    tm = TOKEN_TILE
    per = tm // POOL_HALO
    last = t // POOL_HALO - 1
    return pl.pallas_call(
        functools.partial(_pool_kernel, tm=tm, t_total=t),
        grid=(bsz, t // tm),
        in_specs=[pl.BlockSpec((1, tm, d), lambda b, i: (b, i, 0)),
                  pl.BlockSpec((1, POOL_HALO, d), lambda b, i: (b, jnp.maximum(i * per - 1, 0), 0)),
                  pl.BlockSpec((1, POOL_HALO, d), lambda b, i: (b, jnp.minimum((i + 1) * per, last), 0)),
                  _mod_spec(mod, d),
                  pl.BlockSpec((1, d), lambda b, i: (0, 0)),
                  _resident(pw.shape),
                  pl.BlockSpec((1, d), lambda b, i: (0, 0))],
        out_specs=pl.BlockSpec((1, tm, d), lambda b, i: (b, i, 0)),
        out_shape=jax.ShapeDtypeStruct((bsz, t, d), F32),
        scratch_shapes=[pltpu.VMEM((len(POOL_WINDOWS), tm, tm), BF16)],
        compiler_params=_params("arbitrary", "arbitrary"),
        name="pool",
    )(h, h, h, mod, g.reshape(1, d), pw, ps.reshape(1, d))


def _swa_kernel(sink_ref, q_ref, k_ref, v_ref, kc_ref, vc_ref, o_ref, kt_scr, sloc_a, scx_a, sloc_b, scx_b, *,
                t_total):
    kv = pl.program_id(1)
    blk = SWA_BLOCK
    nb = t_total // blk
    grp = SWA_Q_HEADS // SWA_KV_HEADS
    nrow = grp * blk
    _stage_transposed(kt_scr, k_ref, 0, t_total)
    kct = kc_ref[0].T
    vc = vc_ref[0]
    row = lax.broadcasted_iota(jnp.int32, (nrow, 1), 0)
    sink = jnp.zeros((nrow, 1), F32)
    for gi in range(grp):
        sink = jnp.where((row >= gi * blk) & (row < (gi + 1) * blk), sink_ref[kv * grp + gi], sink)
    qi = lax.broadcasted_iota(jnp.int32, (nrow, blk), 0) & (blk - 1)
    kj = lax.broadcasted_iota(jnp.int32, (nrow, blk), 1)
    open_blk = jnp.zeros((nrow, blk), F32)
    prev_blk = jnp.where(kj >= qi, 0.0, NEG_INF)
    next_blk = jnp.where(kj <= qi, 0.0, NEG_INF)

    shut_blk = jnp.full((nrow, blk), NEG_INF, F32)
    span = 3 * blk
    mask_first = jnp.concatenate([open_blk, next_blk, shut_blk], axis=1)
    mask_mid = jnp.concatenate([prev_blk, open_blk, next_blk], axis=1)
    mask_last = jnp.concatenate([shut_blk, prev_blk, open_blk], axis=1)

    def offsets(n):
        qs = pl.multiple_of(n * blk, blk)
        return qs, pl.multiple_of(jnp.clip(qs - blk, 0, t_total - span), blk)

    def scores(n, mask, sloc_ref, scx_ref):
        qs, ks = offsets(n)
        qb = q_ref[0, pl.ds(qs, blk), :]
        q4 = jnp.concatenate([_split_heads(qb[:, :LANES]), _split_heads(qb[:, LANES:])], axis=0)
        sloc_ref[...] = _dot(q4, kt_scr[:, pl.ds(ks, span)]) + mask
        scx_ref[...] = _dot(q4, kct)

    def finish(n, sloc_ref, scx_ref):
        qs, ks = offsets(n)
        s_loc, s_cx = sloc_ref[...], scx_ref[...]
        m = jnp.maximum(jnp.maximum(jnp.max(s_loc, axis=-1, keepdims=True),
                                    jnp.max(s_cx, axis=-1, keepdims=True)), sink)
        p_loc = jnp.exp(s_loc - m)
        p_cx = jnp.exp(s_cx - m)
        l = jnp.sum(p_loc, axis=-1, keepdims=True) + jnp.sum(p_cx, axis=-1, keepdims=True) + jnp.exp(sink - m)
        o = (_dot(p_loc.astype(BF16), v_ref[0, pl.ds(ks, span), :]) + _dot(p_cx.astype(BF16), vc)) / l
        out = jnp.concatenate([_merge_heads(o[:2 * blk]), _merge_heads(o[2 * blk:])], axis=-1)
        o_ref[0, pl.ds(qs, blk), :] = out.astype(BF16)

    slot_a, slot_b = (sloc_a, scx_a), (sloc_b, scx_b)
    scores(0, mask_first, *slot_a)
    scores(1, mask_mid, *slot_b)
    finish(0, *slot_a)

    def body(i, carry):
        n = 1 + 2 * i
        scores(n + 1, mask_mid, *slot_a)
        finish(n, *slot_b)
        scores(n + 2, mask_mid, *slot_b)
        finish(n + 1, *slot_a)
        return carry

    lax.fori_loop(0, (nb - 4) // 2, body, 0, unroll=2)
    scores(nb - 2, mask_mid, *slot_a)
    finish(nb - 3, *slot_b)
    scores(nb - 1, mask_last, *slot_b)
    finish(nb - 2, *slot_a)
    finish(nb - 1, *slot_b)


def _swa_attention(sink, q, k, v, kc, vc):
    bsz, t, d = q.shape
    l = kc.shape[1]
    nb = t // SWA_BLOCK
    assert nb >= 4 and nb % 2 == 0 and SWA_WINDOW == SWA_BLOCK
    qw = d // SWA_KV_HEADS
    grid_spec = pltpu.PrefetchScalarGridSpec(
        num_scalar_prefetch=1,
        grid=(bsz, SWA_KV_HEADS),
        in_specs=[pl.BlockSpec((1, t, qw), lambda b, h, s: (b, 0, h)),
                  pl.BlockSpec((1, t, LANES), lambda b, h, s: (b, 0, h)),
                  pl.BlockSpec((1, t, LANES), lambda b, h, s: (b, 0, h)),
                  pl.BlockSpec((1, l, LANES), lambda b, h, s: (b, 0, h)),
                  pl.BlockSpec((1, l, LANES), lambda b, h, s: (b, 0, h))],
        out_specs=pl.BlockSpec((1, t, qw), lambda b, h, s: (b, 0, h)),
        scratch_shapes=[pltpu.VMEM((LANES, t), BF16)]
        + [pltpu.VMEM((SWA_Q_HEADS // SWA_KV_HEADS * SWA_BLOCK, n), F32) for n in (3 * SWA_BLOCK, l)] * 2,
    )
    return pl.pallas_call(
        functools.partial(_swa_kernel, t_total=t),
        grid_spec=grid_spec,
        out_shape=jax.ShapeDtypeStruct((bsz, t, d), BF16),
        compiler_params=_params("parallel", "parallel"),
        name="swa_attention",
    )(sink, q, k, v, kc, vc)


def _rope_angles(positions, dim):
    seg = dim // len(positions)
    inv = ROPE_BASE ** (-jnp.arange(0, seg, 2, dtype=F32) / seg)
    return jnp.concatenate([jnp.tile(p.astype(F32)[:, None] * inv, (1, 2)) for p in positions], axis=-1)


def _dup_heads(w, heads, dh):
    d = w.shape[0]
    return jnp.broadcast_to(w.reshape(d, heads, 1, dh), (d, heads, 2, dh)).reshape(d, heads * 2 * dh)


def kernel(x, c, ctx, c_ctx, w_mod, b_mod, norm_g, ffn_w_in, ffn_w_out, ret_w_in, ret_w_out, ret_gn_g, ret_decay_f, ret_decay_b, nat_w_qkv, nat_w_o, nat_rpb, pool_w, pool_scale, swa_w_qkv, swa_w_o, swa_sink, final_norm_g):
    bsz, t, d = x.shape
    depth = w_mod.shape[0]
    cc = jnp.concatenate([c, c_ctx[None], jnp.zeros((SUBLANES - bsz - 1, d), F32)], axis=0)
    mods = _modulation(cc, w_mod, b_mod)
    w_in = ffn_w_in.astype(BF16)
    w_out = ffn_w_out.astype(BF16)
    h, hc = x, ctx
    for i in range(depth):
        kind, occ = i % N_MIXERS, i // N_MIXERS
        last = i == depth - 1
        ctx_live = (not last) or kind != 2
        ml = mods[i, :bsz].reshape(bsz, 9, d)
        mc = mods[i, bsz:bsz + 1].reshape(1, 9, d)
        h = _ffn(h, ml, norm_g[i, 0], w_in, w_out, i, 0)
        if ctx_live:
            hc = _ffn(hc, mc, norm_g[i, 0], w_in, w_out, i, 0)
        g1 = norm_g[i, 1]
        yl = yc = None
        if kind == 0:
            assert not last, "retention as the last layer is not wired up"
            w = ret_w_in[occ].astype(BF16)
            wo = ret_w_out[occ].astype(BF16)
            ang = _rope_angles([jnp.arange(t)], RET_QK_DIM)[:, :RET_QK_DIM // 2]
            dec_f = jnp.broadcast_to(ret_decay_f[occ].astype(F32)[:, None, None], (RET_HEADS, 1, LANES))
            dec_b = jnp.broadcast_to(ret_decay_b[occ].astype(F32)[:, None, None], (RET_HEADS, 1, LANES))
            qc, kc, vc, gc = _ret_proj(hc, mc, g1, w)
            zeros = jnp.zeros((bsz, RET_HEADS, RET_QK_DIM, RET_V_DIM), F32)
            oc, s_f, s_b = _ret_mix(dec_f, dec_b, qc, kc, vc, gc, ret_gn_g[occ], zeros, zeros)
            ql, kl, vl, gl = _ret_proj(h, ml, g1, w, jnp.cos(ang), jnp.sin(ang))
            ol, _, _ = _ret_mix(dec_f, dec_b, ql, kl, vl, gl, ret_gn_g[occ], s_f, s_b)
            yl, yc = (ol, wo), (oc, wo)
        elif kind == 1:
            w = nat_w_qkv[occ].astype(BF16)
            wo = nat_w_o[occ].astype(BF16)
            widths = (d, d, d)
            qc, kc, vc = _qkv_proj(hc, mc, g1, w, widths, NAT_HEAD_DIM ** -0.5, want_q=not last)
            ql, kl, vl = _qkv_proj(h, ml, g1, w, widths, NAT_HEAD_DIM ** -0.5)
            yl = (_nat_attention(ql, kl, vl, kc, vc, _nat_bias(nat_rpb[occ])), wo)
            if not last:
                yc = (_ctx_attention(qc, kc, vc), wo)
        elif kind == 2:
            pw = pool_w[occ].astype(BF16)
            h_new = _pool(h, ml, g1, pw, pool_scale[occ])
            if not last:
                hc = _pool(hc, mc, g1, pw, pool_scale[occ])
            h = h_new
        else:
            assert last, "windowed attention with live context outputs is not wired up"
            nq = SWA_Q_HEADS * SWA_HEAD_DIM
            nkv = SWA_KV_HEADS * SWA_HEAD_DIM
            wq = swa_w_qkv[occ][:, :nq]
            wk = _dup_heads(swa_w_qkv[occ][:, nq:nq + nkv], SWA_KV_HEADS, SWA_HEAD_DIM)
            wv = _dup_heads(swa_w_qkv[occ][:, nq + nkv:], SWA_KV_HEADS, SWA_HEAD_DIM)
            w = jnp.concatenate([wq, wk, wv], axis=-1).astype(BF16)
            wo = swa_w_o[occ].astype(BF16)
            widths = (nq, 2 * nkv, 2 * nkv)
            tt = jnp.arange(t)
            ang = _rope_angles([tt // GRID_W, tt % GRID_W], SWA_HEAD_DIM)
            ang = jnp.tile(ang, (1, LANES // SWA_HEAD_DIM))
            lane = jnp.arange(LANES)
            sin = jnp.where(lane % 32 < 16, -jnp.sin(ang), jnp.sin(ang))
            _, kc, vc = _qkv_proj(hc, mc, g1, w, widths, SWA_HEAD_DIM ** -0.5, want_q=False)
            ql, kl, vl = _qkv_proj(h, ml, g1, w, widths, SWA_HEAD_DIM ** -0.5, jnp.cos(ang), sin)
            yl = (_swa_attention(swa_sink[occ].astype(F32), ql, kl, vl, kc, vc), wo)
        if not last:
            hc = _ffn(hc, mc, norm_g[i, 2], w_in, w_out, i, 1, yc)
        h = _ffn(h, ml, norm_g[i, 2], w_in, w_out, i, 1, yl, final_norm_g if last else None)
    return h
```

```python
import functools
import math

import jax
import jax.numpy as jnp
from jax import lax
from jax.experimental import pallas as pl
from jax.experimental.pallas import tpu as pltpu

F32 = jnp.float32
BF16 = jnp.bfloat16

EPS = 1e-6
NEG_INF = -1e30
LOG2E = math.log2(math.e)
ROPE_BASE = 10000.0
GRID_W = 64
N_MIXERS = 4
FFN_HIDDEN = 2816
RET_HEADS = 4
RET_QK_DIM = 256
RET_V_DIM = 512
NAT_HEADS = 16
NAT_HEAD_DIM = 64
NAT_KH = 8
NAT_KW = 16
POOL_WINDOWS = (2, 4, 8, 16)
SWA_Q_HEADS = 16
SWA_KV_HEADS = 4
SWA_HEAD_DIM = 64
SWA_WINDOW = 128
SWA_BLOCK = 128

LANES = 128
SUBLANES = 8
VMEM_LIMIT = 56 * 1024 * 1024
TOKEN_TILE = 256
PROJ_TILE = 512
FFN_TILE = 1024
FFN_SUBTILE = 256
RET_CHUNK = 256
XPOSE_CHUNK = 512


def _params(*sem):
    return pltpu.CompilerParams(dimension_semantics=sem, vmem_limit_bytes=VMEM_LIMIT)


def _resident(shape):
    nd = len(shape)
    return pl.BlockSpec(shape, lambda *_: (0,) * nd, pipeline_mode=pl.Buffered(1))


def _silu(x):
    return x * jax.nn.sigmoid(x)


def _ada(x, g, shift, scale):
    var = jnp.mean(x * x, axis=-1, keepdims=True)
    y = x * lax.rsqrt(var + EPS) * g
    return y * (1.0 + scale) + shift


def _mod_rows(mod_ref, j):
    return (mod_ref[0, 3 * j:3 * j + 1, :], mod_ref[0, 3 * j + 1:3 * j + 2, :],
            mod_ref[0, 3 * j + 2:3 * j + 3, :])


def _mod_spec(mod, d):
    if mod.shape[0] == 1:
        return pl.BlockSpec((1, 9, d), lambda b, t: (0, 0, 0))
    return pl.BlockSpec((1, 9, d), lambda b, t: (b, 0, 0))


def _dot(a, b):
    return jnp.dot(a, b, preferred_element_type=F32)


def _dot_nt(a, b):
    return lax.dot_general(a, b, (((1,), (1,)), ((), ())), preferred_element_type=F32)


def _dot_tn(a, b):
    return lax.dot_general(a, b, (((0,), (0,)), ((), ())), preferred_element_type=F32)


def _mod_kernel(c_ref, w_ref, b_ref, o_ref):
    s = _silu(c_ref[...]).astype(BF16)
    o_ref[0] = _dot(s, w_ref[0].astype(BF16)) + b_ref[0]


def _modulation(cc, w_mod, b_mod):
    depth, d, n = w_mod.shape
    tn = 2304
    return pl.pallas_call(
        _mod_kernel,
        grid=(depth, n // tn),
        in_specs=[pl.BlockSpec((SUBLANES, d), lambda l, j: (0, 0)),
                  pl.BlockSpec((1, d, tn), lambda l, j: (l, 0, j)),
                  pl.BlockSpec((1, 1, tn), lambda l, j: (l, 0, j))],
        out_specs=pl.BlockSpec((1, SUBLANES, tn), lambda l, j: (l, 0, j)),
        out_shape=jax.ShapeDtypeStruct((depth, SUBLANES, n), F32),
        compiler_params=_params("arbitrary", "arbitrary"),
        name="modulation",
    )(cc, w_mod, b_mod.reshape(depth, 1, n))


def _ffn_kernel(h_ref, mod_ref, g_ref, win_ref, wout_ref, *rest, j, mixer_out, final):
    o_ref = rest[-1]
    shift, scale, gate = _mod_rows(mod_ref, j)
    f = FFN_HIDDEN
    tm = h_ref.shape[1]
    sub = min(tm, FFN_SUBTILE)
    for r0 in range(0, tm, sub):
        x = h_ref[0, r0:r0 + sub, :]
        if mixer_out:
            y_ref, wo_ref = rest[:2]
            x = x + mod_ref[0, 5:6, :] * _dot(y_ref[0, r0:r0 + sub, :], wo_ref[...])
        xb = _ada(x, g_ref[...], shift, scale).astype(BF16)
        a = _dot(xb, win_ref[0, 0, :, :f])
        b = _dot(xb, win_ref[0, 0, :, f:])
        hid = (_silu(a) * b).astype(BF16)
        y = _dot(hid, wout_ref[0, 0])
        out = x + (0.5 * gate) * y
        if final:
            var = jnp.mean(out * out, axis=-1, keepdims=True)
            out = out * lax.rsqrt(var + EPS) * rest[-2][...]
        o_ref[0, r0:r0 + sub, :] = out


def _ffn(h, mod, g, w_in, w_out, layer, half, mixer_out=None, final_g=None):
    bsz, t, d = h.shape
    tm = min(FFN_TILE, t)
    row = lambda n: pl.BlockSpec((1, tm, n), lambda b, i: (b, i, 0))
    vec = pl.BlockSpec((1, d), lambda b, i: (0, 0))
    pick = lambda w: pl.BlockSpec((1, 1) + w.shape[2:], lambda b, i: (layer, half, 0, 0),
                                  pipeline_mode=pl.Buffered(1))
    in_specs = [row(d), _mod_spec(mod, d), vec, pick(w_in), pick(w_out)]
    args = [h, mod, g.reshape(1, d), w_in, w_out]
    if mixer_out is not None:
        y, wo = mixer_out
        in_specs += [row(y.shape[-1]), _resident(wo.shape)]
        args += [y, wo]
    if final_g is not None:
        in_specs.append(vec)
        args.append(final_g.reshape(1, d))
    return pl.pallas_call(
        functools.partial(_ffn_kernel, j=2 * half, mixer_out=mixer_out is not None, final=final_g is not None),
        grid=(bsz, t // tm),
        in_specs=in_specs,
        out_specs=row(d),
        out_shape=jax.ShapeDtypeStruct((bsz, t, d), F32),
        compiler_params=_params("parallel", "parallel"),
        name="ffn",
    )(*args)


def _ret_proj_kernel(h_ref, mod_ref, g_ref, w_ref, *rest, rope):
    if rope:
        cos_ref, sin_ref, q_ref, k_ref, v_ref, gate_ref = rest
    else:
        q_ref, k_ref, v_ref, gate_ref = rest
    shift, scale, _ = _mod_rows(mod_ref, 1)
    xb = _ada(h_ref[0], g_ref[...], shift, scale).astype(BF16)
    dk, nh = RET_QK_DIM, RET_HEADS
    half = dk // 2

    def rot(x):
        if not rope:
            return x.astype(BF16)
        c, s = cos_ref[...], sin_ref[...]
        x1, x2 = x[:, :half], x[:, half:]
        return jnp.concatenate([x1 * c - x2 * s, x2 * c + x1 * s], axis=-1).astype(BF16)

    for hd in range(nh):
        q_ref[0, :, hd * dk:(hd + 1) * dk] = rot(_dot(xb, w_ref[:, hd * dk:(hd + 1) * dk]))
        kcol = nh * dk + hd * dk
        k_ref[0, :, hd * dk:(hd + 1) * dk] = rot(_dot(xb, w_ref[:, kcol:kcol + dk]) * dk ** -0.5)
    v0 = 2 * nh * dk
    nv = nh * RET_V_DIM
    v_ref[0] = _dot(xb, w_ref[:, v0:v0 + nv]).astype(BF16)
    gate_ref[0] = _silu(_dot(xb, w_ref[:, v0 + nv:v0 + 2 * nv])).astype(BF16)


def _ret_proj(h, mod, g, w, cos=None, sin=None):
    bsz, t, d = h.shape
    tm = min(PROJ_TILE, t)
    rope = cos is not None
    nqk = RET_HEADS * RET_QK_DIM
    nv = RET_HEADS * RET_V_DIM
    row = lambda n: pl.BlockSpec((1, tm, n), lambda b, i: (b, i, 0))
    in_specs = [row(d), _mod_spec(mod, d), pl.BlockSpec((1, d), lambda b, i: (0, 0)), _resident(w.shape)]
    args = [h, mod, g.reshape(1, d), w]
    if rope:
        in_specs += [pl.BlockSpec((tm, RET_QK_DIM // 2), lambda b, i: (i, 0))] * 2
        args += [cos, sin]
    return pl.pallas_call(
        functools.partial(_ret_proj_kernel, rope=rope),
        grid=(bsz, t // tm),
        in_specs=in_specs,
        out_specs=[row(nqk), row(nqk), row(nv), row(nv)],
        out_shape=[jax.ShapeDtypeStruct((bsz, t, nqk), BF16), jax.ShapeDtypeStruct((bsz, t, nqk), BF16),
                   jax.ShapeDtypeStruct((bsz, t, nv), BF16), jax.ShapeDtypeStruct((bsz, t, nv), BF16)],
        compiler_params=_params("parallel", "parallel"),
        name="ret_proj",
    )(*args)


def _log_sigmoid(x):
    return jnp.minimum(x, 0.0) - jnp.log(1.0 + jnp.exp(-jnp.abs(x)))


def _ret_mix_kernel(decf_ref, decb_ref, q_ref, k_ref, v_ref, g_ref, gn_ref, s0f_ref, s0b_ref,
                    y_ref, sf_ref, sb_ref, s_scr, sb_scr, *, chunk):
    c = chunk
    nc = q_ref.shape[1] // c
    lgf = _log_sigmoid(decf_ref[0])[:, :1]
    lgb = _log_sigmoid(decb_ref[0])[:, :1]
    dist = lax.broadcasted_iota(jnp.int32, (c, c), 0) - lax.broadcasted_iota(jnp.int32, (c, c), 1)
    decay = jnp.where(dist >= 0,
                      jnp.exp(jnp.maximum(dist, 0).astype(F32) * lgf),
                      jnp.exp(jnp.maximum(-dist, 0).astype(F32) * lgb))
    pos = lax.broadcasted_iota(jnp.int32, (c, 1), 0).astype(F32)
    qd_f, kd_f, cd_f = jnp.exp((pos + 1.0) * lgf), jnp.exp((c - 1.0 - pos) * lgf), jnp.exp(c * lgf)
    qd_b, kd_b, cd_b = jnp.exp((c - pos) * lgb), jnp.exp(pos * lgb), jnp.exp(c * lgb)

    def rows(n):
        return pl.ds(pl.multiple_of(n * c, c), c)

    s_scr[...] = s0b_ref[0, 0]

    def bwd(i, carry):
        n = nc - 1 - i
        s = s_scr[...]
        sb_scr[n] = s.astype(BF16)
        kd = (k_ref[0, rows(n), :].astype(F32) * kd_b).astype(BF16)
        s_scr[...] = s * cd_b + _dot_tn(kd, v_ref[0, rows(n), :])
        return carry

    lax.fori_loop(0, nc, bwd, 0, unroll=min(4, nc))
    sb_ref[0, 0] = s_scr[...]
    s_scr[...] = s0f_ref[0, 0]

    def fwd(n, carry):
        q, k, v = q_ref[0, rows(n), :], k_ref[0, rows(n), :], v_ref[0, rows(n), :]
        qf, kf = q.astype(F32), k.astype(F32)
        s = s_scr[...]
        att = (_dot(q, k.T) * decay).astype(BF16)
        o = (_dot(att, v) + _dot((qf * qd_f).astype(BF16), s.astype(BF16))
             + _dot((qf * qd_b).astype(BF16), sb_scr[n]))
        mu = jnp.mean(o, axis=-1, keepdims=True)
        ctr = o - mu
        var = jnp.mean(ctr * ctr, axis=-1, keepdims=True)
        on = ctr * lax.rsqrt(var + EPS) * gn_ref[...]
        y_ref[0, rows(n), :] = (g_ref[0, rows(n), :].astype(F32) * on).astype(BF16)
        s_scr[...] = s * cd_f + _dot_tn((kf * kd_f).astype(BF16), v)
        return carry

    lax.fori_loop(0, nc, fwd, 0, unroll=min(4, nc))
    sf_ref[0, 0] = s_scr[...]


def _ret_mix(dec_f, dec_b, q, k, v, gate, gn_g, s0_f, s0_b):
    bsz, t, _ = q.shape
    nh, dk, dv = RET_HEADS, RET_QK_DIM, RET_V_DIM
    chunk = min(RET_CHUNK, t)
    seq = lambda n: pl.BlockSpec((1, t, n), lambda b, h: (b, 0, h))
    dec = pl.BlockSpec((1, 1, LANES), lambda b, h: (h, 0, 0))
    state = pl.BlockSpec((1, 1, dk, dv), lambda b, h: (b, h, 0, 0))
    return pl.pallas_call(
        functools.partial(_ret_mix_kernel, chunk=chunk),
        grid=(bsz, nh),
        in_specs=[dec, dec, seq(dk), seq(dk), seq(dv), seq(dv),
                  pl.BlockSpec((1, dv), lambda b, h: (0, h)), state, state],
        out_specs=[seq(dv), state, state],
        out_shape=[jax.ShapeDtypeStruct((bsz, t, nh * dv), BF16),
                   jax.ShapeDtypeStruct((bsz, nh, dk, dv), F32),
                   jax.ShapeDtypeStruct((bsz, nh, dk, dv), F32)],
        scratch_shapes=[pltpu.VMEM((dk, dv), F32), pltpu.VMEM((t // chunk, dk, dv), BF16)],
        compiler_params=_params("parallel", "parallel"),
        name="ret_mix",
    )(dec_f, dec_b, q, k, v, gate, gn_g.reshape(1, nh * dv), s0_f, s0_b)


def _qkv_proj_kernel(h_ref, mod_ref, g_ref, w_ref, *rest, widths, q_scale, rope):
    if rope:
        cos_ref, sin_ref = rest[:2]
        outs = rest[2:]
    else:
        outs = rest
    shift, scale, _ = _mod_rows(mod_ref, 1)
    xb = _ada(h_ref[0], g_ref[...], shift, scale).astype(BF16)
    col = 0
    for idx, (o_ref, n) in enumerate(zip(outs, widths)):
        if o_ref is not None:
            y = _dot(xb, w_ref[:, col:col + n])
            for c0 in range(0, n, LANES):
                yc = y[:, c0:c0 + LANES]
                if rope and idx < 2:
                    lane = lax.broadcasted_iota(jnp.int32, yc.shape, 1)
                    partner = jnp.where((lane & 16) == 0, pltpu.roll(yc, LANES - 16, axis=1),
                                        pltpu.roll(yc, 16, axis=1))
                    yc = yc * cos_ref[...] + partner * sin_ref[...]
                if idx == 0:
                    yc = yc * q_scale
                o_ref[0, :, c0:c0 + LANES] = yc.astype(BF16)
        col += n


def _qkv_proj(h, mod, g, w, widths, q_scale, cos=None, sin=None, want_q=True):
    bsz, t, d = h.shape
    tm = min(PROJ_TILE, t)
    rope = cos is not None
    row = lambda n: pl.BlockSpec((1, tm, n), lambda b, i: (b, i, 0))
    in_specs = [row(d), _mod_spec(mod, d), pl.BlockSpec((1, d), lambda b, i: (0, 0)), _resident(w.shape)]
    args = [h, mod, g.reshape(1, d), w]
    if rope:
        in_specs += [pl.BlockSpec((tm, LANES), lambda b, i: (i, 0))] * 2
        args += [cos, sin]
    keep = [want_q, True, True]
    out_widths = [n for n, kp in zip(widths, keep) if kp]

    def body(*refs):
        n_in = len(args)
        outs = list(refs[n_in:])
        full = [outs.pop(0) if kp else None for kp in keep]
        _qkv_proj_kernel(*refs[:n_in], *full, widths=widths, q_scale=q_scale, rope=rope)

    res = pl.pallas_call(
        body,
        grid=(bsz, t // tm),
        in_specs=in_specs,
        out_specs=[row(n) for n in out_widths],
        out_shape=[jax.ShapeDtypeStruct((bsz, t, n), BF16) for n in out_widths],
        compiler_params=_params("parallel", "parallel"),
        name="qkv_proj",
    )(*args)
    return res if want_q else [None] + list(res)


def _split_heads(x):
    lane = lax.broadcasted_iota(jnp.int32, x.shape, 1)
    zero = jnp.zeros_like(x)
    return jnp.concatenate([jnp.where(lane < 64, x, zero), jnp.where(lane >= 64, x, zero)], axis=0)


def _merge_heads(o):
    r = o.shape[0] // 2
    lane = lax.broadcasted_iota(jnp.int32, (r, o.shape[1]), 1)
    return jnp.where(lane < 64, o[:r], o[r:])


def _with_ones(v):
    return jnp.concatenate([v, jnp.ones_like(v)], axis=1)


def _stage_with_ones(dst_ref, src_ref):
    for c0 in range(0, src_ref.shape[1], XPOSE_CHUNK):
        n = min(XPOSE_CHUNK, src_ref.shape[1] - c0)
        dst_ref[c0:c0 + n, :] = _with_ones(src_ref[0, c0:c0 + n, :])


def _stage_transposed(dst_ref, src_ref, row0, width):
    for c0 in range(0, width, XPOSE_CHUNK):
        n = min(XPOSE_CHUNK, width - c0)
        dst_ref[:, c0:c0 + n] = src_ref[0, row0 + c0:row0 + c0 + n, :].T


def _nat_kernel(q_ref, k_ref, v_ref, kc_ref, vc_ref, bias_ref, o_ref, kt_scr, va_scr, snb_a, scx_a, snb_b, scx_b,
                *, rows):
    w = GRID_W
    kh = min(NAT_KH, rows)
    nk = kh * w
    t = rows * w
    _stage_transposed(kt_scr.at[0], k_ref, 0, t)
    _stage_transposed(kt_scr.at[1], k_ref, w, t - 2 * w)
    kct = kc_ref[0].T
    _stage_with_ones(va_scr, v_ref)
    vca = _with_ones(vc_ref[0])

    def window(r):
        r0 = jnp.clip(r - kh // 2, 0, rows - kh)
        return r0, pl.multiple_of(r * w, w)

    def scores(r, snb_ref, scx_ref):
        r0, qs = window(r)
        ty = r - r0
        par = r0 & 1
        kts = pl.multiple_of((r0 - par) * w, 2 * w)
        q2 = _split_heads(q_ref[0, pl.ds(qs, w), :])
        bias = jnp.concatenate(
            [jnp.concatenate([bias_ref[hd, 2 * m - ty + NAT_KH - 1] for m in range(kh // 2)], axis=1)
             for hd in range(2)], axis=0)
        snb_ref[...] = _dot(q2, kt_scr[par, :, pl.ds(kts, nk)]) + bias
        scx_ref[...] = _dot(q2, kct)

    def finish(r, snb_ref, scx_ref):
        r0, qs = window(r)
        ks = pl.multiple_of(r0 * w, w)
        s_nb, s_cx = snb_ref[...], scx_ref[...]
        m = jnp.maximum(jnp.max(s_nb, axis=-1, keepdims=True), jnp.max(s_cx, axis=-1, keepdims=True))
        p_nb = jnp.exp2(s_nb - m)
        p_cx = jnp.exp2(s_cx - m)
        oa = _dot(p_nb.astype(BF16), va_scr[pl.ds(ks, nk), :]) + _dot(p_cx.astype(BF16), vca)
        o_ref[0, pl.ds(qs, w), :] = _merge_heads(oa[:, :LANES] / oa[:, LANES:]).astype(BF16)

    scores(0, snb_a, scx_a)

    def body(i, carry):
        r = 2 * i
        scores(r + 1, snb_b, scx_b)
        finish(r, snb_a, scx_a)
        scores(jnp.minimum(r + 2, rows - 1), snb_a, scx_a)
        finish(r + 1, snb_b, scx_b)
        return carry

    lax.fori_loop(0, rows // 2, body, 0, unroll=4)


def _nat_attention(q, k, v, kc, vc, bias):
    bsz, t, d = q.shape
    l = kc.shape[1]
    rows = t // GRID_W
    hp = d // LANES
    lat = pl.BlockSpec((1, t, LANES), lambda p, b: (b, 0, p))
    ctx = pl.BlockSpec((1, l, LANES), lambda p, b: (b, 0, p))
    return pl.pallas_call(
        functools.partial(_nat_kernel, rows=rows),
        grid=(hp, bsz),
        in_specs=[lat, lat, lat, ctx, ctx,
                  pl.BlockSpec((2,) + bias.shape[1:], lambda p, b: (p, 0, 0, 0))],
        out_specs=lat,
        out_shape=jax.ShapeDtypeStruct((bsz, t, d), BF16),
        scratch_shapes=[pltpu.VMEM((2, LANES, t), BF16), pltpu.VMEM((t, 2 * LANES), BF16)]
        + [pltpu.VMEM((2 * GRID_W, n), F32) for n in (min(NAT_KH, rows) * GRID_W, l)] * 2,
        compiler_params=_params("parallel", "parallel"),
        name="nat_attention",
    )(q, k, v, kc, vc, bias)


def _nat_bias(rpb):
    w = GRID_W
    nh, nr, ncol = rpb.shape
    left = w - NAT_KW
    v = jnp.pad(rpb.astype(F32), ((0, 0), (0, 0), (left, 2 * w - ncol - left)))
    toep = jnp.tile(v, (1, 1, w))[..., :w * (2 * w - 1)].reshape(nh, nr, w, 2 * w - 1)[..., w - 1:]
    c = jnp.arange(w)[:, None]
    cc = jnp.arange(w)[None, :]
    c0 = jnp.clip(c - NAT_KW // 2, 0, w - NAT_KW)
    blocks = jnp.where((cc >= c0) & (cc < c0 + NAT_KW), toep * LOG2E, NEG_INF)
    return jnp.concatenate([blocks[:, :-1], blocks[:, 1:]], axis=-1)


def _ctx_attn_kernel(q_ref, k_ref, v_ref, o_ref):
    q2 = _split_heads(q_ref[0])
    s = _dot(q2, k_ref[0].T)
    p = jnp.exp2(s - jnp.max(s, axis=-1, keepdims=True))
    l = jnp.sum(p, axis=-1, keepdims=True)
    o_ref[0] = _merge_heads(_dot(p.astype(BF16), v_ref[0]) / l).astype(BF16)


def _ctx_attention(q, k, v):
    bsz, l, d = q.shape
    blk = pl.BlockSpec((1, l, LANES), lambda b, p: (b, 0, p))
    return pl.pallas_call(
        _ctx_attn_kernel,
        grid=(bsz, d // LANES),
        in_specs=[blk, blk, blk],
        out_specs=blk,
        out_shape=jax.ShapeDtypeStruct((bsz, l, d), BF16),
        compiler_params=_params("parallel", "parallel"),
        name="ctx_attention",
    )(q, k, v)


POOL_HALO = SUBLANES


def _split3(x):
    hi = x.astype(BF16)
    r1 = x - hi.astype(F32)
    mid = r1.astype(BF16)
    return hi, mid, (r1 - mid.astype(F32)).astype(BF16)


def _pool_kernel(h_ref, prev_ref, next_ref, mod_ref, g_ref, pw_ref, ps_ref, o_ref, band_scr, *, tm, t_total):
    i = pl.program_id(1)
    nt = pl.num_programs(1)
    hl = POOL_HALO
    x = h_ref[0]
    d = x.shape[-1]
    gd = d // len(POOL_WINDOWS)
    shift, scale, gate = _mod_rows(mod_ref, 1)

    @pl.when((pl.program_id(0) == 0) & (i == 0))
    def _():
        off = lax.broadcasted_iota(jnp.int32, (tm, tm), 1) - lax.broadcasted_iota(jnp.int32, (tm, tm), 0)
        for gi, win in enumerate(POOL_WINDOWS):
            band_scr[gi] = jnp.where((off >= -(win // 2)) & (off < win // 2), 1.0, 0.0).astype(BF16)

    xe = _ada(jnp.concatenate([prev_ref[0], x, next_ref[0]], axis=0), g_ref[...], shift, scale)
    xn = xe[hl:hl + tm]
    halo = jnp.concatenate([jnp.where(i > 0, xe[:hl], 0.0), jnp.where(i < nt - 1, xe[hl + tm:], 0.0)], axis=0)
    x3 = _split3(xn)
    h3 = _split3(halo)
    er = lax.broadcasted_iota(jnp.int32, (2 * hl, 2 * hl), 0)
    hu = lax.broadcasted_iota(jnp.int32, (2 * hl, 2 * hl), 1)
    dist = jnp.where(hu < hl, hu - hl - er, hu - er + hl)
    same_side = (er < hl) == (hu < hl)
    trow = i * tm + lax.broadcasted_iota(jnp.int32, (tm, 1), 0)
    parts = []
    for gi, win in enumerate(POOL_WINDOWS):
        half = win // 2
        cols = slice(gi * gd, (gi + 1) * gd)
        band = band_scr[gi]
        edge = jnp.where(same_side & (dist >= -half) & (dist < half), 1.0, 0.0).astype(BF16)
        tot = _dot(band, x3[0][:, cols]) + _dot(band, x3[1][:, cols]) + _dot(band, x3[2][:, cols])
        fix = _dot(edge, h3[0][:, cols]) + _dot(edge, h3[1][:, cols]) + _dot(edge, h3[2][:, cols])
        tot = jnp.concatenate([tot[:hl] + fix[:hl], tot[hl:tm - hl], tot[tm - hl:] + fix[hl:]], axis=0)
        cnt = (jnp.minimum(trow + half, t_total) - jnp.maximum(trow - half, 0)).astype(F32)
        pooled = (tot / cnt - xn[:, cols]).astype(BF16)
        parts.append(_dot(pooled, pw_ref[gi]))
    y = jnp.concatenate(parts, axis=-1) * ps_ref[...]
    o_ref[0] = x + gate * y


def _pool(h, mod, g, pw, ps):
    bsz, t, d = h.shape
    tm = TOKEN_TILE
    per = tm // POOL_HALO
    last = t // POOL_HALO - 1
    return pl.pallas_call(
        functools.partial(_pool_kernel, tm=tm, t_total=t),
        grid=(bsz, t // tm),
        in_specs=[pl.BlockSpec((1, tm, d), lambda b, i: (b, i, 0)),
                  pl.BlockSpec((1, POOL_HALO, d), lambda b, i: (b, jnp.maximum(i * per - 1, 0), 0)),
                  pl.BlockSpec((1, POOL_HALO, d), lambda b, i: (b, jnp.minimum((i + 1) * per, last), 0)),
                  _mod_spec(mod, d),
                  pl.BlockSpec((1, d), lambda b, i: (0, 0)),
                  _resident(pw.shape),
                  pl.BlockSpec((1, d), lambda b, i: (0, 0))],
        out_specs=pl.BlockSpec((1, tm, d), lambda b, i: (b, i, 0)),
        out_shape=jax.ShapeDtypeStruct((bsz, t, d), F32),
        scratch_shapes=[pltpu.VMEM((len(POOL_WINDOWS), tm, tm), BF16)],
        compiler_params=_params("arbitrary", "arbitrary"),
        name="pool",
    )(h, h, h, mod, g.reshape(1, d), pw, ps.reshape(1, d))


def _swa_kernel(sink_ref, q_ref, k_ref, v_ref, kc_ref, vc_ref, o_ref, kt_scr, va_scr, sloc_a, scx_a, sloc_b, scx_b,
                *, t_total):
    kv = pl.program_id(1)
    blk = SWA_BLOCK
    nb = t_total // blk
    grp = SWA_Q_HEADS // SWA_KV_HEADS
    nrow = grp * blk
    _stage_transposed(kt_scr, k_ref, 0, t_total)
    kct = kc_ref[0].T
    _stage_with_ones(va_scr, v_ref)
    vca = _with_ones(vc_ref[0])
    row = lax.broadcasted_iota(jnp.int32, (nrow, 1), 0)
    sink = jnp.zeros((nrow, 1), F32)
    for gi in range(grp):
        sink = jnp.where((row >= gi * blk) & (row < (gi + 1) * blk), sink_ref[kv * grp + gi] * LOG2E, sink)
    qi = lax.broadcasted_iota(jnp.int32, (nrow, blk), 0) & (blk - 1)
    kj = lax.broadcasted_iota(jnp.int32, (nrow, blk), 1)
    open_blk = jnp.zeros((nrow, blk), F32)
    prev_blk = jnp.where(kj >= qi, 0.0, NEG_INF)
    next_blk = jnp.where(kj <= qi, 0.0, NEG_INF)

    shut_blk = jnp.full((nrow, blk), NEG_INF, F32)
    span = 3 * blk
    mask_first = jnp.concatenate([open_blk, next_blk, shut_blk], axis=1)
    mask_mid = jnp.concatenate([prev_blk, open_blk, next_blk], axis=1)
    mask_last = jnp.concatenate([shut_blk, prev_blk, open_blk], axis=1)

    def offsets(n):
        qs = pl.multiple_of(n * blk, blk)
        return qs, pl.multiple_of(jnp.clip(qs - blk, 0, t_total - span), blk)

    def scores(n, mask, sloc_ref, scx_ref):
        qs, ks = offsets(n)
        qb = q_ref[0, pl.ds(qs, blk), :]
        q4 = jnp.concatenate([_split_heads(qb[:, :LANES]), _split_heads(qb[:, LANES:])], axis=0)
        sloc_ref[...] = _dot(q4, kt_scr[:, pl.ds(ks, span)]) + mask
        scx_ref[...] = _dot(q4, kct)

    def finish(n, sloc_ref, scx_ref):
        qs, ks = offsets(n)
        s_loc, s_cx = sloc_ref[...], scx_ref[...]
        m = jnp.maximum(jnp.maximum(jnp.max(s_loc, axis=-1, keepdims=True),
                                    jnp.max(s_cx, axis=-1, keepdims=True)), sink)
        p_loc = jnp.exp2(s_loc - m)
        p_cx = jnp.exp2(s_cx - m)
        oa = _dot(p_loc.astype(BF16), va_scr[pl.ds(ks, span), :]) + _dot(p_cx.astype(BF16), vca)
        o = oa[:, :LANES] / (oa[:, LANES:] + jnp.exp2(sink - m))
        out = jnp.concatenate([_merge_heads(o[:2 * blk]), _merge_heads(o[2 * blk:])], axis=-1)
        o_ref[0, pl.ds(qs, blk), :] = out.astype(BF16)

    slot_a, slot_b = (sloc_a, scx_a), (sloc_b, scx_b)
    scores(0, mask_first, *slot_a)
    scores(1, mask_mid, *slot_b)
    finish(0, *slot_a)

    def body(i, carry):
        n = 1 + 2 * i
        scores(n + 1, mask_mid, *slot_a)
        finish(n, *slot_b)
        scores(n + 2, mask_mid, *slot_b)
        finish(n + 1, *slot_a)
        return carry

    lax.fori_loop(0, (nb - 4) // 2, body, 0, unroll=2)
    scores(nb - 2, mask_mid, *slot_a)
    finish(nb - 3, *slot_b)
    scores(nb - 1, mask_last, *slot_b)
    finish(nb - 2, *slot_a)
    finish(nb - 1, *slot_b)


def _swa_attention(sink, q, k, v, kc, vc):
    bsz, t, d = q.shape
    l = kc.shape[1]
    nb = t // SWA_BLOCK
    assert nb >= 4 and nb % 2 == 0 and SWA_WINDOW == SWA_BLOCK
    qw = d // SWA_KV_HEADS
    grid_spec = pltpu.PrefetchScalarGridSpec(
        num_scalar_prefetch=1,
        grid=(bsz, SWA_KV_HEADS),
        in_specs=[pl.BlockSpec((1, t, qw), lambda b, h, s: (b, 0, h)),
                  pl.BlockSpec((1, t, LANES), lambda b, h, s: (b, 0, h)),
                  pl.BlockSpec((1, t, LANES), lambda b, h, s: (b, 0, h)),
                  pl.BlockSpec((1, l, LANES), lambda b, h, s: (b, 0, h)),
                  pl.BlockSpec((1, l, LANES), lambda b, h, s: (b, 0, h))],
        out_specs=pl.BlockSpec((1, t, qw), lambda b, h, s: (b, 0, h)),
        scratch_shapes=[pltpu.VMEM((LANES, t), BF16), pltpu.VMEM((t, 2 * LANES), BF16)]
        + [pltpu.VMEM((SWA_Q_HEADS // SWA_KV_HEADS * SWA_BLOCK, n), F32) for n in (3 * SWA_BLOCK, l)] * 2,
    )
    return pl.pallas_call(
        functools.partial(_swa_kernel, t_total=t),
        grid_spec=grid_spec,
        out_shape=jax.ShapeDtypeStruct((bsz, t, d), BF16),
        compiler_params=_params("parallel", "parallel"),
        name="swa_attention",
    )(sink, q, k, v, kc, vc)


def _rope_angles(positions, dim):
    seg = dim // len(positions)
    inv = ROPE_BASE ** (-jnp.arange(0, seg, 2, dtype=F32) / seg)
    return jnp.concatenate([jnp.tile(p.astype(F32)[:, None] * inv, (1, 2)) for p in positions], axis=-1)


def _dup_heads(w, heads, dh):
    d = w.shape[0]
    return jnp.broadcast_to(w.reshape(d, heads, 1, dh), (d, heads, 2, dh)).reshape(d, heads * 2 * dh)


def kernel(x, c, ctx, c_ctx, w_mod, b_mod, norm_g, ffn_w_in, ffn_w_out, ret_w_in, ret_w_out, ret_gn_g, ret_decay_f, ret_decay_b, nat_w_qkv, nat_w_o, nat_rpb, pool_w, pool_scale, swa_w_qkv, swa_w_o, swa_sink, final_norm_g):
    bsz, t, d = x.shape
    depth = w_mod.shape[0]
    cc = jnp.concatenate([c, c_ctx[None], jnp.zeros((SUBLANES - bsz - 1, d), F32)], axis=0)
    mods = _modulation(cc, w_mod, b_mod)
    w_in = ffn_w_in.astype(BF16)
    w_out = ffn_w_out.astype(BF16)
    h, hc = x, ctx
    for i in range(depth):
        kind, occ = i % N_MIXERS, i // N_MIXERS
        last = i == depth - 1
        ctx_live = (not last) or kind != 2
        ml = mods[i, :bsz].reshape(bsz, 9, d)
        mc = mods[i, bsz:bsz + 1].reshape(1, 9, d)
        h = _ffn(h, ml, norm_g[i, 0], w_in, w_out, i, 0)
        if ctx_live:
            hc = _ffn(hc, mc, norm_g[i, 0], w_in, w_out, i, 0)
        g1 = norm_g[i, 1]
        yl = yc = None
        if kind == 0:
            assert not last, "retention as the last layer is not wired up"
            w = ret_w_in[occ].astype(BF16)
            wo = ret_w_out[occ].astype(BF16)
            ang = _rope_angles([jnp.arange(t)], RET_QK_DIM)[:, :RET_QK_DIM // 2]
            dec_f = jnp.broadcast_to(ret_decay_f[occ].astype(F32)[:, None, None], (RET_HEADS, 1, LANES))
            dec_b = jnp.broadcast_to(ret_decay_b[occ].astype(F32)[:, None, None], (RET_HEADS, 1, LANES))
            qc, kc, vc, gc = _ret_proj(hc, mc, g1, w)
            zeros = jnp.zeros((bsz, RET_HEADS, RET_QK_DIM, RET_V_DIM), F32)
            oc, s_f, s_b = _ret_mix(dec_f, dec_b, qc, kc, vc, gc, ret_gn_g[occ], zeros, zeros)
            ql, kl, vl, gl = _ret_proj(h, ml, g1, w, jnp.cos(ang), jnp.sin(ang))
            ol, _, _ = _ret_mix(dec_f, dec_b, ql, kl, vl, gl, ret_gn_g[occ], s_f, s_b)
            yl, yc = (ol, wo), (oc, wo)
        elif kind == 1:
            w = nat_w_qkv[occ].astype(BF16)
            wo = nat_w_o[occ].astype(BF16)
            widths = (d, d, d)
            qc, kc, vc = _qkv_proj(hc, mc, g1, w, widths, NAT_HEAD_DIM ** -0.5 * LOG2E, want_q=not last)
            ql, kl, vl = _qkv_proj(h, ml, g1, w, widths, NAT_HEAD_DIM ** -0.5 * LOG2E)
            yl = (_nat_attention(ql, kl, vl, kc, vc, _nat_bias(nat_rpb[occ])), wo)
            if not last:
                yc = (_ctx_attention(qc, kc, vc), wo)
        elif kind == 2:
            pw = pool_w[occ].astype(BF16)
            h_new = _pool(h, ml, g1, pw, pool_scale[occ])
            if not last:
                hc = _pool(hc, mc, g1, pw, pool_scale[occ])
            h = h_new
        else:
            assert last, "windowed attention with live context outputs is not wired up"
            nq = SWA_Q_HEADS * SWA_HEAD_DIM
            nkv = SWA_KV_HEADS * SWA_HEAD_DIM
            wq = swa_w_qkv[occ][:, :nq]
            wk = _dup_heads(swa_w_qkv[occ][:, nq:nq + nkv], SWA_KV_HEADS, SWA_HEAD_DIM)
            wv = _dup_heads(swa_w_qkv[occ][:, nq + nkv:], SWA_KV_HEADS, SWA_HEAD_DIM)
            w = jnp.concatenate([wq, wk, wv], axis=-1).astype(BF16)
            wo = swa_w_o[occ].astype(BF16)
            widths = (nq, 2 * nkv, 2 * nkv)
            tt = jnp.arange(t)
            ang = _rope_angles([tt // GRID_W, tt % GRID_W], SWA_HEAD_DIM)
            ang = jnp.tile(ang, (1, LANES // SWA_HEAD_DIM))
            lane = jnp.arange(LANES)
            sin = jnp.where(lane % 32 < 16, -jnp.sin(ang), jnp.sin(ang))
            _, kc, vc = _qkv_proj(hc, mc, g1, w, widths, SWA_HEAD_DIM ** -0.5 * LOG2E, want_q=False)
            ql, kl, vl = _qkv_proj(h, ml, g1, w, widths, SWA_HEAD_DIM ** -0.5 * LOG2E, jnp.cos(ang), sin)
            yl = (_swa_attention(swa_sink[occ].astype(F32), ql, kl, vl, kc, vc), wo)
        if not last:
            hc = _ffn(hc, mc, norm_g[i, 2], w_in, w_out, i, 1, yc)
        h = _ffn(h, ml, norm_g[i, 2], w_in, w_out, i, 1, yl, final_norm_g if last else None)
    return h
```

```python
import functools
import math

import jax
import jax.numpy as jnp
from jax import lax
from jax.experimental import pallas as pl
from jax.experimental.pallas import tpu as pltpu

F32 = jnp.float32
BF16 = jnp.bfloat16

EPS = 1e-6
NEG_INF = -1e30
LOG2E = math.log2(math.e)
ROPE_BASE = 10000.0
GRID_W = 64
N_MIXERS = 4
FFN_HIDDEN = 2816
RET_HEADS = 4
RET_QK_DIM = 256
RET_V_DIM = 512
NAT_HEADS = 16
NAT_HEAD_DIM = 64
NAT_KH = 8
NAT_KW = 16
POOL_WINDOWS = (2, 4, 8, 16)
SWA_Q_HEADS = 16
SWA_KV_HEADS = 4
SWA_HEAD_DIM = 64
SWA_WINDOW = 128
SWA_BLOCK = 128

LANES = 128
SUBLANES = 8
VMEM_LIMIT = 56 * 1024 * 1024
TOKEN_TILE = 256
PROJ_TILE = 512
FFN_TILE = 1024
FFN_SUBTILE = 256
CTX_FFN_CHUNK = 256
RET_CHUNK = 256
XPOSE_CHUNK = 512


def _params(*sem):
    return pltpu.CompilerParams(dimension_semantics=sem, vmem_limit_bytes=VMEM_LIMIT)


def _resident(shape):
    nd = len(shape)
    return pl.BlockSpec(shape, lambda *_: (0,) * nd, pipeline_mode=pl.Buffered(1))


def _silu(x):
    return x * jax.nn.sigmoid(x)


def _ada(x, g, shift, scale):
    var = jnp.mean(x * x, axis=-1, keepdims=True)
    y = x * lax.rsqrt(var + EPS) * g
    return y * (1.0 + scale) + shift


def _mod_rows(mod_ref, j):
    return (mod_ref[0, 3 * j:3 * j + 1, :], mod_ref[0, 3 * j + 1:3 * j + 2, :],
            mod_ref[0, 3 * j + 2:3 * j + 3, :])


def _mod_spec(mod, d):
    if mod.shape[0] == 1:
        return pl.BlockSpec((1, 9, d), lambda b, t: (0, 0, 0))
    return pl.BlockSpec((1, 9, d), lambda b, t: (b, 0, 0))


def _dot(a, b):
    return jnp.dot(a, b, preferred_element_type=F32)


def _dot_nt(a, b):
    return lax.dot_general(a, b, (((1,), (1,)), ((), ())), preferred_element_type=F32)


def _dot_tn(a, b):
    return lax.dot_general(a, b, (((0,), (0,)), ((), ())), preferred_element_type=F32)


def _mod_kernel(c_ref, w_ref, b_ref, o_ref):
    s = _silu(c_ref[...]).astype(BF16)
    o_ref[0] = _dot(s, w_ref[0].astype(BF16)) + b_ref[0]


def _modulation(cc, w_mod, b_mod):
    depth, d, n = w_mod.shape
    tn = 2304
    return pl.pallas_call(
        _mod_kernel,
        grid=(depth, n // tn),
        in_specs=[pl.BlockSpec((SUBLANES, d), lambda l, j: (0, 0)),
                  pl.BlockSpec((1, d, tn), lambda l, j: (l, 0, j)),
                  pl.BlockSpec((1, 1, tn), lambda l, j: (l, 0, j))],
        out_specs=pl.BlockSpec((1, SUBLANES, tn), lambda l, j: (l, 0, j)),
        out_shape=jax.ShapeDtypeStruct((depth, SUBLANES, n), F32),
        compiler_params=_params("arbitrary", "arbitrary"),
        name="modulation",
    )(cc, w_mod, b_mod.reshape(depth, 1, n))


def _ffn_kernel(h_ref, mod_ref, g_ref, win_ref, wout_ref, *rest, j, mixer_out, final):
    o_ref = rest[-1]
    shift, scale, gate = _mod_rows(mod_ref, j)
    tm = h_ref.shape[1]
    sub = min(tm, FFN_SUBTILE)
    for r0 in range(0, tm, sub):
        x = h_ref[0, r0:r0 + sub, :]
        if mixer_out:
            y_ref, wo_ref = rest[:2]
            x = x + mod_ref[0, 5:6, :] * _dot(y_ref[0, r0:r0 + sub, :], wo_ref[...])
        xb = _ada(x, g_ref[...], shift, scale).astype(BF16)
        hid = (_silu(_dot(xb, win_ref[0])) * _dot(xb, win_ref[1])).astype(BF16)
        out = x + (0.5 * gate) * _dot(hid, wout_ref[...])
        if final:
            var = jnp.mean(out * out, axis=-1, keepdims=True)
            out = out * lax.rsqrt(var + EPS) * rest[-2][...]
        o_ref[0, r0:r0 + sub, :] = out


def _ffn(h, mod, g, w_in, w_out, half, mixer_out=None, final_g=None):
    bsz, t, d = h.shape
    tm = min(FFN_TILE, t)
    row = lambda n: pl.BlockSpec((1, tm, n), lambda b, i: (b, i, 0))
    vec = pl.BlockSpec((1, d), lambda b, i: (0, 0))
    in_specs = [row(d), _mod_spec(mod, d), vec, _resident(w_in.shape), _resident(w_out.shape)]
    args = [h, mod, g.reshape(1, d), w_in, w_out]
    if mixer_out is not None:
        y, wo = mixer_out
        in_specs += [row(y.shape[-1]), _resident(wo.shape)]
        args += [y, wo]
    if final_g is not None:
        in_specs.append(vec)
        args.append(final_g.reshape(1, d))
    return pl.pallas_call(
        functools.partial(_ffn_kernel, j=2 * half, mixer_out=mixer_out is not None, final=final_g is not None),
        grid=(bsz, t // tm),
        in_specs=in_specs,
        out_specs=row(d),
        out_shape=jax.ShapeDtypeStruct((bsz, t, d), F32),
        compiler_params=_params("parallel", "parallel"),
        name="ffn",
    )(*args)


def _ffn_ctx_kernel(h_ref, mod_ref, g_ref, wa_ref, wb_ref, wo_ref, *rest, j, mixer_out):
    o_ref, win_bf_ref, wout_bf_ref, x_scr, xb_scr, acc_scr = rest[-6:]
    s = pl.program_id(0)
    shift, scale, gate = _mod_rows(mod_ref, j)

    @pl.when(s == 0)
    def _():
        x = h_ref[...]
        if mixer_out:
            y_ref, wmix_ref = rest[:2]
            x = x + mod_ref[0, 5:6, :] * _dot(y_ref[...], wmix_ref[...])
        x_scr[...] = x
        xb_scr[...] = _ada(x, g_ref[...], shift, scale).astype(BF16)
        acc_scr[...] = jnp.zeros_like(acc_scr)

    wa, wb, wo = wa_ref[0, 0].astype(BF16), wb_ref[0, 0].astype(BF16), wo_ref[0, 0].astype(BF16)
    win_bf_ref[0] = wa
    win_bf_ref[1] = wb
    wout_bf_ref[...] = wo
    xb = xb_scr[...]
    hid = (_silu(_dot(xb, wa)) * _dot(xb, wb)).astype(BF16)
    acc_scr[...] += _dot(hid, wo)

    @pl.when(s == pl.num_programs(0) - 1)
    def _():
        o_ref[...] = x_scr[...] + (0.5 * gate) * acc_scr[...]


def _ffn_ctx(hc, mod, g, w_in, w_out, layer, half, mixer_out=None):
    bsz, l, d = hc.shape
    f = w_out.shape[2]
    fc = CTX_FFN_CHUNK
    nf = f // fc
    n = bsz * l
    const = lambda shape: pl.BlockSpec(shape, lambda s: (0,) * len(shape))
    in_specs = [const((n, d)), const((1, 9, d)), const((1, d)),
                pl.BlockSpec((1, 1, d, fc), lambda s: (layer, half, 0, s)),
                pl.BlockSpec((1, 1, d, fc), lambda s: (layer, half, 0, nf + s)),
                pl.BlockSpec((1, 1, fc, d), lambda s: (layer, half, s, 0))]
    args = [hc.reshape(n, d), mod, g.reshape(1, d), w_in, w_in, w_out]
    if mixer_out is not None:
        y, wmix = mixer_out
        in_specs += [const((n, y.shape[-1])), _resident(wmix.shape)]
        args += [y.reshape(n, y.shape[-1]), wmix]
    out, w_in_bf, w_out_bf = pl.pallas_call(
        functools.partial(_ffn_ctx_kernel, j=2 * half, mixer_out=mixer_out is not None),
        grid=(nf,),
        in_specs=in_specs,
        out_specs=[const((n, d)),
                   pl.BlockSpec((2, d, fc), lambda s: (0, 0, s)),
                   pl.BlockSpec((fc, d), lambda s: (s, 0))],
        out_shape=[jax.ShapeDtypeStruct((n, d), F32),
                   jax.ShapeDtypeStruct((2, d, f), BF16),
                   jax.ShapeDtypeStruct((f, d), BF16)],
        scratch_shapes=[pltpu.VMEM((n, d), F32), pltpu.VMEM((n, d), BF16), pltpu.VMEM((n, d), F32)],
        compiler_params=_params("arbitrary"),
        name="ffn_ctx",
    )(*args)
    return out.reshape(bsz, l, d), w_in_bf, w_out_bf


def _ffn_weights_bf16(w_in, w_out, layer, half):
    d, f2 = w_in.shape[2:]
    return (w_in[layer, half].reshape(d, 2, f2 // 2).transpose(1, 0, 2).astype(BF16),
            w_out[layer, half].astype(BF16))


def _ret_proj_kernel(h_ref, mod_ref, g_ref, w_ref, *rest, rope):
    if rope:
        cos_ref, sin_ref, q_ref, k_ref, v_ref, gate_ref = rest
    else:
        q_ref, k_ref, v_ref, gate_ref = rest
    shift, scale, _ = _mod_rows(mod_ref, 1)
    xb = _ada(h_ref[0], g_ref[...], shift, scale).astype(BF16)
    dk, nh = RET_QK_DIM, RET_HEADS
    half = dk // 2

    def rot(x):
        if not rope:
            return x.astype(BF16)
        c, s = cos_ref[...], sin_ref[...]
        x1, x2 = x[:, :half], x[:, half:]
        return jnp.concatenate([x1 * c - x2 * s, x2 * c + x1 * s], axis=-1).astype(BF16)

    for hd in range(nh):
        q_ref[0, :, hd * dk:(hd + 1) * dk] = rot(_dot(xb, w_ref[:, hd * dk:(hd + 1) * dk]))
        kcol = nh * dk + hd * dk
        k_ref[0, :, hd * dk:(hd + 1) * dk] = rot(_dot(xb, w_ref[:, kcol:kcol + dk]) * dk ** -0.5)
    v0 = 2 * nh * dk
    nv = nh * RET_V_DIM
    v_ref[0] = _dot(xb, w_ref[:, v0:v0 + nv]).astype(BF16)
    gate_ref[0] = _silu(_dot(xb, w_ref[:, v0 + nv:v0 + 2 * nv])).astype(BF16)


def _ret_proj(h, mod, g, w, cos=None, sin=None):
    bsz, t, d = h.shape
    tm = min(PROJ_TILE, t)
    rope = cos is not None
    nqk = RET_HEADS * RET_QK_DIM
    nv = RET_HEADS * RET_V_DIM
    row = lambda n: pl.BlockSpec((1, tm, n), lambda b, i: (b, i, 0))
    in_specs = [row(d), _mod_spec(mod, d), pl.BlockSpec((1, d), lambda b, i: (0, 0)), _resident(w.shape)]
    args = [h, mod, g.reshape(1, d), w]
    if rope:
        in_specs += [pl.BlockSpec((tm, RET_QK_DIM // 2), lambda b, i: (i, 0))] * 2
        args += [cos, sin]
    return pl.pallas_call(
        functools.partial(_ret_proj_kernel, rope=rope),
        grid=(bsz, t // tm),
        in_specs=in_specs,
        out_specs=[row(nqk), row(nqk), row(nv), row(nv)],
        out_shape=[jax.ShapeDtypeStruct((bsz, t, nqk), BF16), jax.ShapeDtypeStruct((bsz, t, nqk), BF16),
                   jax.ShapeDtypeStruct((bsz, t, nv), BF16), jax.ShapeDtypeStruct((bsz, t, nv), BF16)],
        compiler_params=_params("parallel", "parallel"),
        name="ret_proj",
    )(*args)


def _log_sigmoid(x):
    return jnp.minimum(x, 0.0) - jnp.log(1.0 + jnp.exp(-jnp.abs(x)))


def _ret_mix_kernel(decf_ref, decb_ref, q_ref, k_ref, v_ref, g_ref, gn_ref, s0f_ref, s0b_ref,
                    y_ref, sf_ref, sb_ref, s_scr, sb_scr, *, chunk):
    c = chunk
    nc = q_ref.shape[1] // c
    lgf = _log_sigmoid(decf_ref[0])[:, :1]
    lgb = _log_sigmoid(decb_ref[0])[:, :1]
    dist = lax.broadcasted_iota(jnp.int32, (c, c), 0) - lax.broadcasted_iota(jnp.int32, (c, c), 1)
    decay = jnp.where(dist >= 0,
                      jnp.exp(jnp.maximum(dist, 0).astype(F32) * lgf),
                      jnp.exp(jnp.maximum(-dist, 0).astype(F32) * lgb))
    pos = lax.broadcasted_iota(jnp.int32, (c, 1), 0).astype(F32)
    qd_f, kd_f, cd_f = jnp.exp((pos + 1.0) * lgf), jnp.exp((c - 1.0 - pos) * lgf), jnp.exp(c * lgf)
    qd_b, kd_b, cd_b = jnp.exp((c - pos) * lgb), jnp.exp(pos * lgb), jnp.exp(c * lgb)

    def rows(n):
        return pl.ds(pl.multiple_of(n * c, c), c)

    s_scr[...] = s0b_ref[0, 0]

    def bwd(i, carry):
        n = nc - 1 - i
        s = s_scr[...]
        sb_scr[n] = s.astype(BF16)
        kd = (k_ref[0, rows(n), :].astype(F32) * kd_b).astype(BF16)
        s_scr[...] = s * cd_b + _dot_tn(kd, v_ref[0, rows(n), :])
        return carry

    lax.fori_loop(0, nc, bwd, 0, unroll=min(4, nc))
    sb_ref[0, 0] = s_scr[...]
    s_scr[...] = s0f_ref[0, 0]

    def fwd(n, carry):
        q, k, v = q_ref[0, rows(n), :], k_ref[0, rows(n), :], v_ref[0, rows(n), :]
        qf, kf = q.astype(F32), k.astype(F32)
        s = s_scr[...]
        att = (_dot(q, k.T) * decay).astype(BF16)
        o = (_dot(att, v) + _dot((qf * qd_f).astype(BF16), s.astype(BF16))
             + _dot((qf * qd_b).astype(BF16), sb_scr[n]))
        mu = jnp.mean(o, axis=-1, keepdims=True)
        ctr = o - mu
        var = jnp.mean(ctr * ctr, axis=-1, keepdims=True)
        on = ctr * lax.rsqrt(var + EPS) * gn_ref[...]
        y_ref[0, rows(n), :] = (g_ref[0, rows(n), :].astype(F32) * on).astype(BF16)
        s_scr[...] = s * cd_f + _dot_tn((kf * kd_f).astype(BF16), v)
        return carry

    lax.fori_loop(0, nc, fwd, 0, unroll=min(4, nc))
    sf_ref[0, 0] = s_scr[...]


def _ret_mix(dec_f, dec_b, q, k, v, gate, gn_g, s0_f, s0_b):
    bsz, t, _ = q.shape
    nh, dk, dv = RET_HEADS, RET_QK_DIM, RET_V_DIM
    chunk = min(RET_CHUNK, t)
    seq = lambda n: pl.BlockSpec((1, t, n), lambda b, h: (b, 0, h))
    dec = pl.BlockSpec((1, 1, LANES), lambda b, h: (h, 0, 0))
    state = pl.BlockSpec((1, 1, dk, dv), lambda b, h: (b, h, 0, 0))
    return pl.pallas_call(
        functools.partial(_ret_mix_kernel, chunk=chunk),
        grid=(bsz, nh),
        in_specs=[dec, dec, seq(dk), seq(dk), seq(dv), seq(dv),
                  pl.BlockSpec((1, dv), lambda b, h: (0, h)), state, state],
        out_specs=[seq(dv), state, state],
        out_shape=[jax.ShapeDtypeStruct((bsz, t, nh * dv), BF16),
                   jax.ShapeDtypeStruct((bsz, nh, dk, dv), F32),
                   jax.ShapeDtypeStruct((bsz, nh, dk, dv), F32)],
        scratch_shapes=[pltpu.VMEM((dk, dv), F32), pltpu.VMEM((t // chunk, dk, dv), BF16)],
        compiler_params=_params("parallel", "parallel"),
        name="ret_mix",
    )(dec_f, dec_b, q, k, v, gate, gn_g.reshape(1, nh * dv), s0_f, s0_b)


def _qkv_proj_kernel(h_ref, mod_ref, g_ref, w_ref, *rest, widths, q_scale, rope):
    if rope:
        cos_ref, sin_ref = rest[:2]
        outs = rest[2:]
    else:
        outs = rest
    shift, scale, _ = _mod_rows(mod_ref, 1)
    xb = _ada(h_ref[0], g_ref[...], shift, scale).astype(BF16)
    col = 0
    for idx, (o_ref, n) in enumerate(zip(outs, widths)):
        if o_ref is not None:
            y = _dot(xb, w_ref[:, col:col + n])
            for c0 in range(0, n, LANES):
                yc = y[:, c0:c0 + LANES]
                if rope and idx < 2:
                    lane = lax.broadcasted_iota(jnp.int32, yc.shape, 1)
                    partner = jnp.where((lane & 16) == 0, pltpu.roll(yc, LANES - 16, axis=1),
                                        pltpu.roll(yc, 16, axis=1))
                    yc = yc * cos_ref[...] + partner * sin_ref[...]
                if idx == 0:
                    yc = yc * q_scale
                o_ref[0, :, c0:c0 + LANES] = yc.astype(BF16)
        col += n


def _qkv_proj(h, mod, g, w, widths, q_scale, cos=None, sin=None, want_q=True):
    bsz, t, d = h.shape
    tm = min(PROJ_TILE, t)
    rope = cos is not None
    row = lambda n: pl.BlockSpec((1, tm, n), lambda b, i: (b, i, 0))
    in_specs = [row(d), _mod_spec(mod, d), pl.BlockSpec((1, d), lambda b, i: (0, 0)), _resident(w.shape)]
    args = [h, mod, g.reshape(1, d), w]
    if rope:
        in_specs += [pl.BlockSpec((tm, LANES), lambda b, i: (i, 0))] * 2
        args += [cos, sin]
    keep = [want_q, True, True]
    out_widths = [n for n, kp in zip(widths, keep) if kp]

    def body(*refs):
        n_in = len(args)
        outs = list(refs[n_in:])
        full = [outs.pop(0) if kp else None for kp in keep]
        _qkv_proj_kernel(*refs[:n_in], *full, widths=widths, q_scale=q_scale, rope=rope)

    res = pl.pallas_call(
        body,
        grid=(bsz, t // tm),
        in_specs=in_specs,
        out_specs=[row(n) for n in out_widths],
        out_shape=[jax.ShapeDtypeStruct((bsz, t, n), BF16) for n in out_widths],
        compiler_params=_params("parallel", "parallel"),
        name="qkv_proj",
    )(*args)
    return res if want_q else [None] + list(res)


def _split_heads(x):
    lane = lax.broadcasted_iota(jnp.int32, x.shape, 1)
    zero = jnp.zeros_like(x)
    return jnp.concatenate([jnp.where(lane < 64, x, zero), jnp.where(lane >= 64, x, zero)], axis=0)


def _merge_heads(o):
    r = o.shape[0] // 2
    lane = lax.broadcasted_iota(jnp.int32, (r, o.shape[1]), 1)
    return jnp.where(lane < 64, o[:r], o[r:])


def _with_ones(v):
    return jnp.concatenate([v, jnp.ones_like(v)], axis=1)


def _stage_with_ones(dst_ref, src_ref):
    for c0 in range(0, src_ref.shape[1], XPOSE_CHUNK):
        n = min(XPOSE_CHUNK, src_ref.shape[1] - c0)
        dst_ref[c0:c0 + n, :] = _with_ones(src_ref[0, c0:c0 + n, :])


def _stage_transposed(dst_ref, src_ref, row0, width):
    for c0 in range(0, width, XPOSE_CHUNK):
        n = min(XPOSE_CHUNK, width - c0)
        dst_ref[:, c0:c0 + n] = src_ref[0, row0 + c0:row0 + c0 + n, :].T


def _nat_kernel(q_ref, k_ref, v_ref, kc_ref, vc_ref, bias_ref, o_ref, kt_scr, va_scr, snb_a, scx_a, snb_b, scx_b,
                *, rows):
    w = GRID_W
    kh = min(NAT_KH, rows)
    nk = kh * w
    t = rows * w
    _stage_transposed(kt_scr.at[0], k_ref, 0, t)
    _stage_transposed(kt_scr.at[1], k_ref, w, t - 2 * w)
    kct = kc_ref[0].T
    _stage_with_ones(va_scr, v_ref)
    vca = _with_ones(vc_ref[0])

    def window(r):
        r0 = jnp.clip(r - kh // 2, 0, rows - kh)
        return r0, pl.multiple_of(r * w, w)

    def scores(r, snb_ref, scx_ref):
        r0, qs = window(r)
        ty = r - r0
        par = r0 & 1
        kts = pl.multiple_of((r0 - par) * w, 2 * w)
        q2 = _split_heads(q_ref[0, pl.ds(qs, w), :])
        bias = jnp.concatenate(
            [jnp.concatenate([bias_ref[hd, 2 * m - ty + NAT_KH - 1] for m in range(kh // 2)], axis=1)
             for hd in range(2)], axis=0)
        snb_ref[...] = _dot(q2, kt_scr[par, :, pl.ds(kts, nk)]) + bias
        scx_ref[...] = _dot(q2, kct)

    def finish(r, snb_ref, scx_ref):
        r0, qs = window(r)
        ks = pl.multiple_of(r0 * w, w)
        s_nb, s_cx = snb_ref[...], scx_ref[...]
        m = jnp.maximum(jnp.max(s_nb, axis=-1, keepdims=True), jnp.max(s_cx, axis=-1, keepdims=True))
        p_nb = jnp.exp2(s_nb - m)
        p_cx = jnp.exp2(s_cx - m)
        oa = _dot(p_nb.astype(BF16), va_scr[pl.ds(ks, nk), :]) + _dot(p_cx.astype(BF16), vca)
        o_ref[0, pl.ds(qs, w), :] = _merge_heads(oa[:, :LANES] / oa[:, LANES:]).astype(BF16)

    scores(0, snb_a, scx_a)

    def body(i, carry):
        r = 2 * i
        scores(r + 1, snb_b, scx_b)
        finish(r, snb_a, scx_a)
        scores(jnp.minimum(r + 2, rows - 1), snb_a, scx_a)
        finish(r + 1, snb_b, scx_b)
        return carry

    lax.fori_loop(0, rows // 2, body, 0, unroll=4)


def _nat_attention(q, k, v, kc, vc, bias):
    bsz, t, d = q.shape
    l = kc.shape[1]
    rows = t // GRID_W
    hp = d // LANES
    lat = pl.BlockSpec((1, t, LANES), lambda p, b: (b, 0, p))
    ctx = pl.BlockSpec((1, l, LANES), lambda p, b: (b, 0, p))
    return pl.pallas_call(
        functools.partial(_nat_kernel, rows=rows),
        grid=(hp, bsz),
        in_specs=[lat, lat, lat, ctx, ctx,
                  pl.BlockSpec((2,) + bias.shape[1:], lambda p, b: (p, 0, 0, 0))],
        out_specs=lat,
        out_shape=jax.ShapeDtypeStruct((bsz, t, d), BF16),
        scratch_shapes=[pltpu.VMEM((2, LANES, t), BF16), pltpu.VMEM((t, 2 * LANES), BF16)]
        + [pltpu.VMEM((2 * GRID_W, n), F32) for n in (min(NAT_KH, rows) * GRID_W, l)] * 2,
        compiler_params=_params("parallel", "parallel"),
        name="nat_attention",
    )(q, k, v, kc, vc, bias)


def _nat_bias(rpb):
    w = GRID_W
    nh, nr, ncol = rpb.shape
    left = w - NAT_KW
    v = jnp.pad(rpb.astype(F32), ((0, 0), (0, 0), (left, 2 * w - ncol - left)))
    toep = jnp.tile(v, (1, 1, w))[..., :w * (2 * w - 1)].reshape(nh, nr, w, 2 * w - 1)[..., w - 1:]
    c = jnp.arange(w)[:, None]
    cc = jnp.arange(w)[None, :]
    c0 = jnp.clip(c - NAT_KW // 2, 0, w - NAT_KW)
    blocks = jnp.where((cc >= c0) & (cc < c0 + NAT_KW), toep * LOG2E, NEG_INF)
    return jnp.concatenate([blocks[:, :-1], blocks[:, 1:]], axis=-1)


def _ctx_attn_kernel(q_ref, k_ref, v_ref, o_ref):
    q2 = _split_heads(q_ref[0])
    s = _dot(q2, k_ref[0].T)
    p = jnp.exp2(s - jnp.max(s, axis=-1, keepdims=True))
    l = jnp.sum(p, axis=-1, keepdims=True)
    o_ref[0] = _merge_heads(_dot(p.astype(BF16), v_ref[0]) / l).astype(BF16)


def _ctx_attention(q, k, v):
    bsz, l, d = q.shape
    blk = pl.BlockSpec((1, l, LANES), lambda b, p: (b, 0, p))
    return pl.pallas_call(
        _ctx_attn_kernel,
        grid=(bsz, d // LANES),
        in_specs=[blk, blk, blk],
        out_specs=blk,
        out_shape=jax.ShapeDtypeStruct((bsz, l, d), BF16),
        compiler_params=_params("parallel", "parallel"),
        name="ctx_attention",
    )(q, k, v)


POOL_HALO = SUBLANES


def _split3(x):
    hi = x.astype(BF16)
    r1 = x - hi.astype(F32)
    mid = r1.astype(BF16)
    return hi, mid, (r1 - mid.astype(F32)).astype(BF16)


def _pool_kernel(h_ref, prev_ref, next_ref, mod_ref, g_ref, pw_ref, ps_ref, o_ref, band_scr, *, tm, t_total):
    i = pl.program_id(1)
    nt = pl.num_programs(1)
    hl = POOL_HALO
    x = h_ref[0]
    d = x.shape[-1]
    gd = d // len(POOL_WINDOWS)
    shift, scale, gate = _mod_rows(mod_ref, 1)

    @pl.when((pl.program_id(0) == 0) & (i == 0))
    def _():
        off = lax.broadcasted_iota(jnp.int32, (tm, tm), 1) - lax.broadcasted_iota(jnp.int32, (tm, tm), 0)
        for gi, win in enumerate(POOL_WINDOWS):
            band_scr[gi] = jnp.where((off >= -(win // 2)) & (off < win // 2), 1.0, 0.0).astype(BF16)

    xe = _ada(jnp.concatenate([prev_ref[0], x, next_ref[0]], axis=0), g_ref[...], shift, scale)
    xn = xe[hl:hl + tm]
    halo = jnp.concatenate([jnp.where(i > 0, xe[:hl], 0.0), jnp.where(i < nt - 1, xe[hl + tm:], 0.0)], axis=0)
    x3 = _split3(xn)
    h3 = _split3(halo)
    er = lax.broadcasted_iota(jnp.int32, (2 * hl, 2 * hl), 0)
    hu = lax.broadcasted_iota(jnp.int32, (2 * hl, 2 * hl), 1)
    dist = jnp.where(hu < hl, hu - hl - er, hu - er + hl)
    same_side = (er < hl) == (hu < hl)
    trow = i * tm + lax.broadcasted_iota(jnp.int32, (tm, 1), 0)
    parts = []
    for gi, win in enumerate(POOL_WINDOWS):
        half = win // 2
        cols = slice(gi * gd, (gi + 1) * gd)
        band = band_scr[gi]
        edge = jnp.where(same_side & (dist >= -half) & (dist < half), 1.0, 0.0).astype(BF16)
        tot = _dot(band, x3[0][:, cols]) + _dot(band, x3[1][:, cols]) + _dot(band, x3[2][:, cols])
        fix = _dot(edge, h3[0][:, cols]) + _dot(edge, h3[1][:, cols]) + _dot(edge, h3[2][:, cols])
        tot = jnp.concatenate([tot[:hl] + fix[:hl], tot[hl:tm - hl], tot[tm - hl:] + fix[hl:]], axis=0)
        cnt = (jnp.minimum(trow + half, t_total) - jnp.maximum(trow - half, 0)).astype(F32)
        pooled = (tot / cnt - xn[:, cols]).astype(BF16)
        parts.append(_dot(pooled, pw_ref[gi]))
    y = jnp.concatenate(parts, axis=-1) * ps_ref[...]
    o_ref[0] = x + gate * y


def _pool(h, mod, g, pw, ps):
    bsz, t, d = h.shape
    tm = TOKEN_TILE
    per = tm // POOL_HALO
    last = t // POOL_HALO - 1
    return pl.pallas_call(
        functools.partial(_pool_kernel, tm=tm, t_total=t),
        grid=(bsz, t // tm),
        in_specs=[pl.BlockSpec((1, tm, d), lambda b, i: (b, i, 0)),
                  pl.BlockSpec((1, POOL_HALO, d), lambda b, i: (b, jnp.maximum(i * per - 1, 0), 0)),
                  pl.BlockSpec((1, POOL_HALO, d), lambda b, i: (b, jnp.minimum((i + 1) * per, last), 0)),
                  _mod_spec(mod, d),
                  pl.BlockSpec((1, d), lambda b, i: (0, 0)),
                  _resident(pw.shape),
                  pl.BlockSpec((1, d), lambda b, i: (0, 0))],
        out_specs=pl.BlockSpec((1, tm, d), lambda b, i: (b, i, 0)),
        out_shape=jax.ShapeDtypeStruct((bsz, t, d), F32),
        scratch_shapes=[pltpu.VMEM((len(POOL_WINDOWS), tm, tm), BF16)],
        compiler_params=_params("arbitrary", "arbitrary"),
        name="pool",
    )(h, h, h, mod, g.reshape(1, d), pw, ps.reshape(1, d))


def _swa_kernel(sink_ref, q_ref, k_ref, v_ref, kc_ref, vc_ref, o_ref, kt_scr, va_scr, sloc_a, scx_a, sloc_b, scx_b,
                *, t_total):
    kv = pl.program_id(1)
    blk = SWA_BLOCK
    nb = t_total // blk
    grp = SWA_Q_HEADS // SWA_KV_HEADS
    nrow = grp * blk
    _stage_transposed(kt_scr, k_ref, 0, t_total)
    kct = kc_ref[0].T
    _stage_with_ones(va_scr, v_ref)
    vca = _with_ones(vc_ref[0])
    row = lax.broadcasted_iota(jnp.int32, (nrow, 1), 0)
    sink = jnp.zeros((nrow, 1), F32)
    for gi in range(grp):
        sink = jnp.where((row >= gi * blk) & (row < (gi + 1) * blk), sink_ref[kv * grp + gi] * LOG2E, sink)
    qi = lax.broadcasted_iota(jnp.int32, (nrow, blk), 0) & (blk - 1)
    kj = lax.broadcasted_iota(jnp.int32, (nrow, blk), 1)
    open_blk = jnp.zeros((nrow, blk), F32)
    prev_blk = jnp.where(kj >= qi, 0.0, NEG_INF)
    next_blk = jnp.where(kj <= qi, 0.0, NEG_INF)

    shut_blk = jnp.full((nrow, blk), NEG_INF, F32)
    span = 3 * blk
    mask_first = jnp.concatenate([open_blk, next_blk, shut_blk], axis=1)
    mask_mid = jnp.concatenate([prev_blk, open_blk, next_blk], axis=1)
    mask_last = jnp.concatenate([shut_blk, prev_blk, open_blk], axis=1)

    def offsets(n):
        qs = pl.multiple_of(n * blk, blk)
        return qs, pl.multiple_of(jnp.clip(qs - blk, 0, t_total - span), blk)

    def scores(n, mask, sloc_ref, scx_ref):
        qs, ks = offsets(n)
        qb = q_ref[0, pl.ds(qs, blk), :]
        q4 = jnp.concatenate([_split_heads(qb[:, :LANES]), _split_heads(qb[:, LANES:])], axis=0)
        sloc_ref[...] = _dot(q4, kt_scr[:, pl.ds(ks, span)]) + mask
        scx_ref[...] = _dot(q4, kct)

    def finish(n, sloc_ref, scx_ref):
        qs, ks = offsets(n)
        s_loc, s_cx = sloc_ref[...], scx_ref[...]
        m = jnp.maximum(jnp.maximum(jnp.max(s_loc, axis=-1, keepdims=True),
                                    jnp.max(s_cx, axis=-1, keepdims=True)), sink)
        p_loc = jnp.exp2(s_loc - m)
        p_cx = jnp.exp2(s_cx - m)
        oa = _dot(p_loc.astype(BF16), va_scr[pl.ds(ks, span), :]) + _dot(p_cx.astype(BF16), vca)
        o = oa[:, :LANES] / (oa[:, LANES:] + jnp.exp2(sink - m))
        out = jnp.concatenate([_merge_heads(o[:2 * blk]), _merge_heads(o[2 * blk:])], axis=-1)
        o_ref[0, pl.ds(qs, blk), :] = out.astype(BF16)

    slot_a, slot_b = (sloc_a, scx_a), (sloc_b, scx_b)
    scores(0, mask_first, *slot_a)
    scores(1, mask_mid, *slot_b)
    finish(0, *slot_a)

    def body(i, carry):
        n = 1 + 2 * i
        scores(n + 1, mask_mid, *slot_a)
        finish(n, *slot_b)
        scores(n + 2, mask_mid, *slot_b)
        finish(n + 1, *slot_a)
        return carry

    lax.fori_loop(0, (nb - 4) // 2, body, 0, unroll=2)
    scores(nb - 2, mask_mid, *slot_a)
    finish(nb - 3, *slot_b)
    scores(nb - 1, mask_last, *slot_b)
    finish(nb - 2, *slot_a)
    finish(nb - 1, *slot_b)


def _swa_attention(sink, q, k, v, kc, vc):
    bsz, t, d = q.shape
    l = kc.shape[1]
    nb = t // SWA_BLOCK
    assert nb >= 4 and nb % 2 == 0 and SWA_WINDOW == SWA_BLOCK
    qw = d // SWA_KV_HEADS
    grid_spec = pltpu.PrefetchScalarGridSpec(
        num_scalar_prefetch=1,
        grid=(bsz, SWA_KV_HEADS),
        in_specs=[pl.BlockSpec((1, t, qw), lambda b, h, s: (b, 0, h)),
                  pl.BlockSpec((1, t, LANES), lambda b, h, s: (b, 0, h)),
                  pl.BlockSpec((1, t, LANES), lambda b, h, s: (b, 0, h)),
                  pl.BlockSpec((1, l, LANES), lambda b, h, s: (b, 0, h)),
                  pl.BlockSpec((1, l, LANES), lambda b, h, s: (b, 0, h))],
        out_specs=pl.BlockSpec((1, t, qw), lambda b, h, s: (b, 0, h)),
        scratch_shapes=[pltpu.VMEM((LANES, t), BF16), pltpu.VMEM((t, 2 * LANES), BF16)]
        + [pltpu.VMEM((SWA_Q_HEADS // SWA_KV_HEADS * SWA_BLOCK, n), F32) for n in (3 * SWA_BLOCK, l)] * 2,
    )
    return pl.pallas_call(
        functools.partial(_swa_kernel, t_total=t),
        grid_spec=grid_spec,
        out_shape=jax.ShapeDtypeStruct((bsz, t, d), BF16),
        compiler_params=_params("parallel", "parallel"),
        name="swa_attention",
    )(sink, q, k, v, kc, vc)


def _rope_angles(positions, dim):
    seg = dim // len(positions)
    inv = ROPE_BASE ** (-jnp.arange(0, seg, 2, dtype=F32) / seg)
    return jnp.concatenate([jnp.tile(p.astype(F32)[:, None] * inv, (1, 2)) for p in positions], axis=-1)


def _dup_heads(w, heads, dh):
    d = w.shape[0]
    return jnp.broadcast_to(w.reshape(d, heads, 1, dh), (d, heads, 2, dh)).reshape(d, heads * 2 * dh)


def kernel(x, c, ctx, c_ctx, w_mod, b_mod, norm_g, ffn_w_in, ffn_w_out, ret_w_in, ret_w_out, ret_gn_g, ret_decay_f, ret_decay_b, nat_w_qkv, nat_w_o, nat_rpb, pool_w, pool_scale, swa_w_qkv, swa_w_o, swa_sink, final_norm_g):
    bsz, t, d = x.shape
    depth = w_mod.shape[0]
    cc = jnp.concatenate([c, c_ctx[None], jnp.zeros((SUBLANES - bsz - 1, d), F32)], axis=0)
    mods = _modulation(cc, w_mod, b_mod)
    h, hc = x, ctx
    for i in range(depth):
        kind, occ = i % N_MIXERS, i // N_MIXERS
        last = i == depth - 1
        ctx_live = (not last) or kind != 2
        ml = mods[i, :bsz].reshape(bsz, 9, d)
        mc = mods[i, bsz:bsz + 1].reshape(1, 9, d)
        if ctx_live:
            hc, w_in, w_out = _ffn_ctx(hc, mc, norm_g[i, 0], ffn_w_in, ffn_w_out, i, 0)
        else:
            w_in, w_out = _ffn_weights_bf16(ffn_w_in, ffn_w_out, i, 0)
        h = _ffn(h, ml, norm_g[i, 0], w_in, w_out, 0)
        g1 = norm_g[i, 1]
        yl = yc = None
        if kind == 0:
            assert not last, "retention as the last layer is not wired up"
            w = ret_w_in[occ].astype(BF16)
            wo = ret_w_out[occ].astype(BF16)
            ang = _rope_angles([jnp.arange(t)], RET_QK_DIM)[:, :RET_QK_DIM // 2]
            dec_f = jnp.broadcast_to(ret_decay_f[occ].astype(F32)[:, None, None], (RET_HEADS, 1, LANES))
            dec_b = jnp.broadcast_to(ret_decay_b[occ].astype(F32)[:, None, None], (RET_HEADS, 1, LANES))
            qc, kc, vc, gc = _ret_proj(hc, mc, g1, w)
            zeros = jnp.zeros((bsz, RET_HEADS, RET_QK_DIM, RET_V_DIM), F32)
            oc, s_f, s_b = _ret_mix(dec_f, dec_b, qc, kc, vc, gc, ret_gn_g[occ], zeros, zeros)
            ql, kl, vl, gl = _ret_proj(h, ml, g1, w, jnp.cos(ang), jnp.sin(ang))
            ol, _, _ = _ret_mix(dec_f, dec_b, ql, kl, vl, gl, ret_gn_g[occ], s_f, s_b)
            yl, yc = (ol, wo), (oc, wo)
        elif kind == 1:
            w = nat_w_qkv[occ].astype(BF16)
            wo = nat_w_o[occ].astype(BF16)
            widths = (d, d, d)
            qc, kc, vc = _qkv_proj(hc, mc, g1, w, widths, NAT_HEAD_DIM ** -0.5 * LOG2E, want_q=not last)
            ql, kl, vl = _qkv_proj(h, ml, g1, w, widths, NAT_HEAD_DIM ** -0.5 * LOG2E)
            yl = (_nat_attention(ql, kl, vl, kc, vc, _nat_bias(nat_rpb[occ])), wo)
            if not last:
                yc = (_ctx_attention(qc, kc, vc), wo)
        elif kind == 2:
            pw = pool_w[occ].astype(BF16)
            h_new = _pool(h, ml, g1, pw, pool_scale[occ])
            if not last:
                hc = _pool(hc, mc, g1, pw, pool_scale[occ])
            h = h_new
        else:
            assert last, "windowed attention with live context outputs is not wired up"
            nq = SWA_Q_HEADS * SWA_HEAD_DIM
            nkv = SWA_KV_HEADS * SWA_HEAD_DIM
            wq = swa_w_qkv[occ][:, :nq]
            wk = _dup_heads(swa_w_qkv[occ][:, nq:nq + nkv], SWA_KV_HEADS, SWA_HEAD_DIM)
            wv = _dup_heads(swa_w_qkv[occ][:, nq + nkv:], SWA_KV_HEADS, SWA_HEAD_DIM)
            w = jnp.concatenate([wq, wk, wv], axis=-1).astype(BF16)
            wo = swa_w_o[occ].astype(BF16)
            widths = (nq, 2 * nkv, 2 * nkv)
            tt = jnp.arange(t)
            ang = _rope_angles([tt // GRID_W, tt % GRID_W], SWA_HEAD_DIM)
            ang = jnp.tile(ang, (1, LANES // SWA_HEAD_DIM))
            lane = jnp.arange(LANES)
            sin = jnp.where(lane % 32 < 16, -jnp.sin(ang), jnp.sin(ang))
            _, kc, vc = _qkv_proj(hc, mc, g1, w, widths, SWA_HEAD_DIM ** -0.5 * LOG2E, want_q=False)
            ql, kl, vl = _qkv_proj(h, ml, g1, w, widths, SWA_HEAD_DIM ** -0.5 * LOG2E, jnp.cos(ang), sin)
            yl = (_swa_attention(swa_sink[occ].astype(F32), ql, kl, vl, kc, vc), wo)
        if not last:
            hc, w_in, w_out = _ffn_ctx(hc, mc, norm_g[i, 2], ffn_w_in, ffn_w_out, i, 1, yc)
        else:
            w_in, w_out = _ffn_weights_bf16(ffn_w_in, ffn_w_out, i, 1)
        h = _ffn(h, ml, norm_g[i, 2], w_in, w_out, 1, yl, final_norm_g if last else None)
    return h
```

```python
import functools
import math

import jax
import jax.numpy as jnp
import numpy as np
from jax import lax
from jax.experimental import pallas as pl
from jax.experimental.pallas import tpu as pltpu

F32 = jnp.float32
BF16 = jnp.bfloat16

EPS = 1e-6
NEG_INF = -1e30
LOG2E = math.log2(math.e)
ROPE_BASE = 10000.0
GRID_W = 64
N_MIXERS = 4
FFN_HIDDEN = 2816
RET_HEADS = 4
RET_QK_DIM = 256
RET_V_DIM = 512
NAT_HEADS = 16
NAT_HEAD_DIM = 64
NAT_KH = 8
NAT_KW = 16
POOL_WINDOWS = (2, 4, 8, 16)
SWA_Q_HEADS = 16
SWA_KV_HEADS = 4
SWA_HEAD_DIM = 64
SWA_WINDOW = 128
SWA_BLOCK = 128

LANES = 128
SUBLANES = 8
VMEM_LIMIT = 56 * 1024 * 1024
TOKEN_TILE = 256
PROJ_TILE = 512
FFN_TILE = 1024
FFN_SUBTILE = 256
CTX_FFN_CHUNK = 256
RET_CHUNK = 256
XPOSE_CHUNK = 512


def _params(*sem):
    return pltpu.CompilerParams(dimension_semantics=sem, vmem_limit_bytes=VMEM_LIMIT)


def _resident(shape):
    nd = len(shape)
    return pl.BlockSpec(shape, lambda *_: (0,) * nd, pipeline_mode=pl.Buffered(1))


def _silu(x):
    return x * jax.nn.sigmoid(x)


def _ada(x, g, shift, scale):
    var = jnp.mean(x * x, axis=-1, keepdims=True)
    y = x * lax.rsqrt(var + EPS) * g
    return y * (1.0 + scale) + shift


def _mod_rows(mod_ref, j):
    return (mod_ref[0, 3 * j:3 * j + 1, :], mod_ref[0, 3 * j + 1:3 * j + 2, :],
            mod_ref[0, 3 * j + 2:3 * j + 3, :])


def _mod_spec(mod, d):
    if mod.shape[0] == 1:
        return pl.BlockSpec((1, 9, d), lambda b, t: (0, 0, 0))
    return pl.BlockSpec((1, 9, d), lambda b, t: (b, 0, 0))


def _dot(a, b):
    return jnp.dot(a, b, preferred_element_type=F32)


def _dot_nt(a, b):
    return lax.dot_general(a, b, (((1,), (1,)), ((), ())), preferred_element_type=F32)


def _dot_tn(a, b):
    return lax.dot_general(a, b, (((0,), (0,)), ((), ())), preferred_element_type=F32)


def _mod_kernel(c_ref, w_ref, b_ref, o_ref):
    s = _silu(c_ref[...]).astype(BF16)
    o_ref[0] = _dot(s, w_ref[0].astype(BF16)) + b_ref[0]


def _modulation(cc, w_mod, b_mod):
    depth, d, n = w_mod.shape
    tn = 2304
    return pl.pallas_call(
        _mod_kernel,
        grid=(depth, n // tn),
        in_specs=[pl.BlockSpec((SUBLANES, d), lambda l, j: (0, 0)),
                  pl.BlockSpec((1, d, tn), lambda l, j: (l, 0, j)),
                  pl.BlockSpec((1, 1, tn), lambda l, j: (l, 0, j))],
        out_specs=pl.BlockSpec((1, SUBLANES, tn), lambda l, j: (l, 0, j)),
        out_shape=jax.ShapeDtypeStruct((depth, SUBLANES, n), F32),
        compiler_params=_params("arbitrary", "arbitrary"),
        name="modulation",
    )(cc, w_mod, b_mod.reshape(depth, 1, n))


def _ffn_kernel(h_ref, mod_ref, g_ref, win_ref, wout_ref, *rest, j, mixer_out, final):
    o_ref = rest[-1]
    shift, scale, gate = _mod_rows(mod_ref, j)
    tm = h_ref.shape[1]
    sub = min(tm, FFN_SUBTILE)
    for r0 in range(0, tm, sub):
        x = h_ref[0, r0:r0 + sub, :]
        if mixer_out:
            y_ref, wo_ref = rest[:2]
            x = x + mod_ref[0, 5:6, :] * _dot(y_ref[0, r0:r0 + sub, :], wo_ref[...])
        xb = _ada(x, g_ref[...], shift, scale).astype(BF16)
        hid = (_silu(_dot(xb, win_ref[0])) * _dot(xb, win_ref[1])).astype(BF16)
        out = x + (0.5 * gate) * _dot(hid, wout_ref[...])
        if final:
            var = jnp.mean(out * out, axis=-1, keepdims=True)
            out = out * lax.rsqrt(var + EPS) * rest[-2][...]
        o_ref[0, r0:r0 + sub, :] = out


def _ffn(h, mod, g, w_in, w_out, half, mixer_out=None, final_g=None):
    bsz, t, d = h.shape
    tm = min(FFN_TILE, t)
    row = lambda n: pl.BlockSpec((1, tm, n), lambda b, i: (b, i, 0))
    vec = pl.BlockSpec((1, d), lambda b, i: (0, 0))
    in_specs = [row(d), _mod_spec(mod, d), vec, _resident(w_in.shape), _resident(w_out.shape)]
    args = [h, mod, g.reshape(1, d), w_in, w_out]
    if mixer_out is not None:
        y, wo = mixer_out
        in_specs += [row(y.shape[-1]), _resident(wo.shape)]
        args += [y, wo]
    if final_g is not None:
        in_specs.append(vec)
        args.append(final_g.reshape(1, d))
    return pl.pallas_call(
        functools.partial(_ffn_kernel, j=2 * half, mixer_out=mixer_out is not None, final=final_g is not None),
        grid=(bsz, t // tm),
        in_specs=in_specs,
        out_specs=row(d),
        out_shape=jax.ShapeDtypeStruct((bsz, t, d), F32),
        compiler_params=_params("parallel", "parallel"),
        name="ffn",
    )(*args)


def _ffn_ctx_kernel(h_ref, mod_ref, g_ref, wa_ref, wb_ref, wo_ref, *rest, j, mixer_out):
    o_ref, win_bf_ref, wout_bf_ref, x_scr, xb_scr, acc_scr = rest[-6:]
    s = pl.program_id(0)
    shift, scale, gate = _mod_rows(mod_ref, j)

    @pl.when(s == 0)
    def _():
        x = h_ref[...]
        if mixer_out:
            y_ref, wmix_ref = rest[:2]
            x = x + mod_ref[0, 5:6, :] * _dot(y_ref[...], wmix_ref[...])
        x_scr[...] = x
        xb_scr[...] = _ada(x, g_ref[...], shift, scale).astype(BF16)
        acc_scr[...] = jnp.zeros_like(acc_scr)

    wa, wb, wo = wa_ref[0, 0].astype(BF16), wb_ref[0, 0].astype(BF16), wo_ref[0, 0].astype(BF16)
    win_bf_ref[0] = wa
    win_bf_ref[1] = wb
    wout_bf_ref[...] = wo
    xb = xb_scr[...]
    hid = (_silu(_dot(xb, wa)) * _dot(xb, wb)).astype(BF16)
    acc_scr[...] += _dot(hid, wo)

    @pl.when(s == pl.num_programs(0) - 1)
    def _():
        o_ref[...] = x_scr[...] + (0.5 * gate) * acc_scr[...]


def _ffn_ctx(hc, mod, g, w_in, w_out, layer, half, mixer_out=None):
    bsz, l, d = hc.shape
    f = w_out.shape[2]
    fc = CTX_FFN_CHUNK
    nf = f // fc
    n = bsz * l
    const = lambda shape: pl.BlockSpec(shape, lambda s: (0,) * len(shape))
    in_specs = [const((n, d)), const((1, 9, d)), const((1, d)),
                pl.BlockSpec((1, 1, d, fc), lambda s: (layer, half, 0, s)),
                pl.BlockSpec((1, 1, d, fc), lambda s: (layer, half, 0, nf + s)),
                pl.BlockSpec((1, 1, fc, d), lambda s: (layer, half, s, 0))]
    args = [hc.reshape(n, d), mod, g.reshape(1, d), w_in, w_in, w_out]
    if mixer_out is not None:
        y, wmix = mixer_out
        in_specs += [const((n, y.shape[-1])), _resident(wmix.shape)]
        args += [y.reshape(n, y.shape[-1]), wmix]
    out, w_in_bf, w_out_bf = pl.pallas_call(
        functools.partial(_ffn_ctx_kernel, j=2 * half, mixer_out=mixer_out is not None),
        grid=(nf,),
        in_specs=in_specs,
        out_specs=[const((n, d)),
                   pl.BlockSpec((2, d, fc), lambda s: (0, 0, s)),
                   pl.BlockSpec((fc, d), lambda s: (s, 0))],
        out_shape=[jax.ShapeDtypeStruct((n, d), F32),
                   jax.ShapeDtypeStruct((2, d, f), BF16),
                   jax.ShapeDtypeStruct((f, d), BF16)],
        scratch_shapes=[pltpu.VMEM((n, d), F32), pltpu.VMEM((n, d), BF16), pltpu.VMEM((n, d), F32)],
        compiler_params=_params("arbitrary"),
        name="ffn_ctx",
    )(*args)
    return out.reshape(bsz, l, d), w_in_bf, w_out_bf


def _ffn_weights_bf16(w_in, w_out, layer, half):
    d, f2 = w_in.shape[2:]
    return (w_in[layer, half].reshape(d, 2, f2 // 2).transpose(1, 0, 2).astype(BF16),
            w_out[layer, half].astype(BF16))


def _ret_proj_kernel(h_ref, mod_ref, g_ref, w_ref, *rest, rope):
    if rope:
        cos_ref, sin_ref, q_ref, k_ref, v_ref, gate_ref = rest
    else:
        q_ref, k_ref, v_ref, gate_ref = rest
    shift, scale, _ = _mod_rows(mod_ref, 1)
    dk, nh = RET_QK_DIM, RET_HEADS
    half = dk // 2
    v0 = 2 * nh * dk
    nv = nh * RET_V_DIM
    tm = h_ref.shape[1]
    sub = min(tm, FFN_SUBTILE)
    for r0 in range(0, tm, sub):
        rows = slice(r0, r0 + sub)
        xb = _ada(h_ref[0, rows, :], g_ref[...], shift, scale).astype(BF16)

        def rot(x):
            if not rope:
                return x.astype(BF16)
            c, s = cos_ref[rows, :], sin_ref[rows, :]
            x1, x2 = x[:, :half], x[:, half:]
            return jnp.concatenate([x1 * c - x2 * s, x2 * c + x1 * s], axis=-1).astype(BF16)

        for hd in range(nh):
            q_ref[0, rows, hd * dk:(hd + 1) * dk] = rot(_dot(xb, w_ref[:, hd * dk:(hd + 1) * dk]))
            kcol = nh * dk + hd * dk
            k_ref[0, rows, hd * dk:(hd + 1) * dk] = rot(_dot(xb, w_ref[:, kcol:kcol + dk]) * dk ** -0.5)
        v_ref[0, rows, :] = _dot(xb, w_ref[:, v0:v0 + nv]).astype(BF16)
        gate_ref[0, rows, :] = _silu(_dot(xb, w_ref[:, v0 + nv:v0 + 2 * nv])).astype(BF16)


def _ret_proj(h, mod, g, w, cos=None, sin=None):
    bsz, t, d = h.shape
    tm = min(PROJ_TILE, t)
    rope = cos is not None
    nqk = RET_HEADS * RET_QK_DIM
    nv = RET_HEADS * RET_V_DIM
    row = lambda n: pl.BlockSpec((1, tm, n), lambda b, i: (b, i, 0))
    in_specs = [row(d), _mod_spec(mod, d), pl.BlockSpec((1, d), lambda b, i: (0, 0)), _resident(w.shape)]
    args = [h, mod, g.reshape(1, d), w]
    if rope:
        in_specs += [pl.BlockSpec((tm, RET_QK_DIM // 2), lambda b, i: (i, 0))] * 2
        args += [cos, sin]
    return pl.pallas_call(
        functools.partial(_ret_proj_kernel, rope=rope),
        grid=(bsz, t // tm),
        in_specs=in_specs,
        out_specs=[row(nqk), row(nqk), row(nv), row(nv)],
        out_shape=[jax.ShapeDtypeStruct((bsz, t, nqk), BF16), jax.ShapeDtypeStruct((bsz, t, nqk), BF16),
                   jax.ShapeDtypeStruct((bsz, t, nv), BF16), jax.ShapeDtypeStruct((bsz, t, nv), BF16)],
        compiler_params=_params("parallel", "parallel"),
        name="ret_proj",
    )(*args)


def _log_sigmoid(x):
    return jnp.minimum(x, 0.0) - jnp.log(1.0 + jnp.exp(-jnp.abs(x)))


def _ret_mix_kernel(decf_ref, decb_ref, q_ref, k_ref, v_ref, g_ref, gn_ref, s0f_ref, s0b_ref,
                    y_ref, sf_ref, sb_ref, s_scr, sb_scr, *, chunk):
    c = chunk
    nc = q_ref.shape[1] // c
    lgf = _log_sigmoid(decf_ref[0])[:, :1]
    lgb = _log_sigmoid(decb_ref[0])[:, :1]
    dist = lax.broadcasted_iota(jnp.int32, (c, c), 0) - lax.broadcasted_iota(jnp.int32, (c, c), 1)
    decay = jnp.where(dist >= 0,
                      jnp.exp(jnp.maximum(dist, 0).astype(F32) * lgf),
                      jnp.exp(jnp.maximum(-dist, 0).astype(F32) * lgb))
    pos = lax.broadcasted_iota(jnp.int32, (c, 1), 0).astype(F32)
    qd_f, kd_f, cd_f = jnp.exp((pos + 1.0) * lgf), jnp.exp((c - 1.0 - pos) * lgf), jnp.exp(c * lgf)
    qd_b, kd_b, cd_b = jnp.exp((c - pos) * lgb), jnp.exp(pos * lgb), jnp.exp(c * lgb)

    def rows(n):
        return pl.ds(pl.multiple_of(n * c, c), c)

    s_scr[...] = s0b_ref[0, 0]

    def bwd(i, carry):
        n = nc - 1 - i
        s = s_scr[...]
        sb_scr[n] = s.astype(BF16)
        kd = (k_ref[0, rows(n), :].astype(F32) * kd_b).astype(BF16)
        s_scr[...] = s * cd_b + _dot_tn(kd, v_ref[0, rows(n), :])
        return carry

    lax.fori_loop(0, nc, bwd, 0, unroll=min(4, nc))
    sb_ref[0, 0] = s_scr[...]
    s_scr[...] = s0f_ref[0, 0]

    def fwd(n, carry):
        q, k, v = q_ref[0, rows(n), :], k_ref[0, rows(n), :], v_ref[0, rows(n), :]
        qf, kf = q.astype(F32), k.astype(F32)
        s = s_scr[...]
        att = (_dot(q, k.T) * decay).astype(BF16)
        o = (_dot(att, v) + _dot((qf * qd_f).astype(BF16), s.astype(BF16))
             + _dot((qf * qd_b).astype(BF16), sb_scr[n]))
        mu = jnp.mean(o, axis=-1, keepdims=True)
        ctr = o - mu
        var = jnp.mean(ctr * ctr, axis=-1, keepdims=True)
        on = ctr * lax.rsqrt(var + EPS) * gn_ref[...]
        y_ref[0, rows(n), :] = (g_ref[0, rows(n), :].astype(F32) * on).astype(BF16)
        s_scr[...] = s * cd_f + _dot_tn((kf * kd_f).astype(BF16), v)
        return carry

    lax.fori_loop(0, nc, fwd, 0, unroll=min(4, nc))
    sf_ref[0, 0] = s_scr[...]


def _ret_mix(dec_f, dec_b, q, k, v, gate, gn_g, s0_f, s0_b):
    bsz, t, _ = q.shape
    nh, dk, dv = RET_HEADS, RET_QK_DIM, RET_V_DIM
    chunk = min(RET_CHUNK, t)
    seq = lambda n: pl.BlockSpec((1, t, n), lambda b, h: (b, 0, h))
    dec = pl.BlockSpec((1, 1, LANES), lambda b, h: (h, 0, 0))
    state = pl.BlockSpec((1, 1, dk, dv), lambda b, h: (b, h, 0, 0))
    return pl.pallas_call(
        functools.partial(_ret_mix_kernel, chunk=chunk),
        grid=(bsz, nh),
        in_specs=[dec, dec, seq(dk), seq(dk), seq(dv), seq(dv),
                  pl.BlockSpec((1, dv), lambda b, h: (0, h)), state, state],
        out_specs=[seq(dv), state, state],
        out_shape=[jax.ShapeDtypeStruct((bsz, t, nh * dv), BF16),
                   jax.ShapeDtypeStruct((bsz, nh, dk, dv), F32),
                   jax.ShapeDtypeStruct((bsz, nh, dk, dv), F32)],
        scratch_shapes=[pltpu.VMEM((dk, dv), F32), pltpu.VMEM((t // chunk, dk, dv), BF16)],
        compiler_params=_params("parallel", "parallel"),
        name="ret_mix",
    )(dec_f, dec_b, q, k, v, gate, gn_g.reshape(1, nh * dv), s0_f, s0_b)


def _qkv_proj_kernel(h_ref, mod_ref, g_ref, w_ref, *rest, widths, q_scale, rope):
    if rope:
        cos_ref, sin_ref = rest[:2]
        outs = rest[2:]
    else:
        outs = rest
    shift, scale, _ = _mod_rows(mod_ref, 1)
    xb = _ada(h_ref[0], g_ref[...], shift, scale).astype(BF16)
    col = 0
    for idx, (o_ref, n) in enumerate(zip(outs, widths)):
        if o_ref is not None:
            y = _dot(xb, w_ref[:, col:col + n])
            for c0 in range(0, n, LANES):
                yc = y[:, c0:c0 + LANES]
                if rope and idx < 2:
                    lane = lax.broadcasted_iota(jnp.int32, yc.shape, 1)
                    partner = jnp.where((lane & 16) == 0, pltpu.roll(yc, LANES - 16, axis=1),
                                        pltpu.roll(yc, 16, axis=1))
                    yc = yc * cos_ref[...] + partner * sin_ref[...]
                if idx == 0:
                    yc = yc * q_scale
                o_ref[0, :, c0:c0 + LANES] = yc.astype(BF16)
        col += n


def _qkv_proj(h, mod, g, w, widths, q_scale, cos=None, sin=None, want_q=True):
    bsz, t, d = h.shape
    tm = min(PROJ_TILE, t)
    rope = cos is not None
    row = lambda n: pl.BlockSpec((1, tm, n), lambda b, i: (b, i, 0))
    in_specs = [row(d), _mod_spec(mod, d), pl.BlockSpec((1, d), lambda b, i: (0, 0)), _resident(w.shape)]
    args = [h, mod, g.reshape(1, d), w]
    if rope:
        in_specs += [pl.BlockSpec((tm, LANES), lambda b, i: (i, 0))] * 2
        args += [cos, sin]
    keep = [want_q, True, True]
    out_widths = [n for n, kp in zip(widths, keep) if kp]

    def body(*refs):
        n_in = len(args)
        outs = list(refs[n_in:])
        full = [outs.pop(0) if kp else None for kp in keep]
        _qkv_proj_kernel(*refs[:n_in], *full, widths=widths, q_scale=q_scale, rope=rope)

    res = pl.pallas_call(
        body,
        grid=(bsz, t // tm),
        in_specs=in_specs,
        out_specs=[row(n) for n in out_widths],
        out_shape=[jax.ShapeDtypeStruct((bsz, t, n), BF16) for n in out_widths],
        compiler_params=_params("parallel", "parallel"),
        name="qkv_proj",
    )(*args)
    return res if want_q else [None] + list(res)


def _split_heads(x):
    lane = lax.broadcasted_iota(jnp.int32, x.shape, 1)
    zero = jnp.zeros_like(x)
    return jnp.concatenate([jnp.where(lane < 64, x, zero), jnp.where(lane >= 64, x, zero)], axis=0)


def _merge_heads(o):
    r = o.shape[0] // 2
    lane = lax.broadcasted_iota(jnp.int32, (r, o.shape[1]), 1)
    return jnp.where(lane < 64, o[:r], o[r:])


def _with_ones(v):
    return jnp.concatenate([v, jnp.ones_like(v)], axis=1)


def _stage_with_ones(dst_ref, src_ref):
    for c0 in range(0, src_ref.shape[1], XPOSE_CHUNK):
        n = min(XPOSE_CHUNK, src_ref.shape[1] - c0)
        dst_ref[c0:c0 + n, :] = _with_ones(src_ref[0, c0:c0 + n, :])


def _stage_transposed(dst_ref, src_ref, row0, width):
    for c0 in range(0, width, XPOSE_CHUNK):
        n = min(XPOSE_CHUNK, width - c0)
        dst_ref[:, c0:c0 + n] = src_ref[0, row0 + c0:row0 + c0 + n, :].T


def _nat_kernel(q_ref, k_ref, v_ref, kc_ref, vc_ref, bias_ref, o_ref, kt_scr, va_scr, snb_a, scx_a, snb_b, scx_b,
                *, rows):
    w = GRID_W
    kh = min(NAT_KH, rows)
    nk = kh * w
    t = rows * w
    _stage_transposed(kt_scr.at[0], k_ref, 0, t)
    _stage_transposed(kt_scr.at[1], k_ref, w, t - 2 * w)
    kct = kc_ref[0].T
    _stage_with_ones(va_scr, v_ref)
    vca = _with_ones(vc_ref[0])

    def window(r):
        r0 = jnp.clip(r - kh // 2, 0, rows - kh)
        return r0, pl.multiple_of(r * w, w)

    def scores(r, snb_ref, scx_ref):
        r0, qs = window(r)
        ty = r - r0
        par = r0 & 1
        kts = pl.multiple_of((r0 - par) * w, 2 * w)
        q2 = _split_heads(q_ref[0, pl.ds(qs, w), :])
        bias = jnp.concatenate(
            [jnp.concatenate([bias_ref[hd, 2 * m - ty + NAT_KH - 1] for m in range(kh // 2)], axis=1)
             for hd in range(2)], axis=0)
        snb_ref[...] = _dot(q2, kt_scr[par, :, pl.ds(kts, nk)]) + bias
        scx_ref[...] = _dot(q2, kct)

    def finish(r, snb_ref, scx_ref):
        r0, qs = window(r)
        ks = pl.multiple_of(r0 * w, w)
        s_nb, s_cx = snb_ref[...], scx_ref[...]
        m = jnp.maximum(jnp.max(s_nb, axis=-1, keepdims=True), jnp.max(s_cx, axis=-1, keepdims=True))
        p_nb = jnp.exp2(s_nb - m)
        p_cx = jnp.exp2(s_cx - m)
        oa = _dot(p_nb.astype(BF16), va_scr[pl.ds(ks, nk), :]) + _dot(p_cx.astype(BF16), vca)
        o_ref[0, pl.ds(qs, w), :] = _merge_heads(oa[:, :LANES] / oa[:, LANES:]).astype(BF16)

    scores(0, snb_a, scx_a)

    def body(i, carry):
        r = 2 * i
        scores(r + 1, snb_b, scx_b)
        finish(r, snb_a, scx_a)
        scores(jnp.minimum(r + 2, rows - 1), snb_a, scx_a)
        finish(r + 1, snb_b, scx_b)
        return carry

    lax.fori_loop(0, rows // 2, body, 0, unroll=4)


def _nat_attention(q, k, v, kc, vc, bias):
    bsz, t, d = q.shape
    l = kc.shape[1]
    rows = t // GRID_W
    hp = d // LANES
    lat = pl.BlockSpec((1, t, LANES), lambda p, b: (b, 0, p))
    ctx = pl.BlockSpec((1, l, LANES), lambda p, b: (b, 0, p))
    return pl.pallas_call(
        functools.partial(_nat_kernel, rows=rows),
        grid=(hp, bsz),
        in_specs=[lat, lat, lat, ctx, ctx,
                  pl.BlockSpec((2,) + bias.shape[1:], lambda p, b: (p, 0, 0, 0))],
        out_specs=lat,
        out_shape=jax.ShapeDtypeStruct((bsz, t, d), BF16),
        scratch_shapes=[pltpu.VMEM((2, LANES, t), BF16), pltpu.VMEM((t, 2 * LANES), BF16)]
        + [pltpu.VMEM((2 * GRID_W, n), F32) for n in (min(NAT_KH, rows) * GRID_W, l)] * 2,
        compiler_params=_params("parallel", "parallel"),
        name="nat_attention",
    )(q, k, v, kc, vc, bias)


def _nat_bias_kernel(x_ref, o_ref, *, tn):
    w = GRID_W
    sh = w.bit_length() - 1
    assert 1 << sh == w
    k2 = x_ref.shape[1]
    n = pl.program_id(0) * tn + lax.broadcasted_iota(jnp.int32, (1, tn), 1)
    c, a, cc = n >> (sh + 1), (n >> sh) & 1, n & (w - 1)
    j = lax.broadcasted_iota(jnp.int32, (k2, 1), 0)
    sel = jnp.where(((j >> (sh + 1)) == a) & ((j & (2 * w - 1)) == cc - c + w - 1), 1.0, 0.0).astype(BF16)
    hi, mid, lo = _split3(x_ref[...])
    vals = _dot(hi, sel) + _dot(mid, sel) + _dot(lo, sel)
    c0 = jnp.clip(c - NAT_KW // 2, 0, w - NAT_KW)
    o_ref[...] = jnp.where((cc >= c0) & (cc < c0 + NAT_KW), vals * LOG2E, NEG_INF)


def _nat_bias(rpb):
    w = GRID_W
    nh, nr, ncol = rpb.shape
    left = w - NAT_KW
    v = jnp.pad(rpb.astype(F32), ((0, 0), (0, 0), (left, 2 * w - ncol - left)))
    pairs = jnp.concatenate([v[:, :-1], v[:, 1:]], axis=-1).reshape(nh * (nr - 1), 4 * w)
    tn = 16 * LANES
    out = pl.pallas_call(
        functools.partial(_nat_bias_kernel, tn=tn),
        grid=(w * 2 * w // tn,),
        in_specs=[pl.BlockSpec(pairs.shape, lambda i: (0, 0))],
        out_specs=pl.BlockSpec((pairs.shape[0], tn), lambda i: (0, i)),
        out_shape=jax.ShapeDtypeStruct((pairs.shape[0], w * 2 * w), F32),
        compiler_params=_params("parallel"),
        name="nat_bias",
    )(pairs)
    return out.reshape(nh, nr - 1, w, 2 * w)


def _ctx_attn_kernel(q_ref, k_ref, v_ref, o_ref):
    q2 = _split_heads(q_ref[0])
    s = _dot(q2, k_ref[0].T)
    p = jnp.exp2(s - jnp.max(s, axis=-1, keepdims=True))
    l = jnp.sum(p, axis=-1, keepdims=True)
    o_ref[0] = _merge_heads(_dot(p.astype(BF16), v_ref[0]) / l).astype(BF16)


def _ctx_attention(q, k, v):
    bsz, l, d = q.shape
    blk = pl.BlockSpec((1, l, LANES), lambda b, p: (b, 0, p))
    return pl.pallas_call(
        _ctx_attn_kernel,
        grid=(bsz, d // LANES),
        in_specs=[blk, blk, blk],
        out_specs=blk,
        out_shape=jax.ShapeDtypeStruct((bsz, l, d), BF16),
        compiler_params=_params("parallel", "parallel"),
        name="ctx_attention",
    )(q, k, v)


POOL_HALO = SUBLANES


def _split3(x):
    hi = x.astype(BF16)
    r1 = x - hi.astype(F32)
    mid = r1.astype(BF16)
    return hi, mid, (r1 - mid.astype(F32)).astype(BF16)


def _pool_kernel(h_ref, prev_ref, next_ref, mod_ref, g_ref, pw_ref, ps_ref, o_ref, band_scr, *, tm, t_total):
    i = pl.program_id(1)
    nt = pl.num_programs(1)
    hl = POOL_HALO
    x = h_ref[0]
    d = x.shape[-1]
    gd = d // len(POOL_WINDOWS)
    shift, scale, gate = _mod_rows(mod_ref, 1)

    @pl.when((pl.program_id(0) == 0) & (i == 0))
    def _():
        off = lax.broadcasted_iota(jnp.int32, (tm, tm), 1) - lax.broadcasted_iota(jnp.int32, (tm, tm), 0)
        for gi, win in enumerate(POOL_WINDOWS):
            band_scr[gi] = jnp.where((off >= -(win // 2)) & (off < win // 2), 1.0, 0.0).astype(BF16)

    xe = _ada(jnp.concatenate([prev_ref[0], x, next_ref[0]], axis=0), g_ref[...], shift, scale)
    xn = xe[hl:hl + tm]
    halo = jnp.concatenate([jnp.where(i > 0, xe[:hl], 0.0), jnp.where(i < nt - 1, xe[hl + tm:], 0.0)], axis=0)
    x3 = _split3(xn)
    h3 = _split3(halo)
    er = lax.broadcasted_iota(jnp.int32, (2 * hl, 2 * hl), 0)
    hu = lax.broadcasted_iota(jnp.int32, (2 * hl, 2 * hl), 1)
    dist = jnp.where(hu < hl, hu - hl - er, hu - er + hl)
    same_side = (er < hl) == (hu < hl)
    trow = i * tm + lax.broadcasted_iota(jnp.int32, (tm, 1), 0)
    parts = []
    for gi, win in enumerate(POOL_WINDOWS):
        half = win // 2
        cols = slice(gi * gd, (gi + 1) * gd)
        band = band_scr[gi]
        edge = jnp.where(same_side & (dist >= -half) & (dist < half), 1.0, 0.0).astype(BF16)
        tot = _dot(band, x3[0][:, cols]) + _dot(band, x3[1][:, cols]) + _dot(band, x3[2][:, cols])
        fix = _dot(edge, h3[0][:, cols]) + _dot(edge, h3[1][:, cols]) + _dot(edge, h3[2][:, cols])
        tot = jnp.concatenate([tot[:hl] + fix[:hl], tot[hl:tm - hl], tot[tm - hl:] + fix[hl:]], axis=0)
        cnt = (jnp.minimum(trow + half, t_total) - jnp.maximum(trow - half, 0)).astype(F32)
        pooled = (tot / cnt - xn[:, cols]).astype(BF16)
        parts.append(_dot(pooled, pw_ref[gi]))
    y = jnp.concatenate(parts, axis=-1) * ps_ref[...]
    o_ref[0] = x + gate * y


def _pool(h, mod, g, pw, ps):
    bsz, t, d = h.shape
    tm = TOKEN_TILE
    per = tm // POOL_HALO
    last = t // POOL_HALO - 1
    return pl.pallas_call(
        functools.partial(_pool_kernel, tm=tm, t_total=t),
        grid=(bsz, t // tm),
        in_specs=[pl.BlockSpec((1, tm, d), lambda b, i: (b, i, 0)),
                  pl.BlockSpec((1, POOL_HALO, d), lambda b, i: (b, jnp.maximum(i * per - 1, 0), 0)),
                  pl.BlockSpec((1, POOL_HALO, d), lambda b, i: (b, jnp.minimum((i + 1) * per, last), 0)),
                  _mod_spec(mod, d),
                  pl.BlockSpec((1, d), lambda b, i: (0, 0)),
                  _resident(pw.shape),
                  pl.BlockSpec((1, d), lambda b, i: (0, 0))],
        out_specs=pl.BlockSpec((1, tm, d), lambda b, i: (b, i, 0)),
        out_shape=jax.ShapeDtypeStruct((bsz, t, d), F32),
        scratch_shapes=[pltpu.VMEM((len(POOL_WINDOWS), tm, tm), BF16)],
        compiler_params=_params("arbitrary", "arbitrary"),
        name="pool",
    )(h, h, h, mod, g.reshape(1, d), pw, ps.reshape(1, d))


def _swa_kernel(sink_ref, q_ref, k_ref, v_ref, kc_ref, vc_ref, o_ref, kt_scr, va_scr, sloc_a, scx_a, sloc_b, scx_b,
                *, t_total):
    kv = pl.program_id(1)
    blk = SWA_BLOCK
    nb = t_total // blk
    grp = SWA_Q_HEADS // SWA_KV_HEADS
    nrow = grp * blk
    _stage_transposed(kt_scr, k_ref, 0, t_total)
    kct = kc_ref[0].T
    _stage_with_ones(va_scr, v_ref)
    vca = _with_ones(vc_ref[0])
    row = lax.broadcasted_iota(jnp.int32, (nrow, 1), 0)
    sink = jnp.zeros((nrow, 1), F32)
    for gi in range(grp):
        sink = jnp.where((row >= gi * blk) & (row < (gi + 1) * blk), sink_ref[kv * grp + gi] * LOG2E, sink)
    qi = lax.broadcasted_iota(jnp.int32, (nrow, blk), 0) & (blk - 1)
    kj = lax.broadcasted_iota(jnp.int32, (nrow, blk), 1)
    open_blk = jnp.zeros((nrow, blk), F32)
    prev_blk = jnp.where(kj >= qi, 0.0, NEG_INF)
    next_blk = jnp.where(kj <= qi, 0.0, NEG_INF)

    shut_blk = jnp.full((nrow, blk), NEG_INF, F32)
    span = 3 * blk
    mask_first = jnp.concatenate([open_blk, next_blk, shut_blk], axis=1)
    mask_mid = jnp.concatenate([prev_blk, open_blk, next_blk], axis=1)
    mask_last = jnp.concatenate([shut_blk, prev_blk, open_blk], axis=1)

    def offsets(n):
        qs = pl.multiple_of(n * blk, blk)
        return qs, pl.multiple_of(jnp.clip(qs - blk, 0, t_total - span), blk)

    def scores(n, mask, sloc_ref, scx_ref):
        qs, ks = offsets(n)
        qb = q_ref[0, pl.ds(qs, blk), :]
        q4 = jnp.concatenate([_split_heads(qb[:, :LANES]), _split_heads(qb[:, LANES:])], axis=0)
        sloc_ref[...] = _dot(q4, kt_scr[:, pl.ds(ks, span)]) + mask
        scx_ref[...] = _dot(q4, kct)

    def finish(n, sloc_ref, scx_ref):
        qs, ks = offsets(n)
        s_loc, s_cx = sloc_ref[...], scx_ref[...]
        m = jnp.maximum(jnp.maximum(jnp.max(s_loc, axis=-1, keepdims=True),
                                    jnp.max(s_cx, axis=-1, keepdims=True)), sink)
        p_loc = jnp.exp2(s_loc - m)
        p_cx = jnp.exp2(s_cx - m)
        oa = _dot(p_loc.astype(BF16), va_scr[pl.ds(ks, span), :]) + _dot(p_cx.astype(BF16), vca)
        o = oa[:, :LANES] / (oa[:, LANES:] + jnp.exp2(sink - m))
        out = jnp.concatenate([_merge_heads(o[:2 * blk]), _merge_heads(o[2 * blk:])], axis=-1)
        o_ref[0, pl.ds(qs, blk), :] = out.astype(BF16)

    slot_a, slot_b = (sloc_a, scx_a), (sloc_b, scx_b)
    scores(0, mask_first, *slot_a)
    scores(1, mask_mid, *slot_b)
    finish(0, *slot_a)

    def body(i, carry):
        n = 1 + 2 * i
        scores(n + 1, mask_mid, *slot_a)
        finish(n, *slot_b)
        scores(n + 2, mask_mid, *slot_b)
        finish(n + 1, *slot_a)
        return carry

    lax.fori_loop(0, (nb - 4) // 2, body, 0, unroll=2)
    scores(nb - 2, mask_mid, *slot_a)
    finish(nb - 3, *slot_b)
    scores(nb - 1, mask_last, *slot_b)
    finish(nb - 2, *slot_a)
    finish(nb - 1, *slot_b)


def _swa_attention(sink, q, k, v, kc, vc):
    bsz, t, d = q.shape
    l = kc.shape[1]
    nb = t // SWA_BLOCK
    assert nb >= 4 and nb % 2 == 0 and SWA_WINDOW == SWA_BLOCK
    qw = d // SWA_KV_HEADS
    grid_spec = pltpu.PrefetchScalarGridSpec(
        num_scalar_prefetch=1,
        grid=(bsz, SWA_KV_HEADS),
        in_specs=[pl.BlockSpec((1, t, qw), lambda b, h, s: (b, 0, h)),
                  pl.BlockSpec((1, t, LANES), lambda b, h, s: (b, 0, h)),
                  pl.BlockSpec((1, t, LANES), lambda b, h, s: (b, 0, h)),
                  pl.BlockSpec((1, l, LANES), lambda b, h, s: (b, 0, h)),
                  pl.BlockSpec((1, l, LANES), lambda b, h, s: (b, 0, h))],
        out_specs=pl.BlockSpec((1, t, qw), lambda b, h, s: (b, 0, h)),
        scratch_shapes=[pltpu.VMEM((LANES, t), BF16), pltpu.VMEM((t, 2 * LANES), BF16)]
        + [pltpu.VMEM((SWA_Q_HEADS // SWA_KV_HEADS * SWA_BLOCK, n), F32) for n in (3 * SWA_BLOCK, l)] * 2,
    )
    return pl.pallas_call(
        functools.partial(_swa_kernel, t_total=t),
        grid_spec=grid_spec,
        out_shape=jax.ShapeDtypeStruct((bsz, t, d), BF16),
        compiler_params=_params("parallel", "parallel"),
        name="swa_attention",
    )(sink, q, k, v, kc, vc)


def _rope_angles(positions, dim):
    seg = dim // len(positions)
    inv = ROPE_BASE ** (-np.arange(0, seg, 2, dtype=np.float64) / seg)
    return np.concatenate([np.tile(p.astype(np.float64)[:, None] * inv, (1, 2)) for p in positions], axis=-1)


def _table(x):
    return jnp.asarray(x.astype(np.float32))


def _dup_heads(w, heads, dh):
    d = w.shape[0]
    return jnp.broadcast_to(w.reshape(d, heads, 1, dh), (d, heads, 2, dh)).reshape(d, heads * 2 * dh)


def kernel(x, c, ctx, c_ctx, w_mod, b_mod, norm_g, ffn_w_in, ffn_w_out, ret_w_in, ret_w_out, ret_gn_g, ret_decay_f, ret_decay_b, nat_w_qkv, nat_w_o, nat_rpb, pool_w, pool_scale, swa_w_qkv, swa_w_o, swa_sink, final_norm_g):
    bsz, t, d = x.shape
    depth = w_mod.shape[0]
    cc = jnp.concatenate([c, c_ctx[None], jnp.zeros((SUBLANES - bsz - 1, d), F32)], axis=0)
    mods = _modulation(cc, w_mod, b_mod)
    h, hc = x, ctx
    for i in range(depth):
        kind, occ = i % N_MIXERS, i // N_MIXERS
        last = i == depth - 1
        ctx_live = (not last) or kind != 2
        ml = mods[i, :bsz].reshape(bsz, 9, d)
        mc = mods[i, bsz:bsz + 1].reshape(1, 9, d)
        if ctx_live:
            hc, w_in, w_out = _ffn_ctx(hc, mc, norm_g[i, 0], ffn_w_in, ffn_w_out, i, 0)
        else:
            w_in, w_out = _ffn_weights_bf16(ffn_w_in, ffn_w_out, i, 0)
        h = _ffn(h, ml, norm_g[i, 0], w_in, w_out, 0)
        g1 = norm_g[i, 1]
        yl = yc = None
        if kind == 0:
            assert not last, "retention as the last layer is not wired up"
            w = ret_w_in[occ].astype(BF16)
            wo = ret_w_out[occ].astype(BF16)
            ang = _rope_angles([np.arange(t)], RET_QK_DIM)[:, :RET_QK_DIM // 2]
            dec_f = jnp.broadcast_to(ret_decay_f[occ].astype(F32)[:, None, None], (RET_HEADS, 1, LANES))
            dec_b = jnp.broadcast_to(ret_decay_b[occ].astype(F32)[:, None, None], (RET_HEADS, 1, LANES))
            qc, kc, vc, gc = _ret_proj(hc, mc, g1, w)
            zeros = jnp.zeros((bsz, RET_HEADS, RET_QK_DIM, RET_V_DIM), F32)
            oc, s_f, s_b = _ret_mix(dec_f, dec_b, qc, kc, vc, gc, ret_gn_g[occ], zeros, zeros)
            ql, kl, vl, gl = _ret_proj(h, ml, g1, w, _table(np.cos(ang)), _table(np.sin(ang)))
            ol, _, _ = _ret_mix(dec_f, dec_b, ql, kl, vl, gl, ret_gn_g[occ], s_f, s_b)
            yl, yc = (ol, wo), (oc, wo)
        elif kind == 1:
            w = nat_w_qkv[occ].astype(BF16)
            wo = nat_w_o[occ].astype(BF16)
            widths = (d, d, d)
            qc, kc, vc = _qkv_proj(hc, mc, g1, w, widths, NAT_HEAD_DIM ** -0.5 * LOG2E, want_q=not last)
            ql, kl, vl = _qkv_proj(h, ml, g1, w, widths, NAT_HEAD_DIM ** -0.5 * LOG2E)
            yl = (_nat_attention(ql, kl, vl, kc, vc, _nat_bias(nat_rpb[occ])), wo)
            if not last:
                yc = (_ctx_attention(qc, kc, vc), wo)
        elif kind == 2:
            pw = pool_w[occ].astype(BF16)
            h_new = _pool(h, ml, g1, pw, pool_scale[occ])
            if not last:
                hc = _pool(hc, mc, g1, pw, pool_scale[occ])
            h = h_new
        else:
            assert last, "windowed attention with live context outputs is not wired up"
            nq = SWA_Q_HEADS * SWA_HEAD_DIM
            nkv = SWA_KV_HEADS * SWA_HEAD_DIM
            wq = swa_w_qkv[occ][:, :nq]
            wk = _dup_heads(swa_w_qkv[occ][:, nq:nq + nkv], SWA_KV_HEADS, SWA_HEAD_DIM)
            wv = _dup_heads(swa_w_qkv[occ][:, nq + nkv:], SWA_KV_HEADS, SWA_HEAD_DIM)
            w = jnp.concatenate([wq, wk, wv], axis=-1).astype(BF16)
            wo = swa_w_o[occ].astype(BF16)
            widths = (nq, 2 * nkv, 2 * nkv)
            tt = np.arange(t)
            ang = _rope_angles([tt // GRID_W, tt % GRID_W], SWA_HEAD_DIM)
            ang = np.tile(ang, (1, LANES // SWA_HEAD_DIM))
            lane = np.arange(LANES)
            sin = np.where(lane % 32 < 16, -np.sin(ang), np.sin(ang))
            _, kc, vc = _qkv_proj(hc, mc, g1, w, widths, SWA_HEAD_DIM ** -0.5 * LOG2E, want_q=False)
            ql, kl, vl = _qkv_proj(h, ml, g1, w, widths, SWA_HEAD_DIM ** -0.5 * LOG2E, _table(np.cos(ang)), _table(sin))
            yl = (_swa_attention(swa_sink[occ].astype(F32), ql, kl, vl, kc, vc), wo)
        if not last:
            hc, w_in, w_out = _ffn_ctx(hc, mc, norm_g[i, 2], ffn_w_in, ffn_w_out, i, 1, yc)
        else:
            w_in, w_out = _ffn_weights_bf16(ffn_w_in, ffn_w_out, i, 1)
        h = _ffn(h, ml, norm_g[i, 2], w_in, w_out, 1, yl, final_norm_g if last else None)
    return h
```

```python
import functools
import math

import jax
import jax.numpy as jnp
import numpy as np
from jax import lax
from jax.experimental import pallas as pl
from jax.experimental.pallas import tpu as pltpu

F32 = jnp.float32
BF16 = jnp.bfloat16

EPS = 1e-6
NEG_INF = -1e30
LOG2E = math.log2(math.e)
ROPE_BASE = 10000.0
GRID_W = 64
N_MIXERS = 4
FFN_HIDDEN = 2816
RET_HEADS = 4
RET_QK_DIM = 256
RET_V_DIM = 512
NAT_HEADS = 16
NAT_HEAD_DIM = 64
NAT_KH = 8
NAT_KW = 16
POOL_WINDOWS = (2, 4, 8, 16)
SWA_Q_HEADS = 16
SWA_KV_HEADS = 4
SWA_HEAD_DIM = 64
SWA_WINDOW = 128
SWA_BLOCK = 128

LANES = 128
SUBLANES = 8
VMEM_LIMIT = 56 * 1024 * 1024
TOKEN_TILE = 256
PROJ_TILE = 512
FFN_TILE = 1024
FFN_SUBTILE = 256
CTX_FFN_CHUNK = 256
RET_CHUNK = 256
XPOSE_CHUNK = 512


def _params(*sem):
    return pltpu.CompilerParams(dimension_semantics=sem, vmem_limit_bytes=VMEM_LIMIT)


def _resident(shape):
    nd = len(shape)
    return pl.BlockSpec(shape, lambda *_: (0,) * nd, pipeline_mode=pl.Buffered(1))


def _silu(x):
    return x * jax.nn.sigmoid(x)


def _ada(x, g, shift, scale):
    var = jnp.mean(x * x, axis=-1, keepdims=True)
    y = x * lax.rsqrt(var + EPS) * g
    return y * (1.0 + scale) + shift


def _mod_rows(mod_ref, j):
    return (mod_ref[0, 3 * j:3 * j + 1, :], mod_ref[0, 3 * j + 1:3 * j + 2, :],
            mod_ref[0, 3 * j + 2:3 * j + 3, :])


def _mod_spec(mod, d):
    if mod.shape[0] == 1:
        return pl.BlockSpec((1, 9, d), lambda b, t: (0, 0, 0))
    return pl.BlockSpec((1, 9, d), lambda b, t: (b, 0, 0))


def _dot(a, b):
    return jnp.dot(a, b, preferred_element_type=F32)


def _dot_nt(a, b):
    return lax.dot_general(a, b, (((1,), (1,)), ((), ())), preferred_element_type=F32)


def _dot_tn(a, b):
    return lax.dot_general(a, b, (((0,), (0,)), ((), ())), preferred_element_type=F32)


def _mod_kernel(c_ref, w_ref, b_ref, o_ref):
    s = _silu(c_ref[...]).astype(BF16)
    o_ref[0] = _dot(s, w_ref[0].astype(BF16)) + b_ref[0]


def _modulation(cc, w_mod, b_mod):
    depth, d, n = w_mod.shape
    tn = 2304
    return pl.pallas_call(
        _mod_kernel,
        grid=(depth, n // tn),
        in_specs=[pl.BlockSpec((SUBLANES, d), lambda l, j: (0, 0)),
                  pl.BlockSpec((1, d, tn), lambda l, j: (l, 0, j)),
                  pl.BlockSpec((1, 1, tn), lambda l, j: (l, 0, j))],
        out_specs=pl.BlockSpec((1, SUBLANES, tn), lambda l, j: (l, 0, j)),
        out_shape=jax.ShapeDtypeStruct((depth, SUBLANES, n), F32),
        compiler_params=_params("arbitrary", "arbitrary"),
        name="modulation",
    )(cc, w_mod, b_mod.reshape(depth, 1, n))


def _ffn_kernel(h_ref, mod_ref, g_ref, win_ref, wout_ref, *rest, j, mixer_out, final):
    o_ref = rest[-1]
    shift, scale, gate = _mod_rows(mod_ref, j)
    tm = h_ref.shape[1]
    sub = min(tm, FFN_SUBTILE)
    for r0 in range(0, tm, sub):
        x = h_ref[0, r0:r0 + sub, :]
        if mixer_out:
            y_ref, wo_ref = rest[:2]
            x = x + mod_ref[0, 5:6, :] * _dot(y_ref[0, r0:r0 + sub, :], wo_ref[...])
        xb = _ada(x, g_ref[...], shift, scale).astype(BF16)
        if len(win_ref.shape) == 3:
            wg, wv = win_ref[0], win_ref[1]
        else:
            f = win_ref.shape[1] // 2
            wg, wv = win_ref[:, :f], win_ref[:, f:]
        hid = (_silu(_dot(xb, wg)) * _dot(xb, wv)).astype(BF16)
        out = x + (0.5 * gate) * _dot(hid, wout_ref[...])
        if final:
            var = jnp.mean(out * out, axis=-1, keepdims=True)
            out = out * lax.rsqrt(var + EPS) * rest[-2][...]
        o_ref[0, r0:r0 + sub, :] = out


def _ffn(h, mod, g, w_in, w_out, half, mixer_out=None, final_g=None):
    bsz, t, d = h.shape
    tm = min(FFN_TILE, t)
    row = lambda n: pl.BlockSpec((1, tm, n), lambda b, i: (b, i, 0))
    vec = pl.BlockSpec((1, d), lambda b, i: (0, 0))
    in_specs = [row(d), _mod_spec(mod, d), vec, _resident(w_in.shape), _resident(w_out.shape)]
    args = [h, mod, g.reshape(1, d), w_in, w_out]
    if mixer_out is not None:
        y, wo = mixer_out
        in_specs += [row(y.shape[-1]), _resident(wo.shape)]
        args += [y, wo]
    if final_g is not None:
        in_specs.append(vec)
        args.append(final_g.reshape(1, d))
    return pl.pallas_call(
        functools.partial(_ffn_kernel, j=2 * half, mixer_out=mixer_out is not None, final=final_g is not None),
        grid=(bsz, t // tm),
        in_specs=in_specs,
        out_specs=row(d),
        out_shape=jax.ShapeDtypeStruct((bsz, t, d), F32),
        compiler_params=_params("parallel", "parallel"),
        name="ffn",
    )(*args)


def _ffn_ctx_kernel(h_ref, mod_ref, g_ref, wa_ref, wb_ref, wo_ref, *rest, j, mixer_out):
    o_ref, win_bf_ref, wout_bf_ref, x_scr, xb_scr, acc_scr = rest[-6:]
    s = pl.program_id(0)
    shift, scale, gate = _mod_rows(mod_ref, j)

    @pl.when(s == 0)
    def _():
        x = h_ref[...]
        if mixer_out:
            y_ref, wmix_ref = rest[:2]
            x = x + mod_ref[0, 5:6, :] * _dot(y_ref[...], wmix_ref[...])
        x_scr[...] = x
        xb_scr[...] = _ada(x, g_ref[...], shift, scale).astype(BF16)
        acc_scr[...] = jnp.zeros_like(acc_scr)

    wa, wb, wo = wa_ref[0, 0].astype(BF16), wb_ref[0, 0].astype(BF16), wo_ref[0, 0].astype(BF16)
    win_bf_ref[0] = wa
    win_bf_ref[1] = wb
    wout_bf_ref[...] = wo
    xb = xb_scr[...]
    hid = (_silu(_dot(xb, wa)) * _dot(xb, wb)).astype(BF16)
    acc_scr[...] += _dot(hid, wo)

    @pl.when(s == pl.num_programs(0) - 1)
    def _():
        o_ref[...] = x_scr[...] + (0.5 * gate) * acc_scr[...]


def _ffn_ctx(hc, mod, g, w_in, w_out, layer, half, mixer_out=None):
    bsz, l, d = hc.shape
    f = w_out.shape[2]
    fc = CTX_FFN_CHUNK
    nf = f // fc
    n = bsz * l
    const = lambda shape: pl.BlockSpec(shape, lambda s: (0,) * len(shape))
    in_specs = [const((n, d)), const((1, 9, d)), const((1, d)),
                pl.BlockSpec((1, 1, d, fc), lambda s: (layer, half, 0, s)),
                pl.BlockSpec((1, 1, d, fc), lambda s: (layer, half, 0, nf + s)),
                pl.BlockSpec((1, 1, fc, d), lambda s: (layer, half, s, 0))]
    args = [hc.reshape(n, d), mod, g.reshape(1, d), w_in, w_in, w_out]
    if mixer_out is not None:
        y, wmix = mixer_out
        in_specs += [const((n, y.shape[-1])), _resident(wmix.shape)]
        args += [y.reshape(n, y.shape[-1]), wmix]
    out, w_in_bf, w_out_bf = pl.pallas_call(
        functools.partial(_ffn_ctx_kernel, j=2 * half, mixer_out=mixer_out is not None),
        grid=(nf,),
        in_specs=in_specs,
        out_specs=[const((n, d)),
                   pl.BlockSpec((2, d, fc), lambda s: (0, 0, s)),
                   pl.BlockSpec((fc, d), lambda s: (s, 0))],
        out_shape=[jax.ShapeDtypeStruct((n, d), F32),
                   jax.ShapeDtypeStruct((2, d, f), BF16),
                   jax.ShapeDtypeStruct((f, d), BF16)],
        scratch_shapes=[pltpu.VMEM((n, d), F32), pltpu.VMEM((n, d), BF16), pltpu.VMEM((n, d), F32)],
        compiler_params=_params("arbitrary"),
        name="ffn_ctx",
    )(*args)
    return out.reshape(bsz, l, d), w_in_bf, w_out_bf


def _ffn_weights_bf16(w_in, w_out, layer, half):
    return w_in[layer, half].astype(BF16), w_out[layer, half].astype(BF16)


def _ret_proj_kernel(h_ref, mod_ref, g_ref, w_ref, *rest, rope):
    if rope:
        cos_ref, sin_ref, q_ref, k_ref, v_ref, gate_ref = rest
    else:
        q_ref, k_ref, v_ref, gate_ref = rest
    shift, scale, _ = _mod_rows(mod_ref, 1)
    dk, nh = RET_QK_DIM, RET_HEADS
    half = dk // 2
    v0 = 2 * nh * dk
    nv = nh * RET_V_DIM
    tm = h_ref.shape[1]
    sub = min(tm, FFN_SUBTILE)
    for r0 in range(0, tm, sub):
        rows = slice(r0, r0 + sub)
        xb = _ada(h_ref[0, rows, :], g_ref[...], shift, scale).astype(BF16)

        def rot(x):
            if not rope:
                return x.astype(BF16)
            c, s = cos_ref[rows, :], sin_ref[rows, :]
            x1, x2 = x[:, :half], x[:, half:]
            return jnp.concatenate([x1 * c - x2 * s, x2 * c + x1 * s], axis=-1).astype(BF16)

        for hd in range(nh):
            q_ref[0, rows, hd * dk:(hd + 1) * dk] = rot(_dot(xb, w_ref[:, hd * dk:(hd + 1) * dk]))
            kcol = nh * dk + hd * dk
            k_ref[0, rows, hd * dk:(hd + 1) * dk] = rot(_dot(xb, w_ref[:, kcol:kcol + dk]) * dk ** -0.5)
        v_ref[0, rows, :] = _dot(xb, w_ref[:, v0:v0 + nv]).astype(BF16)
        gate_ref[0, rows, :] = _silu(_dot(xb, w_ref[:, v0 + nv:v0 + 2 * nv])).astype(BF16)


def _ret_proj(h, mod, g, w, cos=None, sin=None):
    bsz, t, d = h.shape
    tm = min(PROJ_TILE, t)
    rope = cos is not None
    nqk = RET_HEADS * RET_QK_DIM
    nv = RET_HEADS * RET_V_DIM
    row = lambda n: pl.BlockSpec((1, tm, n), lambda b, i: (b, i, 0))
    in_specs = [row(d), _mod_spec(mod, d), pl.BlockSpec((1, d), lambda b, i: (0, 0)), _resident(w.shape)]
    args = [h, mod, g.reshape(1, d), w]
    if rope:
        in_specs += [pl.BlockSpec((tm, RET_QK_DIM // 2), lambda b, i: (i, 0))] * 2
        args += [cos, sin]
    return pl.pallas_call(
        functools.partial(_ret_proj_kernel, rope=rope),
        grid=(bsz, t // tm),
        in_specs=in_specs,
        out_specs=[row(nqk), row(nqk), row(nv), row(nv)],
        out_shape=[jax.ShapeDtypeStruct((bsz, t, nqk), BF16), jax.ShapeDtypeStruct((bsz, t, nqk), BF16),
                   jax.ShapeDtypeStruct((bsz, t, nv), BF16), jax.ShapeDtypeStruct((bsz, t, nv), BF16)],
        compiler_params=_params("parallel", "parallel"),
        name="ret_proj",
    )(*args)


def _log_sigmoid(x):
    return jnp.minimum(x, 0.0) - jnp.log(1.0 + jnp.exp(-jnp.abs(x)))


def _ret_mix_kernel(decf_ref, decb_ref, q_ref, k_ref, v_ref, g_ref, gn_ref, s0f_ref, s0b_ref,
                    y_ref, sf_ref, sb_ref, s_scr, sb_scr, *, chunk):
    c = chunk
    nc = q_ref.shape[1] // c
    lgf = _log_sigmoid(decf_ref[0])[:, :1]
    lgb = _log_sigmoid(decb_ref[0])[:, :1]
    dist = lax.broadcasted_iota(jnp.int32, (c, c), 0) - lax.broadcasted_iota(jnp.int32, (c, c), 1)
    decay = jnp.where(dist >= 0,
                      jnp.exp(jnp.maximum(dist, 0).astype(F32) * lgf),
                      jnp.exp(jnp.maximum(-dist, 0).astype(F32) * lgb))
    pos = lax.broadcasted_iota(jnp.int32, (c, 1), 0).astype(F32)
    qd_f, kd_f, cd_f = jnp.exp((pos + 1.0) * lgf), jnp.exp((c - 1.0 - pos) * lgf), jnp.exp(c * lgf)
    qd_b, kd_b, cd_b = jnp.exp((c - pos) * lgb), jnp.exp(pos * lgb), jnp.exp(c * lgb)

    def rows(n):
        return pl.ds(pl.multiple_of(n * c, c), c)

    s_scr[...] = s0b_ref[0, 0]

    def bwd(i, carry):
        n = nc - 1 - i
        s = s_scr[...]
        sb_scr[n] = s.astype(BF16)
        kd = (k_ref[0, rows(n), :].astype(F32) * kd_b).astype(BF16)
        s_scr[...] = s * cd_b + _dot_tn(kd, v_ref[0, rows(n), :])
        return carry

    lax.fori_loop(0, nc, bwd, 0, unroll=min(4, nc))
    sb_ref[0, 0] = s_scr[...]
    s_scr[...] = s0f_ref[0, 0]

    def fwd(n, carry):
        q, k, v = q_ref[0, rows(n), :], k_ref[0, rows(n), :], v_ref[0, rows(n), :]
        qf, kf = q.astype(F32), k.astype(F32)
        s = s_scr[...]
        att = (_dot(q, k.T) * decay).astype(BF16)
        o = (_dot(att, v) + _dot((qf * qd_f).astype(BF16), s.astype(BF16))
             + _dot((qf * qd_b).astype(BF16), sb_scr[n]))
        mu = jnp.mean(o, axis=-1, keepdims=True)
        ctr = o - mu
        var = jnp.mean(ctr * ctr, axis=-1, keepdims=True)
        on = ctr * lax.rsqrt(var + EPS) * gn_ref[...]
        y_ref[0, rows(n), :] = (g_ref[0, rows(n), :].astype(F32) * on).astype(BF16)
        s_scr[...] = s * cd_f + _dot_tn((kf * kd_f).astype(BF16), v)
        return carry

    lax.fori_loop(0, nc, fwd, 0, unroll=min(4, nc))
    sf_ref[0, 0] = s_scr[...]


def _ret_mix(dec_f, dec_b, q, k, v, gate, gn_g, s0_f, s0_b):
    bsz, t, _ = q.shape
    nh, dk, dv = RET_HEADS, RET_QK_DIM, RET_V_DIM
    chunk = min(RET_CHUNK, t)
    seq = lambda n: pl.BlockSpec((1, t, n), lambda b, h: (b, 0, h))
    dec = pl.BlockSpec((1, 1, LANES), lambda b, h: (h, 0, 0))
    state = pl.BlockSpec((1, 1, dk, dv), lambda b, h: (b, h, 0, 0))
    return pl.pallas_call(
        functools.partial(_ret_mix_kernel, chunk=chunk),
        grid=(bsz, nh),
        in_specs=[dec, dec, seq(dk), seq(dk), seq(dv), seq(dv),
                  pl.BlockSpec((1, dv), lambda b, h: (0, h)), state, state],
        out_specs=[seq(dv), state, state],
        out_shape=[jax.ShapeDtypeStruct((bsz, t, nh * dv), BF16),
                   jax.ShapeDtypeStruct((bsz, nh, dk, dv), F32),
                   jax.ShapeDtypeStruct((bsz, nh, dk, dv), F32)],
        scratch_shapes=[pltpu.VMEM((dk, dv), F32), pltpu.VMEM((t // chunk, dk, dv), BF16)],
        compiler_params=_params("parallel", "parallel"),
        name="ret_mix",
    )(dec_f, dec_b, q, k, v, gate, gn_g.reshape(1, nh * dv), s0_f, s0_b)


def _qkv_proj_kernel(h_ref, mod_ref, g_ref, w_ref, *rest, widths, q_scale, rope):
    if rope:
        cos_ref, sin_ref = rest[:2]
        outs = rest[2:]
    else:
        outs = rest
    shift, scale, _ = _mod_rows(mod_ref, 1)
    tm = h_ref.shape[1]
    sub = min(tm, FFN_SUBTILE)
    for r0 in range(0, tm, sub):
        rows = slice(r0, r0 + sub)
        xb = _ada(h_ref[0, rows, :], g_ref[...], shift, scale).astype(BF16)
        col = 0
        for idx, (o_ref, n) in enumerate(zip(outs, widths)):
            if o_ref is not None:
                y = _dot(xb, w_ref[:, col:col + n])
                for c0 in range(0, n, LANES):
                    yc = y[:, c0:c0 + LANES]
                    if rope and idx < 2:
                        lane = lax.broadcasted_iota(jnp.int32, yc.shape, 1)
                        partner = jnp.where((lane & 16) == 0, pltpu.roll(yc, LANES - 16, axis=1),
                                            pltpu.roll(yc, 16, axis=1))
                        yc = yc * cos_ref[rows, :] + partner * sin_ref[rows, :]
                    if idx == 0:
                        yc = yc * q_scale
                    o_ref[0, rows, c0:c0 + LANES] = yc.astype(BF16)
            col += n


def _qkv_proj(h, mod, g, w, widths, q_scale, cos=None, sin=None, want_q=True):
    bsz, t, d = h.shape
    tm = min(PROJ_TILE, t)
    rope = cos is not None
    row = lambda n: pl.BlockSpec((1, tm, n), lambda b, i: (b, i, 0))
    in_specs = [row(d), _mod_spec(mod, d), pl.BlockSpec((1, d), lambda b, i: (0, 0)), _resident(w.shape)]
    args = [h, mod, g.reshape(1, d), w]
    if rope:
        in_specs += [pl.BlockSpec((tm, LANES), lambda b, i: (i, 0))] * 2
        args += [cos, sin]
    keep = [want_q, True, True]
    out_widths = [n for n, kp in zip(widths, keep) if kp]

    def body(*refs):
        n_in = len(args)
        outs = list(refs[n_in:])
        full = [outs.pop(0) if kp else None for kp in keep]
        _qkv_proj_kernel(*refs[:n_in], *full, widths=widths, q_scale=q_scale, rope=rope)

    res = pl.pallas_call(
        body,
        grid=(bsz, t // tm),
        in_specs=in_specs,
        out_specs=[row(n) for n in out_widths],
        out_shape=[jax.ShapeDtypeStruct((bsz, t, n), BF16) for n in out_widths],
        compiler_params=_params("parallel", "parallel"),
        name="qkv_proj",
    )(*args)
    return res if want_q else [None] + list(res)


def _split_heads(x):
    lane = lax.broadcasted_iota(jnp.int32, x.shape, 1)
    zero = jnp.zeros_like(x)
    return jnp.concatenate([jnp.where(lane < 64, x, zero), jnp.where(lane >= 64, x, zero)], axis=0)


def _merge_heads(o):
    r = o.shape[0] // 2
    lane = lax.broadcasted_iota(jnp.int32, (r, o.shape[1]), 1)
    return jnp.where(lane < 64, o[:r], o[r:])


def _with_ones(v):
    return jnp.concatenate([v, jnp.ones_like(v)], axis=1)


def _stage_with_ones(dst_ref, src_ref):
    for c0 in range(0, src_ref.shape[1], XPOSE_CHUNK):
        n = min(XPOSE_CHUNK, src_ref.shape[1] - c0)
        dst_ref[c0:c0 + n, :] = _with_ones(src_ref[0, c0:c0 + n, :])


def _stage_transposed(dst_ref, src_ref, row0, width):
    for c0 in range(0, width, XPOSE_CHUNK):
        n = min(XPOSE_CHUNK, width - c0)
        dst_ref[:, c0:c0 + n] = src_ref[0, row0 + c0:row0 + c0 + n, :].T


def _nat_kernel(q_ref, k_ref, v_ref, kc_ref, vc_ref, bias_ref, o_ref, kt_scr, va_scr, snb_a, scx_a, snb_b, scx_b,
                *, rows):
    w = GRID_W
    kh = min(NAT_KH, rows)
    nk = kh * w
    t = rows * w
    _stage_transposed(kt_scr.at[0], k_ref, 0, t)
    _stage_transposed(kt_scr.at[1], k_ref, w, t - 2 * w)
    kct = kc_ref[0].T
    _stage_with_ones(va_scr, v_ref)
    vca = _with_ones(vc_ref[0])

    def window(r):
        r0 = jnp.clip(r - kh // 2, 0, rows - kh)
        return r0, pl.multiple_of(r * w, w)

    def scores(r, snb_ref, scx_ref):
        r0, qs = window(r)
        ty = r - r0
        par = r0 & 1
        kts = pl.multiple_of((r0 - par) * w, 2 * w)
        q2 = _split_heads(q_ref[0, pl.ds(qs, w), :])
        bias = jnp.concatenate(
            [jnp.concatenate([bias_ref[hd, 2 * m - ty + NAT_KH - 1] for m in range(kh // 2)], axis=1)
             for hd in range(2)], axis=0)
        snb_ref[...] = _dot(q2, kt_scr[par, :, pl.ds(kts, nk)]) + bias
        scx_ref[...] = _dot(q2, kct)

    def finish(r, snb_ref, scx_ref):
        r0, qs = window(r)
        ks = pl.multiple_of(r0 * w, w)
        s_nb, s_cx = snb_ref[...], scx_ref[...]
        m = jnp.maximum(jnp.max(s_nb, axis=-1, keepdims=True), jnp.max(s_cx, axis=-1, keepdims=True))
        p_nb = jnp.exp2(s_nb - m)
        p_cx = jnp.exp2(s_cx - m)
        oa = _dot(p_nb.astype(BF16), va_scr[pl.ds(ks, nk), :]) + _dot(p_cx.astype(BF16), vca)
        o_ref[0, pl.ds(qs, w), :] = _merge_heads(oa[:, :LANES] / oa[:, LANES:]).astype(BF16)

    scores(0, snb_a, scx_a)

    def body(i, carry):
        r = 2 * i
        scores(r + 1, snb_b, scx_b)
        finish(r, snb_a, scx_a)
        scores(jnp.minimum(r + 2, rows - 1), snb_a, scx_a)
        finish(r + 1, snb_b, scx_b)
        return carry

    lax.fori_loop(0, rows // 2, body, 0, unroll=8)


def _nat_attention(q, k, v, kc, vc, bias):
    bsz, t, d = q.shape
    l = kc.shape[1]
    rows = t // GRID_W
    hp = d // LANES
    lat = pl.BlockSpec((1, t, LANES), lambda p, b: (b, 0, p))
    ctx = pl.BlockSpec((1, l, LANES), lambda p, b: (b, 0, p))
    return pl.pallas_call(
        functools.partial(_nat_kernel, rows=rows),
        grid=(hp, bsz),
        in_specs=[lat, lat, lat, ctx, ctx,
                  pl.BlockSpec((2,) + bias.shape[1:], lambda p, b: (p, 0, 0, 0))],
        out_specs=lat,
        out_shape=jax.ShapeDtypeStruct((bsz, t, d), BF16),
        scratch_shapes=[pltpu.VMEM((2, LANES, t), BF16), pltpu.VMEM((t, 2 * LANES), BF16)]
        + [pltpu.VMEM((2 * GRID_W, n), F32) for n in (min(NAT_KH, rows) * GRID_W, l)] * 2,
        compiler_params=_params("parallel", "parallel"),
        name="nat_attention",
    )(q, k, v, kc, vc, bias)


def _nat_bias_kernel(x_ref, o_ref, *, tn):
    w = GRID_W
    sh = w.bit_length() - 1
    assert 1 << sh == w
    k2 = x_ref.shape[1]
    n = pl.program_id(0) * tn + lax.broadcasted_iota(jnp.int32, (1, tn), 1)
    c, a, cc = n >> (sh + 1), (n >> sh) & 1, n & (w - 1)
    j = lax.broadcasted_iota(jnp.int32, (k2, 1), 0)
    sel = jnp.where(((j >> (sh + 1)) == a) & ((j & (2 * w - 1)) == cc - c + w - 1), 1.0, 0.0).astype(BF16)
    hi, mid, lo = _split3(x_ref[...])
    vals = _dot(hi, sel) + _dot(mid, sel) + _dot(lo, sel)
    c0 = jnp.clip(c - NAT_KW // 2, 0, w - NAT_KW)
    o_ref[...] = jnp.where((cc >= c0) & (cc < c0 + NAT_KW), vals * LOG2E, NEG_INF)


def _nat_bias(rpb):
    w = GRID_W
    nh, nr, ncol = rpb.shape
    left = w - NAT_KW
    v = jnp.pad(rpb.astype(F32), ((0, 0), (0, 0), (left, 2 * w - ncol - left)))
    pairs = jnp.concatenate([v[:, :-1], v[:, 1:]], axis=-1).reshape(nh * (nr - 1), 4 * w)
    tn = 16 * LANES
    out = pl.pallas_call(
        functools.partial(_nat_bias_kernel, tn=tn),
        grid=(w * 2 * w // tn,),
        in_specs=[pl.BlockSpec(pairs.shape, lambda i: (0, 0))],
        out_specs=pl.BlockSpec((pairs.shape[0], tn), lambda i: (0, i)),
        out_shape=jax.ShapeDtypeStruct((pairs.shape[0], w * 2 * w), F32),
        compiler_params=_params("parallel"),
        name="nat_bias",
    )(pairs)
    return out.reshape(nh, nr - 1, w, 2 * w)


def _ctx_attn_kernel(q_ref, k_ref, v_ref, o_ref):
    q2 = _split_heads(q_ref[0])
    s = _dot(q2, k_ref[0].T)
    p = jnp.exp2(s - jnp.max(s, axis=-1, keepdims=True))
    l = jnp.sum(p, axis=-1, keepdims=True)
    o_ref[0] = _merge_heads(_dot(p.astype(BF16), v_ref[0]) / l).astype(BF16)


def _ctx_attention(q, k, v):
    bsz, l, d = q.shape
    blk = pl.BlockSpec((1, l, LANES), lambda b, p: (b, 0, p))
    return pl.pallas_call(
        _ctx_attn_kernel,
        grid=(bsz, d // LANES),
        in_specs=[blk, blk, blk],
        out_specs=blk,
        out_shape=jax.ShapeDtypeStruct((bsz, l, d), BF16),
        compiler_params=_params("parallel", "parallel"),
        name="ctx_attention",
    )(q, k, v)


POOL_HALO = SUBLANES


def _split3(x):
    hi = x.astype(BF16)
    r1 = x - hi.astype(F32)
    mid = r1.astype(BF16)
    return hi, mid, (r1 - mid.astype(F32)).astype(BF16)


def _pool_kernel(h_ref, prev_ref, next_ref, mod_ref, g_ref, pw_ref, ps_ref, o_ref, band_scr, *, tm, t_total):
    i = pl.program_id(1)
    nt = pl.num_programs(1)
    hl = POOL_HALO
    x = h_ref[0]
    d = x.shape[-1]
    gd = d // len(POOL_WINDOWS)
    shift, scale, gate = _mod_rows(mod_ref, 1)

    @pl.when((pl.program_id(0) == 0) & (i == 0))
    def _():
        off = lax.broadcasted_iota(jnp.int32, (tm, tm), 1) - lax.broadcasted_iota(jnp.int32, (tm, tm), 0)
        for gi, win in enumerate(POOL_WINDOWS):
            band_scr[gi] = jnp.where((off >= -(win // 2)) & (off < win // 2), 1.0, 0.0).astype(BF16)

    xe = _ada(jnp.concatenate([prev_ref[0], x, next_ref[0]], axis=0), g_ref[...], shift, scale)
    xn = xe[hl:hl + tm]
    halo = jnp.concatenate([jnp.where(i > 0, xe[:hl], 0.0), jnp.where(i < nt - 1, xe[hl + tm:], 0.0)], axis=0)
    x3 = _split3(xn)
    h3 = _split3(halo)
    er = lax.broadcasted_iota(jnp.int32, (2 * hl, 2 * hl), 0)
    hu = lax.broadcasted_iota(jnp.int32, (2 * hl, 2 * hl), 1)
    dist = jnp.where(hu < hl, hu - hl - er, hu - er + hl)
    same_side = (er < hl) == (hu < hl)
    trow = i * tm + lax.broadcasted_iota(jnp.int32, (tm, 1), 0)
    parts = []
    for gi, win in enumerate(POOL_WINDOWS):
        half = win // 2
        cols = slice(gi * gd, (gi + 1) * gd)
        band = band_scr[gi]
        edge = jnp.where(same_side & (dist >= -half) & (dist < half), 1.0, 0.0).astype(BF16)
        tot = _dot(band, x3[0][:, cols]) + _dot(band, x3[1][:, cols]) + _dot(band, x3[2][:, cols])
        fix = _dot(edge, h3[0][:, cols]) + _dot(edge, h3[1][:, cols]) + _dot(edge, h3[2][:, cols])
        tot = jnp.concatenate([tot[:hl] + fix[:hl], tot[hl:tm - hl], tot[tm - hl:] + fix[hl:]], axis=0)
        cnt = (jnp.minimum(trow + half, t_total) - jnp.maximum(trow - half, 0)).astype(F32)
        pooled = (tot / cnt - xn[:, cols]).astype(BF16)
        parts.append(_dot(pooled, pw_ref[gi]))
    y = jnp.concatenate(parts, axis=-1) * ps_ref[...]
    o_ref[0] = x + gate * y


def _pool(h, mod, g, pw, ps):
    bsz, t, d = h.shape
    tm = TOKEN_TILE
    per = tm // POOL_HALO
    last = t // POOL_HALO - 1
    return pl.pallas_call(
        functools.partial(_pool_kernel, tm=tm, t_total=t),
        grid=(bsz, t // tm),
        in_specs=[pl.BlockSpec((1, tm, d), lambda b, i: (b, i, 0)),
                  pl.BlockSpec((1, POOL_HALO, d), lambda b, i: (b, jnp.maximum(i * per - 1, 0), 0)),
                  pl.BlockSpec((1, POOL_HALO, d), lambda b, i: (b, jnp.minimum((i + 1) * per, last), 0)),
                  _mod_spec(mod, d),
                  pl.BlockSpec((1, d), lambda b, i: (0, 0)),
                  _resident(pw.shape),
                  pl.BlockSpec((1, d), lambda b, i: (0, 0))],
        out_specs=pl.BlockSpec((1, tm, d), lambda b, i: (b, i, 0)),
        out_shape=jax.ShapeDtypeStruct((bsz, t, d), F32),
        scratch_shapes=[pltpu.VMEM((len(POOL_WINDOWS), tm, tm), BF16)],
        compiler_params=_params("arbitrary", "arbitrary"),
        name="pool",
    )(h, h, h, mod, g.reshape(1, d), pw, ps.reshape(1, d))


def _swa_kernel(sink_ref, q_ref, k_ref, v_ref, kc_ref, vc_ref, o_ref, kt_scr, va_scr, sloc_a, scx_a, sloc_b, scx_b,
                *, t_total):
    kv = pl.program_id(1)
    blk = SWA_BLOCK
    nb = t_total // blk
    grp = SWA_Q_HEADS // SWA_KV_HEADS
    nrow = grp * blk
    _stage_transposed(kt_scr, k_ref, 0, t_total)
    kct = kc_ref[0].T
    _stage_with_ones(va_scr, v_ref)
    vca = _with_ones(vc_ref[0])
    row = lax.broadcasted_iota(jnp.int32, (nrow, 1), 0)
    sink = jnp.zeros((nrow, 1), F32)
    for gi in range(grp):
        sink = jnp.where((row >= gi * blk) & (row < (gi + 1) * blk), sink_ref[kv * grp + gi] * LOG2E, sink)
    qi = lax.broadcasted_iota(jnp.int32, (nrow, blk), 0) & (blk - 1)
    kj = lax.broadcasted_iota(jnp.int32, (nrow, blk), 1)
    open_blk = jnp.zeros((nrow, blk), F32)
    prev_blk = jnp.where(kj >= qi, 0.0, NEG_INF)
    next_blk = jnp.where(kj <= qi, 0.0, NEG_INF)

    shut_blk = jnp.full((nrow, blk), NEG_INF, F32)
    span = 3 * blk
    mask_first = jnp.concatenate([open_blk, next_blk, shut_blk], axis=1)
    mask_mid = jnp.concatenate([prev_blk, open_blk, next_blk], axis=1)
    mask_last = jnp.concatenate([shut_blk, prev_blk, open_blk], axis=1)

    def offsets(n):
        qs = pl.multiple_of(n * blk, blk)
        return qs, pl.multiple_of(jnp.clip(qs - blk, 0, t_total - span), blk)

    def scores(n, mask, sloc_ref, scx_ref):
        qs, ks = offsets(n)
        qb = q_ref[0, pl.ds(qs, blk), :]
        q4 = jnp.concatenate([_split_heads(qb[:, :LANES]), _split_heads(qb[:, LANES:])], axis=0)
        sloc_ref[...] = _dot(q4, kt_scr[:, pl.ds(ks, span)]) + mask
        scx_ref[...] = _dot(q4, kct)

    def finish(n, sloc_ref, scx_ref):
        qs, ks = offsets(n)
        s_loc, s_cx = sloc_ref[...], scx_ref[...]
        m = jnp.maximum(jnp.maximum(jnp.max(s_loc, axis=-1, keepdims=True),
                                    jnp.max(s_cx, axis=-1, keepdims=True)), sink)
        p_loc = jnp.exp2(s_loc - m)
        p_cx = jnp.exp2(s_cx - m)
        oa = _dot(p_loc.astype(BF16), va_scr[pl.ds(ks, span), :]) + _dot(p_cx.astype(BF16), vca)
        o = oa[:, :LANES] / (oa[:, LANES:] + jnp.exp2(sink - m))
        out = jnp.concatenate([_merge_heads(o[:2 * blk]), _merge_heads(o[2 * blk:])], axis=-1)
        o_ref[0, pl.ds(qs, blk), :] = out.astype(BF16)

    slot_a, slot_b = (sloc_a, scx_a), (sloc_b, scx_b)
    scores(0, mask_first, *slot_a)
    scores(1, mask_mid, *slot_b)
    finish(0, *slot_a)

    def body(i, carry):
        n = 1 + 2 * i
        scores(n + 1, mask_mid, *slot_a)
        finish(n, *slot_b)
        scores(n + 2, mask_mid, *slot_b)
        finish(n + 1, *slot_a)
        return carry

    lax.fori_loop(0, (nb - 4) // 2, body, 0, unroll=2)
    scores(nb - 2, mask_mid, *slot_a)
    finish(nb - 3, *slot_b)
    scores(nb - 1, mask_last, *slot_b)
    finish(nb - 2, *slot_a)
    finish(nb - 1, *slot_b)


def _swa_attention(sink, q, k, v, kc, vc):
    bsz, t, d = q.shape
    l = kc.shape[1]
    nb = t // SWA_BLOCK
    assert nb >= 4 and nb % 2 == 0 and SWA_WINDOW == SWA_BLOCK
    qw = d // SWA_KV_HEADS
    grid_spec = pltpu.PrefetchScalarGridSpec(
        num_scalar_prefetch=1,
        grid=(bsz, SWA_KV_HEADS),
        in_specs=[pl.BlockSpec((1, t, qw), lambda b, h, s: (b, 0, h)),
                  pl.BlockSpec((1, t, LANES), lambda b, h, s: (b, 0, h)),
                  pl.BlockSpec((1, t, LANES), lambda b, h, s: (b, 0, h)),
                  pl.BlockSpec((1, l, LANES), lambda b, h, s: (b, 0, h)),
                  pl.BlockSpec((1, l, LANES), lambda b, h, s: (b, 0, h))],
        out_specs=pl.BlockSpec((1, t, qw), lambda b, h, s: (b, 0, h)),
        scratch_shapes=[pltpu.VMEM((LANES, t), BF16), pltpu.VMEM((t, 2 * LANES), BF16)]
        + [pltpu.VMEM((SWA_Q_HEADS // SWA_KV_HEADS * SWA_BLOCK, n), F32) for n in (3 * SWA_BLOCK, l)] * 2,
    )
    return pl.pallas_call(
        functools.partial(_swa_kernel, t_total=t),
        grid_spec=grid_spec,
        out_shape=jax.ShapeDtypeStruct((bsz, t, d), BF16),
        compiler_params=_params("parallel", "parallel"),
        name="swa_attention",
    )(sink, q, k, v, kc, vc)


def _rope_angles(positions, dim):
    seg = dim // len(positions)
    inv = ROPE_BASE ** (-np.arange(0, seg, 2, dtype=np.float64) / seg)
    return np.concatenate([np.tile(p.astype(np.float64)[:, None] * inv, (1, 2)) for p in positions], axis=-1)


def _table(x):
    return jnp.asarray(x.astype(np.float32))


def _dup_heads(w, heads, dh):
    d = w.shape[0]
    return jnp.broadcast_to(w.reshape(d, heads, 1, dh), (d, heads, 2, dh)).reshape(d, heads * 2 * dh)


def kernel(x, c, ctx, c_ctx, w_mod, b_mod, norm_g, ffn_w_in, ffn_w_out, ret_w_in, ret_w_out, ret_gn_g, ret_decay_f, ret_decay_b, nat_w_qkv, nat_w_o, nat_rpb, pool_w, pool_scale, swa_w_qkv, swa_w_o, swa_sink, final_norm_g):
    bsz, t, d = x.shape
    depth = w_mod.shape[0]
    cc = jnp.concatenate([c, c_ctx[None], jnp.zeros((SUBLANES - bsz - 1, d), F32)], axis=0)
    mods = _modulation(cc, w_mod, b_mod)
    h, hc = x, ctx
    for i in range(depth):
        kind, occ = i % N_MIXERS, i // N_MIXERS
        last = i == depth - 1
        ctx_live = (not last) or kind != 2
        ml = mods[i, :bsz].reshape(bsz, 9, d)
        mc = mods[i, bsz:bsz + 1].reshape(1, 9, d)
        if ctx_live:
            hc, w_in, w_out = _ffn_ctx(hc, mc, norm_g[i, 0], ffn_w_in, ffn_w_out, i, 0)
        else:
            w_in, w_out = _ffn_weights_bf16(ffn_w_in, ffn_w_out, i, 0)
        h = _ffn(h, ml, norm_g[i, 0], w_in, w_out, 0)
        g1 = norm_g[i, 1]
        yl = yc = None
        if kind == 0:
            assert not last, "retention as the last layer is not wired up"
            w = ret_w_in[occ].astype(BF16)
            wo = ret_w_out[occ].astype(BF16)
            ang = _rope_angles([np.arange(t)], RET_QK_DIM)[:, :RET_QK_DIM // 2]
            dec_f = jnp.broadcast_to(ret_decay_f[occ].astype(F32)[:, None, None], (RET_HEADS, 1, LANES))
            dec_b = jnp.broadcast_to(ret_decay_b[occ].astype(F32)[:, None, None], (RET_HEADS, 1, LANES))
            qc, kc, vc, gc = _ret_proj(hc, mc, g1, w)
            zeros = jnp.zeros((bsz, RET_HEADS, RET_QK_DIM, RET_V_DIM), F32)
            oc, s_f, s_b = _ret_mix(dec_f, dec_b, qc, kc, vc, gc, ret_gn_g[occ], zeros, zeros)
            ql, kl, vl, gl = _ret_proj(h, ml, g1, w, _table(np.cos(ang)), _table(np.sin(ang)))
            ol, _, _ = _ret_mix(dec_f, dec_b, ql, kl, vl, gl, ret_gn_g[occ], s_f, s_b)
            yl, yc = (ol, wo), (oc, wo)
        elif kind == 1:
            w = nat_w_qkv[occ].astype(BF16)
            wo = nat_w_o[occ].astype(BF16)
            widths = (d, d, d)
            qc, kc, vc = _qkv_proj(hc, mc, g1, w, widths, NAT_HEAD_DIM ** -0.5 * LOG2E, want_q=not last)
            ql, kl, vl = _qkv_proj(h, ml, g1, w, widths, NAT_HEAD_DIM ** -0.5 * LOG2E)
            yl = (_nat_attention(ql, kl, vl, kc, vc, _nat_bias(nat_rpb[occ])), wo)
            if not last:
                yc = (_ctx_attention(qc, kc, vc), wo)
        elif kind == 2:
            pw = pool_w[occ].astype(BF16)
            h_new = _pool(h, ml, g1, pw, pool_scale[occ])
            if not last:
                hc = _pool(hc, mc, g1, pw, pool_scale[occ])
            h = h_new
        else:
            assert last, "windowed attention with live context outputs is not wired up"
            nq = SWA_Q_HEADS * SWA_HEAD_DIM
            nkv = SWA_KV_HEADS * SWA_HEAD_DIM
            wq = swa_w_qkv[occ][:, :nq]
            wk = _dup_heads(swa_w_qkv[occ][:, nq:nq + nkv], SWA_KV_HEADS, SWA_HEAD_DIM)
            wv = _dup_heads(swa_w_qkv[occ][:, nq + nkv:], SWA_KV_HEADS, SWA_HEAD_DIM)
            w = jnp.concatenate([wq, wk, wv], axis=-1).astype(BF16)
            wo = swa_w_o[occ].astype(BF16)
            widths = (nq, 2 * nkv, 2 * nkv)
            tt = np.arange(t)
            ang = _rope_angles([tt // GRID_W, tt % GRID_W], SWA_HEAD_DIM)
            ang = np.tile(ang, (1, LANES // SWA_HEAD_DIM))
            lane = np.arange(LANES)
            sin = np.where(lane % 32 < 16, -np.sin(ang), np.sin(ang))
            _, kc, vc = _qkv_proj(hc, mc, g1, w, widths, SWA_HEAD_DIM ** -0.5 * LOG2E, want_q=False)
            ql, kl, vl = _qkv_proj(h, ml, g1, w, widths, SWA_HEAD_DIM ** -0.5 * LOG2E, _table(np.cos(ang)), _table(sin))
            yl = (_swa_attention(swa_sink[occ].astype(F32), ql, kl, vl, kc, vc), wo)
        if not last:
            hc, w_in, w_out = _ffn_ctx(hc, mc, norm_g[i, 2], ffn_w_in, ffn_w_out, i, 1, yc)
        else:
            w_in, w_out = _ffn_weights_bf16(ffn_w_in, ffn_w_out, i, 1)
        h = _ffn(h, ml, norm_g[i, 2], w_in, w_out, 1, yl, final_norm_g if last else None)
    return h
```

```python
import functools
import math

import jax
import jax.numpy as jnp
import numpy as np
from jax import lax
from jax.experimental import pallas as pl
from jax.experimental.pallas import tpu as pltpu

F32 = jnp.float32
BF16 = jnp.bfloat16

EPS = 1e-6
NEG_INF = -1e30
LOG2E = math.log2(math.e)
ROPE_BASE = 10000.0
GRID_W = 64
N_MIXERS = 4
FFN_HIDDEN = 2816
RET_HEADS = 4
RET_QK_DIM = 256
RET_V_DIM = 512
NAT_HEADS = 16
NAT_HEAD_DIM = 64
NAT_KH = 8
NAT_KW = 16
POOL_WINDOWS = (2, 4, 8, 16)
SWA_Q_HEADS = 16
SWA_KV_HEADS = 4
SWA_HEAD_DIM = 64
SWA_WINDOW = 128
SWA_BLOCK = 128

LANES = 128
SUBLANES = 8
VMEM_LIMIT = 56 * 1024 * 1024
TOKEN_TILE = 256
PROJ_TILE = 512
FFN_TILE = 1024
FFN_SUBTILE = 256
CTX_FFN_CHUNK = 256
RET_CHUNK = 256
XPOSE_CHUNK = 512


def _params(*sem):
    return pltpu.CompilerParams(dimension_semantics=sem, vmem_limit_bytes=VMEM_LIMIT)


def _resident(shape):
    nd = len(shape)
    return pl.BlockSpec(shape, lambda *_: (0,) * nd, pipeline_mode=pl.Buffered(1))


def _silu(x):
    return x * jax.nn.sigmoid(x)


def _ada(x, g, shift, scale):
    var = jnp.mean(x * x, axis=-1, keepdims=True)
    y = x * lax.rsqrt(var + EPS) * g
    return y * (1.0 + scale) + shift


def _mod_rows(mod_ref, j):
    return (mod_ref[0, 3 * j:3 * j + 1, :], mod_ref[0, 3 * j + 1:3 * j + 2, :],
            mod_ref[0, 3 * j + 2:3 * j + 3, :])


def _mod_spec(mod, d):
    if mod.shape[0] == 1:
        return pl.BlockSpec((1, 9, d), lambda b, t: (0, 0, 0))
    return pl.BlockSpec((1, 9, d), lambda b, t: (b, 0, 0))


def _dot(a, b):
    return jnp.dot(a, b, preferred_element_type=F32)


def _dot_nt(a, b):
    return lax.dot_general(a, b, (((1,), (1,)), ((), ())), preferred_element_type=F32)


def _dot_tn(a, b):
    return lax.dot_general(a, b, (((0,), (0,)), ((), ())), preferred_element_type=F32)


def _mod_kernel(c_ref, w_ref, b_ref, o_ref):
    s = _silu(c_ref[...]).astype(BF16)
    o_ref[0] = _dot(s, w_ref[0].astype(BF16)) + b_ref[0]


def _modulation(cc, w_mod, b_mod):
    depth, d, n = w_mod.shape
    tn = 2304
    return pl.pallas_call(
        _mod_kernel,
        grid=(depth, n // tn),
        in_specs=[pl.BlockSpec((SUBLANES, d), lambda l, j: (0, 0)),
                  pl.BlockSpec((1, d, tn), lambda l, j: (l, 0, j)),
                  pl.BlockSpec((1, 1, tn), lambda l, j: (l, 0, j))],
        out_specs=pl.BlockSpec((1, SUBLANES, tn), lambda l, j: (l, 0, j)),
        out_shape=jax.ShapeDtypeStruct((depth, SUBLANES, n), F32),
        compiler_params=_params("arbitrary", "arbitrary"),
        name="modulation",
    )(cc, w_mod, b_mod.reshape(depth, 1, n))


def _ffn_kernel(h_ref, mod_ref, g_ref, win_ref, wout_ref, *rest, j, mixer_out, final):
    o_ref = rest[-1]
    shift, scale, gate = _mod_rows(mod_ref, j)
    tm = h_ref.shape[1]
    sub = min(tm, FFN_SUBTILE)
    for r0 in range(0, tm, sub):
        x = h_ref[0, r0:r0 + sub, :]
        if mixer_out:
            y_ref, wo_ref = rest[:2]
            x = x + mod_ref[0, 5:6, :] * _dot(y_ref[0, r0:r0 + sub, :], wo_ref[...])
        xb = _ada(x, g_ref[...], shift, scale).astype(BF16)
        if len(win_ref.shape) == 3:
            wg, wv = win_ref[0], win_ref[1]
        else:
            f = win_ref.shape[1] // 2
            wg, wv = win_ref[:, :f], win_ref[:, f:]
        hid = (_silu(_dot(xb, wg)) * _dot(xb, wv)).astype(BF16)
        out = x + (0.5 * gate) * _dot(hid, wout_ref[...])
        if final:
            var = jnp.mean(out * out, axis=-1, keepdims=True)
            out = out * lax.rsqrt(var + EPS) * rest[-2][...]
        o_ref[0, r0:r0 + sub, :] = out


def _ffn(h, mod, g, w_in, w_out, half, mixer_out=None, final_g=None):
    bsz, t, d = h.shape
    tm = min(FFN_TILE, t)
    row = lambda n: pl.BlockSpec((1, tm, n), lambda b, i: (b, i, 0))
    vec = pl.BlockSpec((1, d), lambda b, i: (0, 0))
    in_specs = [row(d), _mod_spec(mod, d), vec, _resident(w_in.shape), _resident(w_out.shape)]
    args = [h, mod, g.reshape(1, d), w_in, w_out]
    if mixer_out is not None:
        y, wo = mixer_out
        in_specs += [row(y.shape[-1]), _resident(wo.shape)]
        args += [y, wo]
    if final_g is not None:
        in_specs.append(vec)
        args.append(final_g.reshape(1, d))
    return pl.pallas_call(
        functools.partial(_ffn_kernel, j=2 * half, mixer_out=mixer_out is not None, final=final_g is not None),
        grid=(bsz, t // tm),
        in_specs=in_specs,
        out_specs=row(d),
        out_shape=jax.ShapeDtypeStruct((bsz, t, d), F32),
        compiler_params=_params("parallel", "parallel"),
        name="ffn",
    )(*args)


def _ffn_ctx_kernel(h_ref, mod_ref, g_ref, wa_ref, wb_ref, wo_ref, *rest, j, mixer_out):
    o_ref, win_bf_ref, wout_bf_ref, x_scr, xb_scr, acc_scr = rest[-6:]
    s = pl.program_id(0)
    shift, scale, gate = _mod_rows(mod_ref, j)

    @pl.when(s == 0)
    def _():
        x = h_ref[...]
        if mixer_out:
            y_ref, wmix_ref = rest[:2]
            x = x + mod_ref[0, 5:6, :] * _dot(y_ref[...], wmix_ref[...])
        x_scr[...] = x
        xb_scr[...] = _ada(x, g_ref[...], shift, scale).astype(BF16)
        acc_scr[...] = jnp.zeros_like(acc_scr)

    wa, wb, wo = wa_ref[0, 0].astype(BF16), wb_ref[0, 0].astype(BF16), wo_ref[0, 0].astype(BF16)
    win_bf_ref[0] = wa
    win_bf_ref[1] = wb
    wout_bf_ref[...] = wo
    xb = xb_scr[...]
    hid = (_silu(_dot(xb, wa)) * _dot(xb, wb)).astype(BF16)
    acc_scr[...] += _dot(hid, wo)

    @pl.when(s == pl.num_programs(0) - 1)
    def _():
        o_ref[...] = x_scr[...] + (0.5 * gate) * acc_scr[...]


def _ffn_ctx(hc, mod, g, w_in, w_out, layer, half, mixer_out=None):
    bsz, l, d = hc.shape
    f = w_out.shape[2]
    fc = CTX_FFN_CHUNK
    nf = f // fc
    n = bsz * l
    const = lambda shape: pl.BlockSpec(shape, lambda s: (0,) * len(shape))
    in_specs = [const((n, d)), const((1, 9, d)), const((1, d)),
                pl.BlockSpec((1, 1, d, fc), lambda s: (layer, half, 0, s)),
                pl.BlockSpec((1, 1, d, fc), lambda s: (layer, half, 0, nf + s)),
                pl.BlockSpec((1, 1, fc, d), lambda s: (layer, half, s, 0))]
    args = [hc.reshape(n, d), mod, g.reshape(1, d), w_in, w_in, w_out]
    if mixer_out is not None:
        y, wmix = mixer_out
        in_specs += [const((n, y.shape[-1])), _resident(wmix.shape)]
        args += [y.reshape(n, y.shape[-1]), wmix]
    out, w_in_bf, w_out_bf = pl.pallas_call(
        functools.partial(_ffn_ctx_kernel, j=2 * half, mixer_out=mixer_out is not None),
        grid=(nf,),
        in_specs=in_specs,
        out_specs=[const((n, d)),
                   pl.BlockSpec((2, d, fc), lambda s: (0, 0, s)),
                   pl.BlockSpec((fc, d), lambda s: (s, 0))],
        out_shape=[jax.ShapeDtypeStruct((n, d), F32),
                   jax.ShapeDtypeStruct((2, d, f), BF16),
                   jax.ShapeDtypeStruct((f, d), BF16)],
        scratch_shapes=[pltpu.VMEM((n, d), F32), pltpu.VMEM((n, d), BF16), pltpu.VMEM((n, d), F32)],
        compiler_params=_params("arbitrary"),
        name="ffn_ctx",
    )(*args)
    return out.reshape(bsz, l, d), w_in_bf, w_out_bf


def _ffn_weights_bf16(w_in, w_out, layer, half):
    return w_in[layer, half].astype(BF16), w_out[layer, half].astype(BF16)


def _ret_proj_kernel(h_ref, mod_ref, g_ref, w_ref, *rest, rope):
    if rope:
        cos_ref, sin_ref, q_ref, k_ref, v_ref, gate_ref = rest
    else:
        q_ref, k_ref, v_ref, gate_ref = rest
    shift, scale, _ = _mod_rows(mod_ref, 1)
    dk, nh = RET_QK_DIM, RET_HEADS
    half = dk // 2
    v0 = 2 * nh * dk
    nv = nh * RET_V_DIM
    tm = h_ref.shape[1]
    sub = min(tm, FFN_SUBTILE)
    for r0 in range(0, tm, sub):
        rows = slice(r0, r0 + sub)
        xb = _ada(h_ref[0, rows, :], g_ref[...], shift, scale).astype(BF16)

        def rot(x):
            if not rope:
                return x.astype(BF16)
            c, s = cos_ref[rows, :], sin_ref[rows, :]
            x1, x2 = x[:, :half], x[:, half:]
            return jnp.concatenate([x1 * c - x2 * s, x2 * c + x1 * s], axis=-1).astype(BF16)

        for hd in range(nh):
            q_ref[0, rows, hd * dk:(hd + 1) * dk] = rot(_dot(xb, w_ref[:, hd * dk:(hd + 1) * dk]))
            kcol = nh * dk + hd * dk
            k_ref[0, rows, hd * dk:(hd + 1) * dk] = rot(_dot(xb, w_ref[:, kcol:kcol + dk]) * dk ** -0.5)
        v_ref[0, rows, :] = _dot(xb, w_ref[:, v0:v0 + nv]).astype(BF16)
        gate_ref[0, rows, :] = _silu(_dot(xb, w_ref[:, v0 + nv:v0 + 2 * nv])).astype(BF16)


def _ret_proj(h, mod, g, w, cos=None, sin=None):
    bsz, t, d = h.shape
    tm = min(PROJ_TILE, t)
    rope = cos is not None
    nqk = RET_HEADS * RET_QK_DIM
    nv = RET_HEADS * RET_V_DIM
    row = lambda n: pl.BlockSpec((1, tm, n), lambda b, i: (b, i, 0))
    in_specs = [row(d), _mod_spec(mod, d), pl.BlockSpec((1, d), lambda b, i: (0, 0)), _resident(w.shape)]
    args = [h, mod, g.reshape(1, d), w]
    if rope:
        in_specs += [pl.BlockSpec((tm, RET_QK_DIM // 2), lambda b, i: (i, 0))] * 2
        args += [cos, sin]
    return pl.pallas_call(
        functools.partial(_ret_proj_kernel, rope=rope),
        grid=(bsz, t // tm),
        in_specs=in_specs,
        out_specs=[row(nqk), row(nqk), row(nv), row(nv)],
        out_shape=[jax.ShapeDtypeStruct((bsz, t, nqk), BF16), jax.ShapeDtypeStruct((bsz, t, nqk), BF16),
                   jax.ShapeDtypeStruct((bsz, t, nv), BF16), jax.ShapeDtypeStruct((bsz, t, nv), BF16)],
        compiler_params=_params("parallel", "parallel"),
        name="ret_proj",
    )(*args)


def _log_sigmoid(x):
    return jnp.minimum(x, 0.0) - jnp.log(1.0 + jnp.exp(-jnp.abs(x)))


def _ret_mix_kernel(decf_ref, decb_ref, q_ref, k_ref, v_ref, g_ref, gn_ref, s0f_ref, s0b_ref,
                    y_ref, sf_ref, sb_ref, s_scr, sb_scr, *, chunk):
    c = chunk
    nc = q_ref.shape[1] // c
    lgf = _log_sigmoid(decf_ref[0])[:, :1]
    lgb = _log_sigmoid(decb_ref[0])[:, :1]
    dist = lax.broadcasted_iota(jnp.int32, (c, c), 0) - lax.broadcasted_iota(jnp.int32, (c, c), 1)
    decay = jnp.where(dist >= 0,
                      jnp.exp(jnp.maximum(dist, 0).astype(F32) * lgf),
                      jnp.exp(jnp.maximum(-dist, 0).astype(F32) * lgb))
    pos = lax.broadcasted_iota(jnp.int32, (c, 1), 0).astype(F32)
    qd_f, kd_f, cd_f = jnp.exp((pos + 1.0) * lgf), jnp.exp((c - 1.0 - pos) * lgf), jnp.exp(c * lgf)
    qd_b, kd_b, cd_b = jnp.exp((c - pos) * lgb), jnp.exp(pos * lgb), jnp.exp(c * lgb)

    def rows(n):
        return pl.ds(pl.multiple_of(n * c, c), c)

    s_scr[...] = s0b_ref[0, 0]

    def bwd(i, carry):
        n = nc - 1 - i
        s = s_scr[...]
        sb_scr[n] = s.astype(BF16)
        kd = (k_ref[0, rows(n), :].astype(F32) * kd_b).astype(BF16)
        s_scr[...] = s * cd_b + _dot_tn(kd, v_ref[0, rows(n), :])
        return carry

    lax.fori_loop(0, nc, bwd, 0, unroll=min(8, nc))
    sb_ref[0, 0] = s_scr[...]
    s_scr[...] = s0f_ref[0, 0]

    def fwd(n, carry):
        q, k, v = q_ref[0, rows(n), :], k_ref[0, rows(n), :], v_ref[0, rows(n), :]
        qf, kf = q.astype(F32), k.astype(F32)
        s = s_scr[...]
        att = (_dot(q, k.T) * decay).astype(BF16)
        o = (_dot(att, v) + _dot((qf * qd_f).astype(BF16), s.astype(BF16))
             + _dot((qf * qd_b).astype(BF16), sb_scr[n]))
        mu = jnp.mean(o, axis=-1, keepdims=True)
        ctr = o - mu
        var = jnp.mean(ctr * ctr, axis=-1, keepdims=True)
        on = ctr * lax.rsqrt(var + EPS) * gn_ref[...]
        y_ref[0, rows(n), :] = (g_ref[0, rows(n), :].astype(F32) * on).astype(BF16)
        s_scr[...] = s * cd_f + _dot_tn((kf * kd_f).astype(BF16), v)
        return carry

    lax.fori_loop(0, nc, fwd, 0, unroll=min(8, nc))
    sf_ref[0, 0] = s_scr[...]


def _ret_mix(dec_f, dec_b, q, k, v, gate, gn_g, s0_f, s0_b):
    bsz, t, _ = q.shape
    nh, dk, dv = RET_HEADS, RET_QK_DIM, RET_V_DIM
    chunk = min(RET_CHUNK, t)
    seq = lambda n: pl.BlockSpec((1, t, n), lambda b, h: (b, 0, h))
    dec = pl.BlockSpec((1, 1, LANES), lambda b, h: (h, 0, 0))
    state = pl.BlockSpec((1, 1, dk, dv), lambda b, h: (b, h, 0, 0))
    return pl.pallas_call(
        functools.partial(_ret_mix_kernel, chunk=chunk),
        grid=(bsz, nh),
        in_specs=[dec, dec, seq(dk), seq(dk), seq(dv), seq(dv),
                  pl.BlockSpec((1, dv), lambda b, h: (0, h)), state, state],
        out_specs=[seq(dv), state, state],
        out_shape=[jax.ShapeDtypeStruct((bsz, t, nh * dv), BF16),
                   jax.ShapeDtypeStruct((bsz, nh, dk, dv), F32),
                   jax.ShapeDtypeStruct((bsz, nh, dk, dv), F32)],
        scratch_shapes=[pltpu.VMEM((dk, dv), F32), pltpu.VMEM((t // chunk, dk, dv), BF16)],
        compiler_params=_params("parallel", "parallel"),
        name="ret_mix",
    )(dec_f, dec_b, q, k, v, gate, gn_g.reshape(1, nh * dv), s0_f, s0_b)


def _qkv_proj_kernel(h_ref, mod_ref, g_ref, w_ref, *rest, widths, q_scale, rope):
    if rope:
        cos_ref, sin_ref = rest[:2]
        outs = rest[2:]
    else:
        outs = rest
    shift, scale, _ = _mod_rows(mod_ref, 1)
    tm = h_ref.shape[1]
    sub = min(tm, FFN_SUBTILE)
    for r0 in range(0, tm, sub):
        rows = slice(r0, r0 + sub)
        xb = _ada(h_ref[0, rows, :], g_ref[...], shift, scale).astype(BF16)
        col = 0
        for idx, (o_ref, n) in enumerate(zip(outs, widths)):
            if o_ref is not None:
                y = _dot(xb, w_ref[:, col:col + n])
                for c0 in range(0, n, LANES):
                    yc = y[:, c0:c0 + LANES]
                    if rope and idx < 2:
                        lane = lax.broadcasted_iota(jnp.int32, yc.shape, 1)
                        partner = jnp.where((lane & 16) == 0, pltpu.roll(yc, LANES - 16, axis=1),
                                            pltpu.roll(yc, 16, axis=1))
                        yc = yc * cos_ref[rows, :] + partner * sin_ref[rows, :]
                    if idx == 0:
                        yc = yc * q_scale
                    o_ref[0, rows, c0:c0 + LANES] = yc.astype(BF16)
            col += n


def _qkv_proj(h, mod, g, w, widths, q_scale, cos=None, sin=None, want_q=True):
    bsz, t, d = h.shape
    tm = min(PROJ_TILE, t)
    rope = cos is not None
    row = lambda n: pl.BlockSpec((1, tm, n), lambda b, i: (b, i, 0))
    in_specs = [row(d), _mod_spec(mod, d), pl.BlockSpec((1, d), lambda b, i: (0, 0)), _resident(w.shape)]
    args = [h, mod, g.reshape(1, d), w]
    if rope:
        in_specs += [pl.BlockSpec((tm, LANES), lambda b, i: (i, 0))] * 2
        args += [cos, sin]
    keep = [want_q, True, True]
    out_widths = [n for n, kp in zip(widths, keep) if kp]

    def body(*refs):
        n_in = len(args)
        outs = list(refs[n_in:])
        full = [outs.pop(0) if kp else None for kp in keep]
        _qkv_proj_kernel(*refs[:n_in], *full, widths=widths, q_scale=q_scale, rope=rope)

    res = pl.pallas_call(
        body,
        grid=(bsz, t // tm),
        in_specs=in_specs,
        out_specs=[row(n) for n in out_widths],
        out_shape=[jax.ShapeDtypeStruct((bsz, t, n), BF16) for n in out_widths],
        compiler_params=_params("parallel", "parallel"),
        name="qkv_proj",
    )(*args)
    return res if want_q else [None] + list(res)


def _split_heads(x):
    lane = lax.broadcasted_iota(jnp.int32, x.shape, 1)
    zero = jnp.zeros_like(x)
    return jnp.concatenate([jnp.where(lane < 64, x, zero), jnp.where(lane >= 64, x, zero)], axis=0)


def _merge_heads(o):
    r = o.shape[0] // 2
    lane = lax.broadcasted_iota(jnp.int32, (r, o.shape[1]), 1)
    return jnp.where(lane < 64, o[:r], o[r:])


def _with_ones(v):
    return jnp.concatenate([v, jnp.ones_like(v)], axis=1)


def _stage_with_ones(dst_ref, src_ref):
    for c0 in range(0, src_ref.shape[1], XPOSE_CHUNK):
        n = min(XPOSE_CHUNK, src_ref.shape[1] - c0)
        dst_ref[c0:c0 + n, :] = _with_ones(src_ref[0, c0:c0 + n, :])


def _stage_transposed(dst_ref, src_ref, row0, width):
    for c0 in range(0, width, XPOSE_CHUNK):
        n = min(XPOSE_CHUNK, width - c0)
        dst_ref[:, c0:c0 + n] = src_ref[0, row0 + c0:row0 + c0 + n, :].T


def _nat_kernel(q_ref, k_ref, v_ref, kc_ref, vc_ref, bias_ref, o_ref, kt_scr, va_scr, snb_a, scx_a, snb_b, scx_b,
                *, rows):
    w = GRID_W
    kh = min(NAT_KH, rows)
    nk = kh * w
    t = rows * w
    _stage_transposed(kt_scr.at[0], k_ref, 0, t)
    _stage_transposed(kt_scr.at[1], k_ref, w, t - 2 * w)
    kct = kc_ref[0].T
    _stage_with_ones(va_scr, v_ref)
    vca = _with_ones(vc_ref[0])

    def window(r):
        r0 = jnp.clip(r - kh // 2, 0, rows - kh)
        return r0, pl.multiple_of(r * w, w)

    def scores(r, snb_ref, scx_ref):
        r0, qs = window(r)
        ty = r - r0
        par = r0 & 1
        kts = pl.multiple_of((r0 - par) * w, 2 * w)
        q2 = _split_heads(q_ref[0, pl.ds(qs, w), :])
        bias = jnp.concatenate(
            [jnp.concatenate([bias_ref[hd, 2 * m - ty + NAT_KH - 1] for m in range(kh // 2)], axis=1)
             for hd in range(2)], axis=0)
        snb_ref[...] = _dot(q2, kt_scr[par, :, pl.ds(kts, nk)]) + bias
        scx_ref[...] = _dot(q2, kct)

    def finish(r, snb_ref, scx_ref):
        r0, qs = window(r)
        ks = pl.multiple_of(r0 * w, w)
        s_nb, s_cx = snb_ref[...], scx_ref[...]
        m = jnp.maximum(jnp.max(s_nb, axis=-1, keepdims=True), jnp.max(s_cx, axis=-1, keepdims=True))
        p_nb = jnp.exp2(s_nb - m)
        p_cx = jnp.exp2(s_cx - m)
        oa = _dot(p_nb.astype(BF16), va_scr[pl.ds(ks, nk), :]) + _dot(p_cx.astype(BF16), vca)
        o_ref[0, pl.ds(qs, w), :] = _merge_heads(oa[:, :LANES] / oa[:, LANES:]).astype(BF16)

    scores(0, snb_a, scx_a)

    def body(i, carry):
        r = 2 * i
        scores(r + 1, snb_b, scx_b)
        finish(r, snb_a, scx_a)
        scores(jnp.minimum(r + 2, rows - 1), snb_a, scx_a)
        finish(r + 1, snb_b, scx_b)
        return carry

    lax.fori_loop(0, rows // 2, body, 0, unroll=16)


def _nat_attention(q, k, v, kc, vc, bias):
    bsz, t, d = q.shape
    l = kc.shape[1]
    rows = t // GRID_W
    hp = d // LANES
    lat = pl.BlockSpec((1, t, LANES), lambda p, b: (b, 0, p))
    ctx = pl.BlockSpec((1, l, LANES), lambda p, b: (b, 0, p))
    return pl.pallas_call(
        functools.partial(_nat_kernel, rows=rows),
        grid=(hp, bsz),
        in_specs=[lat, lat, lat, ctx, ctx,
                  pl.BlockSpec((2,) + bias.shape[1:], lambda p, b: (p, 0, 0, 0))],
        out_specs=lat,
        out_shape=jax.ShapeDtypeStruct((bsz, t, d), BF16),
        scratch_shapes=[pltpu.VMEM((2, LANES, t), BF16), pltpu.VMEM((t, 2 * LANES), BF16)]
        + [pltpu.VMEM((2 * GRID_W, n), F32) for n in (min(NAT_KH, rows) * GRID_W, l)] * 2,
        compiler_params=_params("parallel", "parallel"),
        name="nat_attention",
    )(q, k, v, kc, vc, bias)


def _nat_bias_kernel(x_ref, o_ref, *, tn):
    w = GRID_W
    sh = w.bit_length() - 1
    assert 1 << sh == w
    k2 = x_ref.shape[1]
    n = pl.program_id(0) * tn + lax.broadcasted_iota(jnp.int32, (1, tn), 1)
    c, a, cc = n >> (sh + 1), (n >> sh) & 1, n & (w - 1)
    j = lax.broadcasted_iota(jnp.int32, (k2, 1), 0)
    sel = jnp.where(((j >> (sh + 1)) == a) & ((j & (2 * w - 1)) == cc - c + w - 1), 1.0, 0.0).astype(BF16)
    hi, mid, lo = _split3(x_ref[...])
    vals = _dot(hi, sel) + _dot(mid, sel) + _dot(lo, sel)
    c0 = jnp.clip(c - NAT_KW // 2, 0, w - NAT_KW)
    o_ref[...] = jnp.where((cc >= c0) & (cc < c0 + NAT_KW), vals * LOG2E, NEG_INF)


def _nat_bias(rpb):
    w = GRID_W
    nh, nr, ncol = rpb.shape
    left = w - NAT_KW
    v = jnp.pad(rpb.astype(F32), ((0, 0), (0, 0), (left, 2 * w - ncol - left)))
    pairs = jnp.concatenate([v[:, :-1], v[:, 1:]], axis=-1).reshape(nh * (nr - 1), 4 * w)
    tn = 16 * LANES
    out = pl.pallas_call(
        functools.partial(_nat_bias_kernel, tn=tn),
        grid=(w * 2 * w // tn,),
        in_specs=[pl.BlockSpec(pairs.shape, lambda i: (0, 0))],
        out_specs=pl.BlockSpec((pairs.shape[0], tn), lambda i: (0, i)),
        out_shape=jax.ShapeDtypeStruct((pairs.shape[0], w * 2 * w), F32),
        compiler_params=_params("parallel"),
        name="nat_bias",
    )(pairs)
    return out.reshape(nh, nr - 1, w, 2 * w)


def _ctx_attn_kernel(q_ref, k_ref, v_ref, o_ref):
    q2 = _split_heads(q_ref[0])
    s = _dot(q2, k_ref[0].T)
    p = jnp.exp2(s - jnp.max(s, axis=-1, keepdims=True))
    l = jnp.sum(p, axis=-1, keepdims=True)
    o_ref[0] = _merge_heads(_dot(p.astype(BF16), v_ref[0]) / l).astype(BF16)


def _ctx_attention(q, k, v):
    bsz, l, d = q.shape
    blk = pl.BlockSpec((1, l, LANES), lambda b, p: (b, 0, p))
    return pl.pallas_call(
        _ctx_attn_kernel,
        grid=(bsz, d // LANES),
        in_specs=[blk, blk, blk],
        out_specs=blk,
        out_shape=jax.ShapeDtypeStruct((bsz, l, d), BF16),
        compiler_params=_params("parallel", "parallel"),
        name="ctx_attention",
    )(q, k, v)


POOL_HALO = SUBLANES


def _split3(x):
    hi = x.astype(BF16)
    r1 = x - hi.astype(F32)
    mid = r1.astype(BF16)
    return hi, mid, (r1 - mid.astype(F32)).astype(BF16)


def _pool_kernel(h_ref, prev_ref, next_ref, mod_ref, g_ref, pw_ref, ps_ref, o_ref, band_scr, *, tm, t_total):
    i = pl.program_id(1)
    nt = pl.num_programs(1)
    hl = POOL_HALO
    x = h_ref[0]
    d = x.shape[-1]
    gd = d // len(POOL_WINDOWS)
    shift, scale, gate = _mod_rows(mod_ref, 1)

    @pl.when((pl.program_id(0) == 0) & (i == 0))
    def _():
        off = lax.broadcasted_iota(jnp.int32, (tm, tm), 1) - lax.broadcasted_iota(jnp.int32, (tm, tm), 0)
        for gi, win in enumerate(POOL_WINDOWS):
            band_scr[gi] = jnp.where((off >= -(win // 2)) & (off < win // 2), 1.0, 0.0).astype(BF16)

    xe = _ada(jnp.concatenate([prev_ref[0], x, next_ref[0]], axis=0), g_ref[...], shift, scale)
    xn = xe[hl:hl + tm]
    halo = jnp.concatenate([jnp.where(i > 0, xe[:hl], 0.0), jnp.where(i < nt - 1, xe[hl + tm:], 0.0)], axis=0)
    x3 = _split3(xn)
    h3 = _split3(halo)
    er = lax.broadcasted_iota(jnp.int32, (2 * hl, 2 * hl), 0)
    hu = lax.broadcasted_iota(jnp.int32, (2 * hl, 2 * hl), 1)
    dist = jnp.where(hu < hl, hu - hl - er, hu - er + hl)
    same_side = (er < hl) == (hu < hl)
    trow = i * tm + lax.broadcasted_iota(jnp.int32, (tm, 1), 0)
    parts = []
    for gi, win in enumerate(POOL_WINDOWS):
        half = win // 2
        cols = slice(gi * gd, (gi + 1) * gd)
        band = band_scr[gi]
        edge = jnp.where(same_side & (dist >= -half) & (dist < half), 1.0, 0.0).astype(BF16)
        tot = _dot(band, x3[0][:, cols]) + _dot(band, x3[1][:, cols]) + _dot(band, x3[2][:, cols])
        fix = _dot(edge, h3[0][:, cols]) + _dot(edge, h3[1][:, cols]) + _dot(edge, h3[2][:, cols])
        tot = jnp.concatenate([tot[:hl] + fix[:hl], tot[hl:tm - hl], tot[tm - hl:] + fix[hl:]], axis=0)
        cnt = (jnp.minimum(trow + half, t_total) - jnp.maximum(trow - half, 0)).astype(F32)
        pooled = (tot / cnt - xn[:, cols]).astype(BF16)
        parts.append(_dot(pooled, pw_ref[gi]))
    y = jnp.concatenate(parts, axis=-1) * ps_ref[...]
    o_ref[0] = x + gate * y


def _pool(h, mod, g, pw, ps):
    bsz, t, d = h.shape
    tm = TOKEN_TILE
    per = tm // POOL_HALO
    last = t // POOL_HALO - 1
    return pl.pallas_call(
        functools.partial(_pool_kernel, tm=tm, t_total=t),
        grid=(bsz, t // tm),
        in_specs=[pl.BlockSpec((1, tm, d), lambda b, i: (b, i, 0)),
                  pl.BlockSpec((1, POOL_HALO, d), lambda b, i: (b, jnp.maximum(i * per - 1, 0), 0)),
                  pl.BlockSpec((1, POOL_HALO, d), lambda b, i: (b, jnp.minimum((i + 1) * per, last), 0)),
                  _mod_spec(mod, d),
                  pl.BlockSpec((1, d), lambda b, i: (0, 0)),
                  _resident(pw.shape),
                  pl.BlockSpec((1, d), lambda b, i: (0, 0))],
        out_specs=pl.BlockSpec((1, tm, d), lambda b, i: (b, i, 0)),
        out_shape=jax.ShapeDtypeStruct((bsz, t, d), F32),
        scratch_shapes=[pltpu.VMEM((len(POOL_WINDOWS), tm, tm), BF16)],
        compiler_params=_params("arbitrary", "arbitrary"),
        name="pool",
    )(h, h, h, mod, g.reshape(1, d), pw, ps.reshape(1, d))


def _swa_kernel(sink_ref, q_ref, k_ref, v_ref, kc_ref, vc_ref, o_ref, kt_scr, va_scr, sloc_a, scx_a, sloc_b, scx_b,
                *, t_total):
    kv = pl.program_id(1)
    blk = SWA_BLOCK
    nb = t_total // blk
    grp = SWA_Q_HEADS // SWA_KV_HEADS
    nrow = grp * blk
    _stage_transposed(kt_scr, k_ref, 0, t_total)
    kct = kc_ref[0].T
    _stage_with_ones(va_scr, v_ref)
    vca = _with_ones(vc_ref[0])
    row = lax.broadcasted_iota(jnp.int32, (nrow, 1), 0)
    sink = jnp.zeros((nrow, 1), F32)
    for gi in range(grp):
        sink = jnp.where((row >= gi * blk) & (row < (gi + 1) * blk), sink_ref[kv * grp + gi] * LOG2E, sink)
    qi = lax.broadcasted_iota(jnp.int32, (nrow, blk), 0) & (blk - 1)
    kj = lax.broadcasted_iota(jnp.int32, (nrow, blk), 1)
    open_blk = jnp.zeros((nrow, blk), F32)
    prev_blk = jnp.where(kj >= qi, 0.0, NEG_INF)
    next_blk = jnp.where(kj <= qi, 0.0, NEG_INF)

    shut_blk = jnp.full((nrow, blk), NEG_INF, F32)
    span = 3 * blk
    mask_first = jnp.concatenate([open_blk, next_blk, shut_blk], axis=1)
    mask_mid = jnp.concatenate([prev_blk, open_blk, next_blk], axis=1)
    mask_last = jnp.concatenate([shut_blk, prev_blk, open_blk], axis=1)

    def offsets(n):
        qs = pl.multiple_of(n * blk, blk)
        return qs, pl.multiple_of(jnp.clip(qs - blk, 0, t_total - span), blk)

    def scores(n, mask, sloc_ref, scx_ref):
        qs, ks = offsets(n)
        qb = q_ref[0, pl.ds(qs, blk), :]
        q4 = jnp.concatenate([_split_heads(qb[:, :LANES]), _split_heads(qb[:, LANES:])], axis=0)
        sloc_ref[...] = _dot(q4, kt_scr[:, pl.ds(ks, span)]) + mask
        scx_ref[...] = _dot(q4, kct)

    def finish(n, sloc_ref, scx_ref):
        qs, ks = offsets(n)
        s_loc, s_cx = sloc_ref[...], scx_ref[...]
        m = jnp.maximum(jnp.maximum(jnp.max(s_loc, axis=-1, keepdims=True),
                                    jnp.max(s_cx, axis=-1, keepdims=True)), sink)
        p_loc = jnp.exp2(s_loc - m)
        p_cx = jnp.exp2(s_cx - m)
        oa = _dot(p_loc.astype(BF16), va_scr[pl.ds(ks, span), :]) + _dot(p_cx.astype(BF16), vca)
        o = oa[:, :LANES] / (oa[:, LANES:] + jnp.exp2(sink - m))
        out = jnp.concatenate([_merge_heads(o[:2 * blk]), _merge_heads(o[2 * blk:])], axis=-1)
        o_ref[0, pl.ds(qs, blk), :] = out.astype(BF16)

    slot_a, slot_b = (sloc_a, scx_a), (sloc_b, scx_b)
    scores(0, mask_first, *slot_a)
    scores(1, mask_mid, *slot_b)
    finish(0, *slot_a)

    def body(i, carry):
        n = 1 + 2 * i
        scores(n + 1, mask_mid, *slot_a)
        finish(n, *slot_b)
        scores(n + 2, mask_mid, *slot_b)
        finish(n + 1, *slot_a)
        return carry

    lax.fori_loop(0, (nb - 4) // 2, body, 0, unroll=7)
    scores(nb - 2, mask_mid, *slot_a)
    finish(nb - 3, *slot_b)
    scores(nb - 1, mask_last, *slot_b)
    finish(nb - 2, *slot_a)
    finish(nb - 1, *slot_b)


def _swa_attention(sink, q, k, v, kc, vc):
    bsz, t, d = q.shape
    l = kc.shape[1]
    nb = t // SWA_BLOCK
    assert nb >= 4 and nb % 2 == 0 and SWA_WINDOW == SWA_BLOCK
    qw = d // SWA_KV_HEADS
    grid_spec = pltpu.PrefetchScalarGridSpec(
        num_scalar_prefetch=1,
        grid=(bsz, SWA_KV_HEADS),
        in_specs=[pl.BlockSpec((1, t, qw), lambda b, h, s: (b, 0, h)),
                  pl.BlockSpec((1, t, LANES), lambda b, h, s: (b, 0, h)),
                  pl.BlockSpec((1, t, LANES), lambda b, h, s: (b, 0, h)),
                  pl.BlockSpec((1, l, LANES), lambda b, h, s: (b, 0, h)),
                  pl.BlockSpec((1, l, LANES), lambda b, h, s: (b, 0, h))],
        out_specs=pl.BlockSpec((1, t, qw), lambda b, h, s: (b, 0, h)),
        scratch_shapes=[pltpu.VMEM((LANES, t), BF16), pltpu.VMEM((t, 2 * LANES), BF16)]
        + [pltpu.VMEM((SWA_Q_HEADS // SWA_KV_HEADS * SWA_BLOCK, n), F32) for n in (3 * SWA_BLOCK, l)] * 2,
    )
    return pl.pallas_call(
        functools.partial(_swa_kernel, t_total=t),
        grid_spec=grid_spec,
        out_shape=jax.ShapeDtypeStruct((bsz, t, d), BF16),
        compiler_params=_params("parallel", "parallel"),
        name="swa_attention",
    )(sink, q, k, v, kc, vc)


def _rope_angles(positions, dim):
    seg = dim // len(positions)
    inv = ROPE_BASE ** (-np.arange(0, seg, 2, dtype=np.float64) / seg)
    return np.concatenate([np.tile(p.astype(np.float64)[:, None] * inv, (1, 2)) for p in positions], axis=-1)


def _table(x):
    return jnp.asarray(x.astype(np.float32))


def _dup_heads(w, heads, dh):
    d = w.shape[0]
    return jnp.broadcast_to(w.reshape(d, heads, 1, dh), (d, heads, 2, dh)).reshape(d, heads * 2 * dh)


def kernel(x, c, ctx, c_ctx, w_mod, b_mod, norm_g, ffn_w_in, ffn_w_out, ret_w_in, ret_w_out, ret_gn_g, ret_decay_f, ret_decay_b, nat_w_qkv, nat_w_o, nat_rpb, pool_w, pool_scale, swa_w_qkv, swa_w_o, swa_sink, final_norm_g):
    bsz, t, d = x.shape
    depth = w_mod.shape[0]
    cc = jnp.concatenate([c, c_ctx[None], jnp.zeros((SUBLANES - bsz - 1, d), F32)], axis=0)
    mods = _modulation(cc, w_mod, b_mod)
    h, hc = x, ctx
    for i in range(depth):
        kind, occ = i % N_MIXERS, i // N_MIXERS
        last = i == depth - 1
        ctx_live = (not last) or kind != 2
        ml = mods[i, :bsz].reshape(bsz, 9, d)
        mc = mods[i, bsz:bsz + 1].reshape(1, 9, d)
        if ctx_live:
            hc, w_in, w_out = _ffn_ctx(hc, mc, norm_g[i, 0], ffn_w_in, ffn_w_out, i, 0)
        else:
            w_in, w_out = _ffn_weights_bf16(ffn_w_in, ffn_w_out, i, 0)
        h = _ffn(h, ml, norm_g[i, 0], w_in, w_out, 0)
        g1 = norm_g[i, 1]
        yl = yc = None
        if kind == 0:
            assert not last, "retention as the last layer is not wired up"
            w = ret_w_in[occ].astype(BF16)
            wo = ret_w_out[occ].astype(BF16)
            ang = _rope_angles([np.arange(t)], RET_QK_DIM)[:, :RET_QK_DIM // 2]
            dec_f = jnp.broadcast_to(ret_decay_f[occ].astype(F32)[:, None, None], (RET_HEADS, 1, LANES))
            dec_b = jnp.broadcast_to(ret_decay_b[occ].astype(F32)[:, None, None], (RET_HEADS, 1, LANES))
            qc, kc, vc, gc = _ret_proj(hc, mc, g1, w)
            zeros = jnp.zeros((bsz, RET_HEADS, RET_QK_DIM, RET_V_DIM), F32)
            oc, s_f, s_b = _ret_mix(dec_f, dec_b, qc, kc, vc, gc, ret_gn_g[occ], zeros, zeros)
            ql, kl, vl, gl = _ret_proj(h, ml, g1, w, _table(np.cos(ang)), _table(np.sin(ang)))
            ol, _, _ = _ret_mix(dec_f, dec_b, ql, kl, vl, gl, ret_gn_g[occ], s_f, s_b)
            yl, yc = (ol, wo), (oc, wo)
        elif kind == 1:
            w = nat_w_qkv[occ].astype(BF16)
            wo = nat_w_o[occ].astype(BF16)
            widths = (d, d, d)
            qc, kc, vc = _qkv_proj(hc, mc, g1, w, widths, NAT_HEAD_DIM ** -0.5 * LOG2E, want_q=not last)
            ql, kl, vl = _qkv_proj(h, ml, g1, w, widths, NAT_HEAD_DIM ** -0.5 * LOG2E)
            yl = (_nat_attention(ql, kl, vl, kc, vc, _nat_bias(nat_rpb[occ])), wo)
            if not last:
                yc = (_ctx_attention(qc, kc, vc), wo)
        elif kind == 2:
            pw = pool_w[occ].astype(BF16)
            h_new = _pool(h, ml, g1, pw, pool_scale[occ])
            if not last:
                hc = _pool(hc, mc, g1, pw, pool_scale[occ])
            h = h_new
        else:
            assert last, "windowed attention with live context outputs is not wired up"
            nq = SWA_Q_HEADS * SWA_HEAD_DIM
            nkv = SWA_KV_HEADS * SWA_HEAD_DIM
            wq = swa_w_qkv[occ][:, :nq]
            wk = _dup_heads(swa_w_qkv[occ][:, nq:nq + nkv], SWA_KV_HEADS, SWA_HEAD_DIM)
            wv = _dup_heads(swa_w_qkv[occ][:, nq + nkv:], SWA_KV_HEADS, SWA_HEAD_DIM)
            w = jnp.concatenate([wq, wk, wv], axis=-1).astype(BF16)
            wo = swa_w_o[occ].astype(BF16)
            widths = (nq, 2 * nkv, 2 * nkv)
            tt = np.arange(t)
            ang = _rope_angles([tt // GRID_W, tt % GRID_W], SWA_HEAD_DIM)
            ang = np.tile(ang, (1, LANES // SWA_HEAD_DIM))
            lane = np.arange(LANES)
            sin = np.where(lane % 32 < 16, -np.sin(ang), np.sin(ang))
            _, kc, vc = _qkv_proj(hc, mc, g1, w, widths, SWA_HEAD_DIM ** -0.5 * LOG2E, want_q=False)
            ql, kl, vl = _qkv_proj(h, ml, g1, w, widths, SWA_HEAD_DIM ** -0.5 * LOG2E, _table(np.cos(ang)), _table(sin))
            yl = (_swa_attention(swa_sink[occ].astype(F32), ql, kl, vl, kc, vc), wo)
        if not last:
            hc, w_in, w_out = _ffn_ctx(hc, mc, norm_g[i, 2], ffn_w_in, ffn_w_out, i, 1, yc)
        else:
            w_in, w_out = _ffn_weights_bf16(ffn_w_in, ffn_w_out, i, 1)
        h = _ffn(h, ml, norm_g[i, 2], w_in, w_out, 1, yl, final_norm_g if last else None)
    return h
```

```python
import functools
import math

import jax
import jax.numpy as jnp
import numpy as np
from jax import lax
from jax.experimental import pallas as pl
from jax.experimental.pallas import tpu as pltpu

F32 = jnp.float32
BF16 = jnp.bfloat16

EPS = 1e-6
NEG_INF = -1e30
LOG2E = math.log2(math.e)
ROPE_BASE = 10000.0
GRID_W = 64
N_MIXERS = 4
FFN_HIDDEN = 2816
RET_HEADS = 4
RET_QK_DIM = 256
RET_V_DIM = 512
NAT_HEADS = 16
NAT_HEAD_DIM = 64
NAT_KH = 8
NAT_KW = 16
POOL_WINDOWS = (2, 4, 8, 16)
SWA_Q_HEADS = 16
SWA_KV_HEADS = 4
SWA_HEAD_DIM = 64
SWA_WINDOW = 128
SWA_BLOCK = 128

LANES = 128
SUBLANES = 8
VMEM_LIMIT = 56 * 1024 * 1024
POOL_TILE = 512
POOL_SUBTILE = 256
PROJ_TILE = 512
FFN_TILE = 1024
FFN_SUBTILE = 256
CTX_FFN_CHUNK = 256
RET_CHUNK = 256
XPOSE_CHUNK = 512


def _params(*sem):
    return pltpu.CompilerParams(dimension_semantics=sem, vmem_limit_bytes=VMEM_LIMIT)


def _resident(shape):
    nd = len(shape)
    return pl.BlockSpec(shape, lambda *_: (0,) * nd, pipeline_mode=pl.Buffered(1))


def _silu(x):
    return x * jax.nn.sigmoid(x)


def _ada(x, g, shift, scale):
    var = jnp.mean(x * x, axis=-1, keepdims=True)
    y = x * lax.rsqrt(var + EPS) * g
    return y * (1.0 + scale) + shift


def _mod_rows(mod_ref, j):
    return (mod_ref[0, 3 * j:3 * j + 1, :], mod_ref[0, 3 * j + 1:3 * j + 2, :],
            mod_ref[0, 3 * j + 2:3 * j + 3, :])


def _mod_spec(mod, d):
    if mod.shape[0] == 1:
        return pl.BlockSpec((1, 9, d), lambda b, t: (0, 0, 0))
    return pl.BlockSpec((1, 9, d), lambda b, t: (b, 0, 0))


def _dot(a, b):
    return jnp.dot(a, b, preferred_element_type=F32)


def _dot_nt(a, b):
    return lax.dot_general(a, b, (((1,), (1,)), ((), ())), preferred_element_type=F32)


def _dot_tn(a, b):
    return lax.dot_general(a, b, (((0,), (0,)), ((), ())), preferred_element_type=F32)


def _mod_kernel(c_ref, w_ref, b_ref, o_ref):
    s = _silu(c_ref[...]).astype(BF16)
    o_ref[0] = _dot(s, w_ref[0].astype(BF16)) + b_ref[0]


def _modulation(cc, w_mod, b_mod):
    depth, d, n = w_mod.shape
    tn = 2304
    return pl.pallas_call(
        _mod_kernel,
        grid=(depth, n // tn),
        in_specs=[pl.BlockSpec((SUBLANES, d), lambda l, j: (0, 0)),
                  pl.BlockSpec((1, d, tn), lambda l, j: (l, 0, j)),
                  pl.BlockSpec((1, 1, tn), lambda l, j: (l, 0, j))],
        out_specs=pl.BlockSpec((1, SUBLANES, tn), lambda l, j: (l, 0, j)),
        out_shape=jax.ShapeDtypeStruct((depth, SUBLANES, n), F32),
        compiler_params=_params("arbitrary", "arbitrary"),
        name="modulation",
    )(cc, w_mod, b_mod.reshape(depth, 1, n))


def _ffn_kernel(h_ref, mod_ref, g_ref, win_ref, wout_ref, *rest, j, mixer_out, final):
    o_ref = rest[-1]
    shift, scale, gate = _mod_rows(mod_ref, j)
    tm = h_ref.shape[1]
    sub = min(tm, FFN_SUBTILE)
    for r0 in range(0, tm, sub):
        x = h_ref[0, r0:r0 + sub, :]
        if mixer_out:
            y_ref, wo_ref = rest[:2]
            x = x + mod_ref[0, 5:6, :] * _dot(y_ref[0, r0:r0 + sub, :], wo_ref[...])
        xb = _ada(x, g_ref[...], shift, scale).astype(BF16)
        if len(win_ref.shape) == 3:
            wg, wv = win_ref[0], win_ref[1]
        else:
            f = win_ref.shape[1] // 2
            wg, wv = win_ref[:, :f], win_ref[:, f:]
        hid = (_silu(_dot(xb, wg)) * _dot(xb, wv)).astype(BF16)
        out = x + (0.5 * gate) * _dot(hid, wout_ref[...])
        if final:
            var = jnp.mean(out * out, axis=-1, keepdims=True)
            out = out * lax.rsqrt(var + EPS) * rest[-2][...]
        o_ref[0, r0:r0 + sub, :] = out


def _ffn(h, mod, g, w_in, w_out, half, mixer_out=None, final_g=None):
    bsz, t, d = h.shape
    tm = min(FFN_TILE, t)
    row = lambda n: pl.BlockSpec((1, tm, n), lambda b, i: (b, i, 0))
    vec = pl.BlockSpec((1, d), lambda b, i: (0, 0))
    in_specs = [row(d), _mod_spec(mod, d), vec, _resident(w_in.shape), _resident(w_out.shape)]
    args = [h, mod, g.reshape(1, d), w_in, w_out]
    if mixer_out is not None:
        y, wo = mixer_out
        in_specs += [row(y.shape[-1]), _resident(wo.shape)]
        args += [y, wo]
    if final_g is not None:
        in_specs.append(vec)
        args.append(final_g.reshape(1, d))
    return pl.pallas_call(
        functools.partial(_ffn_kernel, j=2 * half, mixer_out=mixer_out is not None, final=final_g is not None),
        grid=(bsz, t // tm),
        in_specs=in_specs,
        out_specs=row(d),
        out_shape=jax.ShapeDtypeStruct((bsz, t, d), F32),
        compiler_params=_params("parallel", "parallel"),
        name="ffn",
    )(*args)


def _ffn_ctx_kernel(h_ref, mod_ref, g_ref, wa_ref, wb_ref, wo_ref, *rest, j, mixer_out):
    o_ref, win_bf_ref, wout_bf_ref, x_scr, xb_scr, acc_scr = rest[-6:]
    s = pl.program_id(0)
    shift, scale, gate = _mod_rows(mod_ref, j)

    @pl.when(s == 0)
    def _():
        x = h_ref[...]
        if mixer_out:
            y_ref, wmix_ref = rest[:2]
            x = x + mod_ref[0, 5:6, :] * _dot(y_ref[...], wmix_ref[...])
        x_scr[...] = x
        xb_scr[...] = _ada(x, g_ref[...], shift, scale).astype(BF16)
        acc_scr[...] = jnp.zeros_like(acc_scr)

    wa, wb, wo = wa_ref[0, 0].astype(BF16), wb_ref[0, 0].astype(BF16), wo_ref[0, 0].astype(BF16)
    win_bf_ref[0] = wa
    win_bf_ref[1] = wb
    wout_bf_ref[...] = wo
    xb = xb_scr[...]
    hid = (_silu(_dot(xb, wa)) * _dot(xb, wb)).astype(BF16)
    acc_scr[...] += _dot(hid, wo)

    @pl.when(s == pl.num_programs(0) - 1)
    def _():
        o_ref[...] = x_scr[...] + (0.5 * gate) * acc_scr[...]


def _ffn_ctx(hc, mod, g, w_in, w_out, layer, half, mixer_out=None):
    bsz, l, d = hc.shape
    f = w_out.shape[2]
    fc = CTX_FFN_CHUNK
    nf = f // fc
    n = bsz * l
    const = lambda shape: pl.BlockSpec(shape, lambda s: (0,) * len(shape))
    in_specs = [const((n, d)), const((1, 9, d)), const((1, d)),
                pl.BlockSpec((1, 1, d, fc), lambda s: (layer, half, 0, s)),
                pl.BlockSpec((1, 1, d, fc), lambda s: (layer, half, 0, nf + s)),
                pl.BlockSpec((1, 1, fc, d), lambda s: (layer, half, s, 0))]
    args = [hc.reshape(n, d), mod, g.reshape(1, d), w_in, w_in, w_out]
    if mixer_out is not None:
        y, wmix = mixer_out
        in_specs += [const((n, y.shape[-1])), _resident(wmix.shape)]
        args += [y.reshape(n, y.shape[-1]), wmix]
    out, w_in_bf, w_out_bf = pl.pallas_call(
        functools.partial(_ffn_ctx_kernel, j=2 * half, mixer_out=mixer_out is not None),
        grid=(nf,),
        in_specs=in_specs,
        out_specs=[const((n, d)),
                   pl.BlockSpec((2, d, fc), lambda s: (0, 0, s)),
                   pl.BlockSpec((fc, d), lambda s: (s, 0))],
        out_shape=[jax.ShapeDtypeStruct((n, d), F32),
                   jax.ShapeDtypeStruct((2, d, f), BF16),
                   jax.ShapeDtypeStruct((f, d), BF16)],
        scratch_shapes=[pltpu.VMEM((n, d), F32), pltpu.VMEM((n, d), BF16), pltpu.VMEM((n, d), F32)],
        compiler_params=_params("arbitrary"),
        name="ffn_ctx",
    )(*args)
    return out.reshape(bsz, l, d), w_in_bf, w_out_bf


def _ffn_weights_bf16(w_in, w_out, layer, half):
    return w_in[layer, half].astype(BF16), w_out[layer, half].astype(BF16)


def _ret_proj_kernel(h_ref, mod_ref, g_ref, w_ref, *rest, rope):
    if rope:
        cos_ref, sin_ref, q_ref, k_ref, v_ref, gate_ref = rest
    else:
        q_ref, k_ref, v_ref, gate_ref = rest
    shift, scale, _ = _mod_rows(mod_ref, 1)
    dk, nh = RET_QK_DIM, RET_HEADS
    half = dk // 2
    v0 = 2 * nh * dk
    nv = nh * RET_V_DIM
    tm = h_ref.shape[1]
    sub = min(tm, FFN_SUBTILE)
    for r0 in range(0, tm, sub):
        rows = slice(r0, r0 + sub)
        xb = _ada(h_ref[0, rows, :], g_ref[...], shift, scale).astype(BF16)

        def rot(x):
            if not rope:
                return x.astype(BF16)
            c, s = cos_ref[rows, :], sin_ref[rows, :]
            x1, x2 = x[:, :half], x[:, half:]
            return jnp.concatenate([x1 * c - x2 * s, x2 * c + x1 * s], axis=-1).astype(BF16)

        for hd in range(nh):
            q_ref[0, rows, hd * dk:(hd + 1) * dk] = rot(_dot(xb, w_ref[:, hd * dk:(hd + 1) * dk]))
            kcol = nh * dk + hd * dk
            k_ref[0, rows, hd * dk:(hd + 1) * dk] = rot(_dot(xb, w_ref[:, kcol:kcol + dk]) * dk ** -0.5)
        v_ref[0, rows, :] = _dot(xb, w_ref[:, v0:v0 + nv]).astype(BF16)
        gate_ref[0, rows, :] = _silu(_dot(xb, w_ref[:, v0 + nv:v0 + 2 * nv])).astype(BF16)


def _ret_proj(h, mod, g, w, cos=None, sin=None):
    bsz, t, d = h.shape
    tm = min(PROJ_TILE, t)
    rope = cos is not None
    nqk = RET_HEADS * RET_QK_DIM
    nv = RET_HEADS * RET_V_DIM
    row = lambda n: pl.BlockSpec((1, tm, n), lambda b, i: (b, i, 0))
    in_specs = [row(d), _mod_spec(mod, d), pl.BlockSpec((1, d), lambda b, i: (0, 0)), _resident(w.shape)]
    args = [h, mod, g.reshape(1, d), w]
    if rope:
        in_specs += [pl.BlockSpec((tm, RET_QK_DIM // 2), lambda b, i: (i, 0))] * 2
        args += [cos, sin]
    return pl.pallas_call(
        functools.partial(_ret_proj_kernel, rope=rope),
        grid=(bsz, t // tm),
        in_specs=in_specs,
        out_specs=[row(nqk), row(nqk), row(nv), row(nv)],
        out_shape=[jax.ShapeDtypeStruct((bsz, t, nqk), BF16), jax.ShapeDtypeStruct((bsz, t, nqk), BF16),
                   jax.ShapeDtypeStruct((bsz, t, nv), BF16), jax.ShapeDtypeStruct((bsz, t, nv), BF16)],
        compiler_params=_params("parallel", "parallel"),
        name="ret_proj",
    )(*args)


def _log_sigmoid(x):
    return jnp.minimum(x, 0.0) - jnp.log(1.0 + jnp.exp(-jnp.abs(x)))


def _ret_mix_kernel(decf_ref, decb_ref, q_ref, k_ref, v_ref, g_ref, gn_ref, s0f_ref, s0b_ref,
                    y_ref, sf_ref, sb_ref, s_scr, sb_scr, *, chunk):
    c = chunk
    nc = q_ref.shape[1] // c
    lgf = _log_sigmoid(decf_ref[0])[:, :1]
    lgb = _log_sigmoid(decb_ref[0])[:, :1]
    dist = lax.broadcasted_iota(jnp.int32, (c, c), 0) - lax.broadcasted_iota(jnp.int32, (c, c), 1)
    decay = jnp.where(dist >= 0,
                      jnp.exp(jnp.maximum(dist, 0).astype(F32) * lgf),
                      jnp.exp(jnp.maximum(-dist, 0).astype(F32) * lgb))
    pos = lax.broadcasted_iota(jnp.int32, (c, 1), 0).astype(F32)
    qd_f, kd_f, cd_f = jnp.exp((pos + 1.0) * lgf), jnp.exp((c - 1.0 - pos) * lgf), jnp.exp(c * lgf)
    qd_b, kd_b, cd_b = jnp.exp((c - pos) * lgb), jnp.exp(pos * lgb), jnp.exp(c * lgb)

    def rows(n):
        return pl.ds(pl.multiple_of(n * c, c), c)

    s_scr[...] = s0b_ref[0, 0]

    def bwd(i, carry):
        n = nc - 1 - i
        s = s_scr[...]
        sb_scr[n] = s.astype(BF16)
        kd = (k_ref[0, rows(n), :].astype(F32) * kd_b).astype(BF16)
        s_scr[...] = s * cd_b + _dot_tn(kd, v_ref[0, rows(n), :])
        return carry

    lax.fori_loop(0, nc, bwd, 0, unroll=min(8, nc))
    sb_ref[0, 0] = s_scr[...]
    s_scr[...] = s0f_ref[0, 0]

    def fwd(n, carry):
        q, k, v = q_ref[0, rows(n), :], k_ref[0, rows(n), :], v_ref[0, rows(n), :]
        qf, kf = q.astype(F32), k.astype(F32)
        s = s_scr[...]
        att = (_dot(q, k.T) * decay).astype(BF16)
        o = (_dot(att, v) + _dot((qf * qd_f).astype(BF16), s.astype(BF16))
             + _dot((qf * qd_b).astype(BF16), sb_scr[n]))
        mu = jnp.mean(o, axis=-1, keepdims=True)
        ctr = o - mu
        var = jnp.mean(ctr * ctr, axis=-1, keepdims=True)
        on = ctr * lax.rsqrt(var + EPS) * gn_ref[...]
        y_ref[0, rows(n), :] = (g_ref[0, rows(n), :].astype(F32) * on).astype(BF16)
        s_scr[...] = s * cd_f + _dot_tn((kf * kd_f).astype(BF16), v)
        return carry

    lax.fori_loop(0, nc, fwd, 0, unroll=min(8, nc))
    sf_ref[0, 0] = s_scr[...]


def _ret_mix(dec_f, dec_b, q, k, v, gate, gn_g, s0_f, s0_b):
    bsz, t, _ = q.shape
    nh, dk, dv = RET_HEADS, RET_QK_DIM, RET_V_DIM
    chunk = min(RET_CHUNK, t)
    seq = lambda n: pl.BlockSpec((1, t, n), lambda b, h: (b, 0, h))
    dec = pl.BlockSpec((1, 1, LANES), lambda b, h: (h, 0, 0))
    state = pl.BlockSpec((1, 1, dk, dv), lambda b, h: (b, h, 0, 0))
    return pl.pallas_call(
        functools.partial(_ret_mix_kernel, chunk=chunk),
        grid=(bsz, nh),
        in_specs=[dec, dec, seq(dk), seq(dk), seq(dv), seq(dv),
                  pl.BlockSpec((1, dv), lambda b, h: (0, h)), state, state],
        out_specs=[seq(dv), state, state],
        out_shape=[jax.ShapeDtypeStruct((bsz, t, nh * dv), BF16),
                   jax.ShapeDtypeStruct((bsz, nh, dk, dv), F32),
                   jax.ShapeDtypeStruct((bsz, nh, dk, dv), F32)],
        scratch_shapes=[pltpu.VMEM((dk, dv), F32), pltpu.VMEM((t // chunk, dk, dv), BF16)],
        compiler_params=_params("parallel", "parallel"),
        name="ret_mix",
    )(dec_f, dec_b, q, k, v, gate, gn_g.reshape(1, nh * dv), s0_f, s0_b)


def _qkv_proj_kernel(h_ref, mod_ref, g_ref, w_ref, *rest, widths, q_scale, rope):
    if rope:
        cos_ref, sin_ref = rest[:2]
        outs = rest[2:]
    else:
        outs = rest
    shift, scale, _ = _mod_rows(mod_ref, 1)
    tm = h_ref.shape[1]
    sub = min(tm, FFN_SUBTILE)
    for r0 in range(0, tm, sub):
        rows = slice(r0, r0 + sub)
        xb = _ada(h_ref[0, rows, :], g_ref[...], shift, scale).astype(BF16)
        col = 0
        for idx, (o_ref, n) in enumerate(zip(outs, widths)):
            if o_ref is not None:
                y = _dot(xb, w_ref[:, col:col + n])
                for c0 in range(0, n, LANES):
                    yc = y[:, c0:c0 + LANES]
                    if rope and idx < 2:
                        lane = lax.broadcasted_iota(jnp.int32, yc.shape, 1)
                        partner = jnp.where((lane & 16) == 0, pltpu.roll(yc, LANES - 16, axis=1),
                                            pltpu.roll(yc, 16, axis=1))
                        yc = yc * cos_ref[rows, :] + partner * sin_ref[rows, :]
                    if idx == 0:
                        yc = yc * q_scale
                    o_ref[0, rows, c0:c0 + LANES] = yc.astype(BF16)
            col += n


def _qkv_proj(h, mod, g, w, widths, q_scale, cos=None, sin=None, want_q=True):
    bsz, t, d = h.shape
    tm = min(PROJ_TILE, t)
    rope = cos is not None
    row = lambda n: pl.BlockSpec((1, tm, n), lambda b, i: (b, i, 0))
    in_specs = [row(d), _mod_spec(mod, d), pl.BlockSpec((1, d), lambda b, i: (0, 0)), _resident(w.shape)]
    args = [h, mod, g.reshape(1, d), w]
    if rope:
        in_specs += [pl.BlockSpec((tm, LANES), lambda b, i: (i, 0))] * 2
        args += [cos, sin]
    keep = [want_q, True, True]
    out_widths = [n for n, kp in zip(widths, keep) if kp]

    def body(*refs):
        n_in = len(args)
        outs = list(refs[n_in:])
        full = [outs.pop(0) if kp else None for kp in keep]
        _qkv_proj_kernel(*refs[:n_in], *full, widths=widths, q_scale=q_scale, rope=rope)

    res = pl.pallas_call(
        body,
        grid=(bsz, t // tm),
        in_specs=in_specs,
        out_specs=[row(n) for n in out_widths],
        out_shape=[jax.ShapeDtypeStruct((bsz, t, n), BF16) for n in out_widths],
        compiler_params=_params("parallel", "parallel"),
        name="qkv_proj",
    )(*args)
    return res if want_q else [None] + list(res)


def _split_heads(x):
    lane = lax.broadcasted_iota(jnp.int32, x.shape, 1)
    zero = jnp.zeros_like(x)
    return jnp.concatenate([jnp.where(lane < 64, x, zero), jnp.where(lane >= 64, x, zero)], axis=0)


def _merge_heads(o):
    r = o.shape[0] // 2
    lane = lax.broadcasted_iota(jnp.int32, (r, o.shape[1]), 1)
    return jnp.where(lane < 64, o[:r], o[r:])


def _with_ones(v):
    return jnp.concatenate([v, jnp.ones_like(v)], axis=1)


def _stage_with_ones(dst_ref, src_ref):
    for c0 in range(0, src_ref.shape[1], XPOSE_CHUNK):
        n = min(XPOSE_CHUNK, src_ref.shape[1] - c0)
        dst_ref[c0:c0 + n, :] = _with_ones(src_ref[0, c0:c0 + n, :])


def _stage_transposed(dst_ref, src_ref, row0, width):
    for c0 in range(0, width, XPOSE_CHUNK):
        n = min(XPOSE_CHUNK, width - c0)
        dst_ref[:, c0:c0 + n] = src_ref[0, row0 + c0:row0 + c0 + n, :].T


def _nat_kernel(q_ref, k_ref, v_ref, kc_ref, vc_ref, bias_ref, o_ref, kt_scr, va_scr, snb_a, scx_a, snb_b, scx_b,
                *, rows):
    w = GRID_W
    kh = min(NAT_KH, rows)
    nk = kh * w
    t = rows * w
    _stage_transposed(kt_scr.at[0], k_ref, 0, t)
    _stage_transposed(kt_scr.at[1], k_ref, w, t - 2 * w)
    kct = kc_ref[0].T
    _stage_with_ones(va_scr, v_ref)
    vca = _with_ones(vc_ref[0])

    def window(r):
        r0 = jnp.clip(r - kh // 2, 0, rows - kh)
        return r0, pl.multiple_of(r * w, w)

    def scores(r, snb_ref, scx_ref):
        r0, qs = window(r)
        ty = r - r0
        par = r0 & 1
        kts = pl.multiple_of((r0 - par) * w, 2 * w)
        q2 = _split_heads(q_ref[0, pl.ds(qs, w), :])
        bias = jnp.concatenate(
            [jnp.concatenate([bias_ref[hd, 2 * m - ty + NAT_KH - 1] for m in range(kh // 2)], axis=1)
             for hd in range(2)], axis=0)
        snb_ref[...] = _dot(q2, kt_scr[par, :, pl.ds(kts, nk)]) + bias
        scx_ref[...] = _dot(q2, kct)

    def finish(r, snb_ref, scx_ref):
        r0, qs = window(r)
        ks = pl.multiple_of(r0 * w, w)
        s_nb, s_cx = snb_ref[...], scx_ref[...]
        m = jnp.maximum(jnp.max(s_nb, axis=-1, keepdims=True), jnp.max(s_cx, axis=-1, keepdims=True))
        p_nb = jnp.exp2(s_nb - m)
        p_cx = jnp.exp2(s_cx - m)
        oa = _dot(p_nb.astype(BF16), va_scr[pl.ds(ks, nk), :]) + _dot(p_cx.astype(BF16), vca)
        o_ref[0, pl.ds(qs, w), :] = _merge_heads(oa[:, :LANES] / oa[:, LANES:]).astype(BF16)

    scores(0, snb_a, scx_a)

    def body(i, carry):
        r = 2 * i
        scores(r + 1, snb_b, scx_b)
        finish(r, snb_a, scx_a)
        scores(jnp.minimum(r + 2, rows - 1), snb_a, scx_a)
        finish(r + 1, snb_b, scx_b)
        return carry

    lax.fori_loop(0, rows // 2, body, 0, unroll=16)


def _nat_attention(q, k, v, kc, vc, bias):
    bsz, t, d = q.shape
    l = kc.shape[1]
    rows = t // GRID_W
    hp = d // LANES
    lat = pl.BlockSpec((1, t, LANES), lambda p, b: (b, 0, p))
    ctx = pl.BlockSpec((1, l, LANES), lambda p, b: (b, 0, p))
    return pl.pallas_call(
        functools.partial(_nat_kernel, rows=rows),
        grid=(hp, bsz),
        in_specs=[lat, lat, lat, ctx, ctx,
                  pl.BlockSpec((2,) + bias.shape[1:], lambda p, b: (p, 0, 0, 0))],
        out_specs=lat,
        out_shape=jax.ShapeDtypeStruct((bsz, t, d), BF16),
        scratch_shapes=[pltpu.VMEM((2, LANES, t), BF16), pltpu.VMEM((t, 2 * LANES), BF16)]
        + [pltpu.VMEM((2 * GRID_W, n), F32) for n in (min(NAT_KH, rows) * GRID_W, l)] * 2,
        compiler_params=_params("parallel", "parallel"),
        name="nat_attention",
    )(q, k, v, kc, vc, bias)


def _nat_bias_kernel(x_ref, o_ref, *, tn):
    w = GRID_W
    sh = w.bit_length() - 1
    assert 1 << sh == w
    k2 = x_ref.shape[1]
    n = pl.program_id(0) * tn + lax.broadcasted_iota(jnp.int32, (1, tn), 1)
    c, a, cc = n >> (sh + 1), (n >> sh) & 1, n & (w - 1)
    j = lax.broadcasted_iota(jnp.int32, (k2, 1), 0)
    sel = jnp.where(((j >> (sh + 1)) == a) & ((j & (2 * w - 1)) == cc - c + w - 1), 1.0, 0.0).astype(BF16)
    hi, mid, lo = _split3(x_ref[...])
    vals = _dot(hi, sel) + _dot(mid, sel) + _dot(lo, sel)
    c0 = jnp.clip(c - NAT_KW // 2, 0, w - NAT_KW)
    o_ref[...] = jnp.where((cc >= c0) & (cc < c0 + NAT_KW), vals * LOG2E, NEG_INF)


def _nat_bias(rpb):
    w = GRID_W
    nh, nr, ncol = rpb.shape
    left = w - NAT_KW
    v = jnp.pad(rpb.astype(F32), ((0, 0), (0, 0), (left, 2 * w - ncol - left)))
    pairs = jnp.concatenate([v[:, :-1], v[:, 1:]], axis=-1).reshape(nh * (nr - 1), 4 * w)
    tn = 16 * LANES
    out = pl.pallas_call(
        functools.partial(_nat_bias_kernel, tn=tn),
        grid=(w * 2 * w // tn,),
        in_specs=[pl.BlockSpec(pairs.shape, lambda i: (0, 0))],
        out_specs=pl.BlockSpec((pairs.shape[0], tn), lambda i: (0, i)),
        out_shape=jax.ShapeDtypeStruct((pairs.shape[0], w * 2 * w), F32),
        compiler_params=_params("parallel"),
        name="nat_bias",
    )(pairs)
    return out.reshape(nh, nr - 1, w, 2 * w)


def _ctx_attn_kernel(q_ref, k_ref, v_ref, o_ref):
    q2 = _split_heads(q_ref[0])
    s = _dot(q2, k_ref[0].T)
    p = jnp.exp2(s - jnp.max(s, axis=-1, keepdims=True))
    l = jnp.sum(p, axis=-1, keepdims=True)
    o_ref[0] = _merge_heads(_dot(p.astype(BF16), v_ref[0]) / l).astype(BF16)


def _ctx_attention(q, k, v):
    bsz, l, d = q.shape
    blk = pl.BlockSpec((1, l, LANES), lambda b, p: (b, 0, p))
    return pl.pallas_call(
        _ctx_attn_kernel,
        grid=(bsz, d // LANES),
        in_specs=[blk, blk, blk],
        out_specs=blk,
        out_shape=jax.ShapeDtypeStruct((bsz, l, d), BF16),
        compiler_params=_params("parallel", "parallel"),
        name="ctx_attention",
    )(q, k, v)


POOL_HALO = SUBLANES


def _split3(x):
    hi = x.astype(BF16)
    r1 = x - hi.astype(F32)
    mid = r1.astype(BF16)
    return hi, mid, (r1 - mid.astype(F32)).astype(BF16)


def _pool_kernel(h_ref, prev_ref, next_ref, mod_ref, g_ref, pw_ref, ps_ref, o_ref, band_scr, *, sub, t_total):
    i = pl.program_id(1)
    nt = pl.num_programs(1)
    hl = POOL_HALO
    tm, d = h_ref.shape[1:]
    gd = d // len(POOL_WINDOWS)
    shift, scale, gate = _mod_rows(mod_ref, 1)

    @pl.when((pl.program_id(0) == 0) & (i == 0))
    def _():
        off = lax.broadcasted_iota(jnp.int32, (sub, sub), 1) - lax.broadcasted_iota(jnp.int32, (sub, sub), 0)
        for gi, win in enumerate(POOL_WINDOWS):
            band_scr[gi] = jnp.where((off >= -(win // 2)) & (off < win // 2), 1.0, 0.0).astype(BF16)

    er = lax.broadcasted_iota(jnp.int32, (2 * hl, 2 * hl), 0)
    hu = lax.broadcasted_iota(jnp.int32, (2 * hl, 2 * hl), 1)
    dist = jnp.where(hu < hl, hu - hl - er, hu - er + hl)
    same_side = (er < hl) == (hu < hl)
    edges = [jnp.where(same_side & (dist >= -(win // 2)) & (dist < win // 2), 1.0, 0.0).astype(BF16)
             for win in POOL_WINDOWS]

    for s0 in range(0, tm, sub):
        first, final = s0 == 0, s0 + sub == tm
        x = h_ref[0, s0:s0 + sub, :]
        above = prev_ref[0] if first else h_ref[0, s0 - hl:s0, :]
        below = next_ref[0] if final else h_ref[0, s0 + sub:s0 + sub + hl, :]
        xe = _ada(jnp.concatenate([above, x, below], axis=0), g_ref[...], shift, scale)
        xn = xe[hl:hl + sub]
        top = jnp.where(i > 0, xe[:hl], 0.0) if first else xe[:hl]
        bot = jnp.where(i < nt - 1, xe[hl + sub:], 0.0) if final else xe[hl + sub:]
        x3 = _split3(xn)
        h3 = _split3(jnp.concatenate([top, bot], axis=0))
        trow = i * tm + s0 + lax.broadcasted_iota(jnp.int32, (sub, 1), 0)
        parts = []
        for gi, win in enumerate(POOL_WINDOWS):
            half = win // 2
            cols = slice(gi * gd, (gi + 1) * gd)
            band, edge = band_scr[gi], edges[gi]
            tot = _dot(band, x3[0][:, cols]) + _dot(band, x3[1][:, cols]) + _dot(band, x3[2][:, cols])
            fix = _dot(edge, h3[0][:, cols]) + _dot(edge, h3[1][:, cols]) + _dot(edge, h3[2][:, cols])
            tot = jnp.concatenate([tot[:hl] + fix[:hl], tot[hl:sub - hl], tot[sub - hl:] + fix[hl:]], axis=0)
            cnt = (jnp.minimum(trow + half, t_total) - jnp.maximum(trow - half, 0)).astype(F32)
            pooled = (tot / cnt - xn[:, cols]).astype(BF16)
            parts.append(_dot(pooled, pw_ref[gi]))
        y = jnp.concatenate(parts, axis=-1) * ps_ref[...]
        o_ref[0, s0:s0 + sub, :] = x + gate * y


def _pool(h, mod, g, pw, ps):
    bsz, t, d = h.shape
    tm = min(POOL_TILE, t)
    sub = min(POOL_SUBTILE, tm)
    per = tm // POOL_HALO
    last = t // POOL_HALO - 1
    return pl.pallas_call(
        functools.partial(_pool_kernel, sub=sub, t_total=t),
        grid=(bsz, t // tm),
        in_specs=[pl.BlockSpec((1, tm, d), lambda b, i: (b, i, 0)),
                  pl.BlockSpec((1, POOL_HALO, d), lambda b, i: (b, jnp.maximum(i * per - 1, 0), 0)),
                  pl.BlockSpec((1, POOL_HALO, d), lambda b, i: (b, jnp.minimum((i + 1) * per, last), 0)),
                  _mod_spec(mod, d),
                  pl.BlockSpec((1, d), lambda b, i: (0, 0)),
                  _resident(pw.shape),
                  pl.BlockSpec((1, d), lambda b, i: (0, 0))],
        out_specs=pl.BlockSpec((1, tm, d), lambda b, i: (b, i, 0)),
        out_shape=jax.ShapeDtypeStruct((bsz, t, d), F32),
        scratch_shapes=[pltpu.VMEM((len(POOL_WINDOWS), sub, sub), BF16)],
        compiler_params=_params("arbitrary", "arbitrary"),
        name="pool",
    )(h, h, h, mod, g.reshape(1, d), pw, ps.reshape(1, d))


def _swa_kernel(sink_ref, q_ref, k_ref, v_ref, kc_ref, vc_ref, o_ref, kt_scr, va_scr, sloc_a, scx_a, sloc_b, scx_b,
                *, t_total):
    kv = pl.program_id(1)
    blk = SWA_BLOCK
    nb = t_total // blk
    grp = SWA_Q_HEADS // SWA_KV_HEADS
    nrow = grp * blk
    _stage_transposed(kt_scr, k_ref, 0, t_total)
    kct = kc_ref[0].T
    _stage_with_ones(va_scr, v_ref)
    vca = _with_ones(vc_ref[0])
    row = lax.broadcasted_iota(jnp.int32, (nrow, 1), 0)
    sink = jnp.zeros((nrow, 1), F32)
    for gi in range(grp):
        sink = jnp.where((row >= gi * blk) & (row < (gi + 1) * blk), sink_ref[kv * grp + gi] * LOG2E, sink)
    qi = lax.broadcasted_iota(jnp.int32, (nrow, blk), 0) & (blk - 1)
    kj = lax.broadcasted_iota(jnp.int32, (nrow, blk), 1)
    open_blk = jnp.zeros((nrow, blk), F32)
    prev_blk = jnp.where(kj >= qi, 0.0, NEG_INF)
    next_blk = jnp.where(kj <= qi, 0.0, NEG_INF)

    shut_blk = jnp.full((nrow, blk), NEG_INF, F32)
    span = 3 * blk
    mask_first = jnp.concatenate([open_blk, next_blk, shut_blk], axis=1)
    mask_mid = jnp.concatenate([prev_blk, open_blk, next_blk], axis=1)
    mask_last = jnp.concatenate([shut_blk, prev_blk, open_blk], axis=1)

    def offsets(n):
        qs = pl.multiple_of(n * blk, blk)
        return qs, pl.multiple_of(jnp.clip(qs - blk, 0, t_total - span), blk)

    def scores(n, mask, sloc_ref, scx_ref):
        qs, ks = offsets(n)
        qb = q_ref[0, pl.ds(qs, blk), :]
        q4 = jnp.concatenate([_split_heads(qb[:, :LANES]), _split_heads(qb[:, LANES:])], axis=0)
        sloc_ref[...] = _dot(q4, kt_scr[:, pl.ds(ks, span)]) + mask
        scx_ref[...] = _dot(q4, kct)

    def finish(n, sloc_ref, scx_ref):
        qs, ks = offsets(n)
        s_loc, s_cx = sloc_ref[...], scx_ref[...]
        m = jnp.maximum(jnp.maximum(jnp.max(s_loc, axis=-1, keepdims=True),
                                    jnp.max(s_cx, axis=-1, keepdims=True)), sink)
        p_loc = jnp.exp2(s_loc - m)
        p_cx = jnp.exp2(s_cx - m)
        oa = _dot(p_loc.astype(BF16), va_scr[pl.ds(ks, span), :]) + _dot(p_cx.astype(BF16), vca)
        o = oa[:, :LANES] / (oa[:, LANES:] + jnp.exp2(sink - m))
        out = jnp.concatenate([_merge_heads(o[:2 * blk]), _merge_heads(o[2 * blk:])], axis=-1)
        o_ref[0, pl.ds(qs, blk), :] = out.astype(BF16)

    slot_a, slot_b = (sloc_a, scx_a), (sloc_b, scx_b)
    scores(0, mask_first, *slot_a)
    scores(1, mask_mid, *slot_b)
    finish(0, *slot_a)

    def body(i, carry):
        n = 1 + 2 * i
        scores(n + 1, mask_mid, *slot_a)
        finish(n, *slot_b)
        scores(n + 2, mask_mid, *slot_b)
        finish(n + 1, *slot_a)
        return carry

    lax.fori_loop(0, (nb - 4) // 2, body, 0, unroll=7)
    scores(nb - 2, mask_mid, *slot_a)
    finish(nb - 3, *slot_b)
    scores(nb - 1, mask_last, *slot_b)
    finish(nb - 2, *slot_a)
    finish(nb - 1, *slot_b)


def _swa_attention(sink, q, k, v, kc, vc):
    bsz, t, d = q.shape
    l = kc.shape[1]
    nb = t // SWA_BLOCK
    assert nb >= 4 and nb % 2 == 0 and SWA_WINDOW == SWA_BLOCK
    qw = d // SWA_KV_HEADS
    grid_spec = pltpu.PrefetchScalarGridSpec(
        num_scalar_prefetch=1,
        grid=(bsz, SWA_KV_HEADS),
        in_specs=[pl.BlockSpec((1, t, qw), lambda b, h, s: (b, 0, h)),
                  pl.BlockSpec((1, t, LANES), lambda b, h, s: (b, 0, h)),
                  pl.BlockSpec((1, t, LANES), lambda b, h, s: (b, 0, h)),
                  pl.BlockSpec((1, l, LANES), lambda b, h, s: (b, 0, h)),
                  pl.BlockSpec((1, l, LANES), lambda b, h, s: (b, 0, h))],
        out_specs=pl.BlockSpec((1, t, qw), lambda b, h, s: (b, 0, h)),
        scratch_shapes=[pltpu.VMEM((LANES, t), BF16), pltpu.VMEM((t, 2 * LANES), BF16)]
        + [pltpu.VMEM((SWA_Q_HEADS // SWA_KV_HEADS * SWA_BLOCK, n), F32) for n in (3 * SWA_BLOCK, l)] * 2,
    )
    return pl.pallas_call(
        functools.partial(_swa_kernel, t_total=t),
        grid_spec=grid_spec,
        out_shape=jax.ShapeDtypeStruct((bsz, t, d), BF16),
        compiler_params=_params("parallel", "parallel"),
        name="swa_attention",
    )(sink, q, k, v, kc, vc)


def _rope_angles(positions, dim):
    seg = dim // len(positions)
    inv = ROPE_BASE ** (-np.arange(0, seg, 2, dtype=np.float64) / seg)
    return np.concatenate([np.tile(p.astype(np.float64)[:, None] * inv, (1, 2)) for p in positions], axis=-1)


def _table(x):
    return jnp.asarray(x.astype(np.float32))


def _dup_heads(w, heads, dh):
    d = w.shape[0]
    return jnp.broadcast_to(w.reshape(d, heads, 1, dh), (d, heads, 2, dh)).reshape(d, heads * 2 * dh)


def kernel(x, c, ctx, c_ctx, w_mod, b_mod, norm_g, ffn_w_in, ffn_w_out, ret_w_in, ret_w_out, ret_gn_g, ret_decay_f, ret_decay_b, nat_w_qkv, nat_w_o, nat_rpb, pool_w, pool_scale, swa_w_qkv, swa_w_o, swa_sink, final_norm_g):
    bsz, t, d = x.shape
    depth = w_mod.shape[0]
    cc = jnp.concatenate([c, c_ctx[None], jnp.zeros((SUBLANES - bsz - 1, d), F32)], axis=0)
    mods = _modulation(cc, w_mod, b_mod)
    h, hc = x, ctx
    for i in range(depth):
        kind, occ = i % N_MIXERS, i // N_MIXERS
        last = i == depth - 1
        ctx_live = (not last) or kind != 2
        ml = mods[i, :bsz].reshape(bsz, 9, d)
        mc = mods[i, bsz:bsz + 1].reshape(1, 9, d)
        if ctx_live:
            hc, w_in, w_out = _ffn_ctx(hc, mc, norm_g[i, 0], ffn_w_in, ffn_w_out, i, 0)
        else:
            w_in, w_out = _ffn_weights_bf16(ffn_w_in, ffn_w_out, i, 0)
        h = _ffn(h, ml, norm_g[i, 0], w_in, w_out, 0)
        g1 = norm_g[i, 1]
        yl = yc = None
        if kind == 0:
            assert not last, "retention as the last layer is not wired up"
            w = ret_w_in[occ].astype(BF16)
            wo = ret_w_out[occ].astype(BF16)
            ang = _rope_angles([np.arange(t)], RET_QK_DIM)[:, :RET_QK_DIM // 2]
            dec_f = jnp.broadcast_to(ret_decay_f[occ].astype(F32)[:, None, None], (RET_HEADS, 1, LANES))
            dec_b = jnp.broadcast_to(ret_decay_b[occ].astype(F32)[:, None, None], (RET_HEADS, 1, LANES))
            qc, kc, vc, gc = _ret_proj(hc, mc, g1, w)
            zeros = jnp.zeros((bsz, RET_HEADS, RET_QK_DIM, RET_V_DIM), F32)
            oc, s_f, s_b = _ret_mix(dec_f, dec_b, qc, kc, vc, gc, ret_gn_g[occ], zeros, zeros)
            ql, kl, vl, gl = _ret_proj(h, ml, g1, w, _table(np.cos(ang)), _table(np.sin(ang)))
            ol, _, _ = _ret_mix(dec_f, dec_b, ql, kl, vl, gl, ret_gn_g[occ], s_f, s_b)
            yl, yc = (ol, wo), (oc, wo)
        elif kind == 1:
            w = nat_w_qkv[occ].astype(BF16)
            wo = nat_w_o[occ].astype(BF16)
            widths = (d, d, d)
            qc, kc, vc = _qkv_proj(hc, mc, g1, w, widths, NAT_HEAD_DIM ** -0.5 * LOG2E, want_q=not last)
            ql, kl, vl = _qkv_proj(h, ml, g1, w, widths, NAT_HEAD_DIM ** -0.5 * LOG2E)
            yl = (_nat_attention(ql, kl, vl, kc, vc, _nat_bias(nat_rpb[occ])), wo)
            if not last:
                yc = (_ctx_attention(qc, kc, vc), wo)
        elif kind == 2:
            pw = pool_w[occ].astype(BF16)
            h_new = _pool(h, ml, g1, pw, pool_scale[occ])
            if not last:
                hc = _pool(hc, mc, g1, pw, pool_scale[occ])
            h = h_new
        else:
            assert last, "windowed attention with live context outputs is not wired up"
            nq = SWA_Q_HEADS * SWA_HEAD_DIM
            nkv = SWA_KV_HEADS * SWA_HEAD_DIM
            wq = swa_w_qkv[occ][:, :nq]
            wk = _dup_heads(swa_w_qkv[occ][:, nq:nq + nkv], SWA_KV_HEADS, SWA_HEAD_DIM)
            wv = _dup_heads(swa_w_qkv[occ][:, nq + nkv:], SWA_KV_HEADS, SWA_HEAD_DIM)
            w = jnp.concatenate([wq, wk, wv], axis=-1).astype(BF16)
            wo = swa_w_o[occ].astype(BF16)
            widths = (nq, 2 * nkv, 2 * nkv)
            tt = np.arange(t)
            ang = _rope_angles([tt // GRID_W, tt % GRID_W], SWA_HEAD_DIM)
            ang = np.tile(ang, (1, LANES // SWA_HEAD_DIM))
            lane = np.arange(LANES)
            sin = np.where(lane % 32 < 16, -np.sin(ang), np.sin(ang))
            _, kc, vc = _qkv_proj(hc, mc, g1, w, widths, SWA_HEAD_DIM ** -0.5 * LOG2E, want_q=False)
            ql, kl, vl = _qkv_proj(h, ml, g1, w, widths, SWA_HEAD_DIM ** -0.5 * LOG2E, _table(np.cos(ang)), _table(sin))
            yl = (_swa_attention(swa_sink[occ].astype(F32), ql, kl, vl, kc, vc), wo)
        if not last:
            hc, w_in, w_out = _ffn_ctx(hc, mc, norm_g[i, 2], ffn_w_in, ffn_w_out, i, 1, yc)
        else:
            w_in, w_out = _ffn_weights_bf16(ffn_w_in, ffn_w_out, i, 1)
        h = _ffn(h, ml, norm_g[i, 2], w_in, w_out, 1, yl, final_norm_g if last else None)
    return h
```

```python
import functools
import math

import jax
import jax.numpy as jnp
import numpy as np
from jax import lax
from jax.experimental import pallas as pl
from jax.experimental.pallas import tpu as pltpu

F32 = jnp.float32
BF16 = jnp.bfloat16

EPS = 1e-6
NEG_INF = -1e30
LOG2E = math.log2(math.e)
ROPE_BASE = 10000.0
GRID_W = 64
N_MIXERS = 4
FFN_HIDDEN = 2816
RET_HEADS = 4
RET_QK_DIM = 256
RET_V_DIM = 512
NAT_HEAD_DIM = 64
NAT_KH = 8
NAT_KW = 16
POOL_WINDOWS = (2, 4, 8, 16)
SWA_Q_HEADS = 16
SWA_KV_HEADS = 4
SWA_HEAD_DIM = 64
SWA_WINDOW = 128
SWA_BLOCK = 128
HEAD_DIM = 64
assert NAT_HEAD_DIM == SWA_HEAD_DIM == HEAD_DIM
MOD_ROWS = 9

LANES = 128
SUBLANES = 8
VMEM_LIMIT = 56 * 1024 * 1024
MOD_TILE = 2304
POOL_TILE = 512
POOL_SUBTILE = 256
PROJ_TILE = 512
FFN_TILE = 1024
FFN_SUBTILE = 256
CTX_FFN_CHUNK = 256
RET_CHUNK = 256
XPOSE_CHUNK = 512


def _params(*sem):
    return pltpu.CompilerParams(dimension_semantics=sem, vmem_limit_bytes=VMEM_LIMIT)


def _resident(shape):
    nd = len(shape)
    return pl.BlockSpec(shape, lambda *_: (0,) * nd, pipeline_mode=pl.Buffered(1))


def _silu(x):
    return x * jax.nn.sigmoid(x)


def _ada(x, g, shift, scale):
    var = jnp.mean(x * x, axis=-1, keepdims=True)
    y = x * lax.rsqrt(var + EPS) * g
    return y * (1.0 + scale) + shift


def _mod_rows(mod_ref, j):
    return (mod_ref[0, 3 * j:3 * j + 1, :], mod_ref[0, 3 * j + 1:3 * j + 2, :],
            mod_ref[0, 3 * j + 2:3 * j + 3, :])


def _mod_spec(mod, d):
    if mod.shape[0] == 1:
        return pl.BlockSpec((1, MOD_ROWS, d), lambda b, t: (0, 0, 0))
    return pl.BlockSpec((1, MOD_ROWS, d), lambda b, t: (b, 0, 0))


def _dot(a, b):
    return jnp.dot(a, b, preferred_element_type=F32)


def _dot_tn(a, b):
    return lax.dot_general(a, b, (((0,), (0,)), ((), ())), preferred_element_type=F32)


def _mod_kernel(c_ref, w_ref, b_ref, o_ref):
    s = _silu(c_ref[...]).astype(BF16)
    o_ref[0] = _dot(s, w_ref[0].astype(BF16)) + b_ref[0]


def _modulation(cc, w_mod, b_mod):
    depth, d, n = w_mod.shape
    tn = MOD_TILE
    return pl.pallas_call(
        _mod_kernel,
        grid=(depth, n // tn),
        in_specs=[pl.BlockSpec((SUBLANES, d), lambda l, j: (0, 0)),
                  pl.BlockSpec((1, d, tn), lambda l, j: (l, 0, j)),
                  pl.BlockSpec((1, 1, tn), lambda l, j: (l, 0, j))],
        out_specs=pl.BlockSpec((1, SUBLANES, tn), lambda l, j: (l, 0, j)),
        out_shape=jax.ShapeDtypeStruct((depth, SUBLANES, n), F32),
        compiler_params=_params("arbitrary", "arbitrary"),
        name="modulation",
    )(cc, w_mod, b_mod.reshape(depth, 1, n))


def _ffn_kernel(h_ref, mod_ref, g_ref, win_ref, wout_ref, *rest, j, mixer_out, final):
    o_ref = rest[-1]
    shift, scale, gate = _mod_rows(mod_ref, j)
    tm = h_ref.shape[1]
    sub = min(tm, FFN_SUBTILE)
    for r0 in range(0, tm, sub):
        x = h_ref[0, r0:r0 + sub, :]
        if mixer_out:
            y_ref, wo_ref = rest[:2]
            x = x + _mod_rows(mod_ref, 1)[2] * _dot(y_ref[0, r0:r0 + sub, :], wo_ref[...])
        xb = _ada(x, g_ref[...], shift, scale).astype(BF16)
        if len(win_ref.shape) == 3:
            wg, wv = win_ref[0], win_ref[1]
        else:
            f = win_ref.shape[1] // 2
            wg, wv = win_ref[:, :f], win_ref[:, f:]
        hid = (_silu(_dot(xb, wg)) * _dot(xb, wv)).astype(BF16)
        out = x + (0.5 * gate) * _dot(hid, wout_ref[...])
        if final:
            var = jnp.mean(out * out, axis=-1, keepdims=True)
            out = out * lax.rsqrt(var + EPS) * rest[-2][...]
        o_ref[0, r0:r0 + sub, :] = out


def _ffn(h, mod, g, w_in, w_out, half, mixer_out=None, final_g=None):
    bsz, t, d = h.shape
    tm = min(FFN_TILE, t)
    row = lambda n: pl.BlockSpec((1, tm, n), lambda b, i: (b, i, 0))
    vec = pl.BlockSpec((1, d), lambda b, i: (0, 0))
    in_specs = [row(d), _mod_spec(mod, d), vec, _resident(w_in.shape), _resident(w_out.shape)]
    args = [h, mod, g.reshape(1, d), w_in, w_out]
    if mixer_out is not None:
        y, wo = mixer_out
        in_specs += [row(y.shape[-1]), _resident(wo.shape)]
        args += [y, wo]
    if final_g is not None:
        in_specs.append(vec)
        args.append(final_g.reshape(1, d))
    return pl.pallas_call(
        functools.partial(_ffn_kernel, j=2 * half, mixer_out=mixer_out is not None, final=final_g is not None),
        grid=(bsz, t // tm),
        in_specs=in_specs,
        out_specs=row(d),
        out_shape=jax.ShapeDtypeStruct((bsz, t, d), F32),
        compiler_params=_params("parallel", "parallel"),
        name="ffn",
    )(*args)


def _ffn_ctx_kernel(h_ref, mod_ref, g_ref, wa_ref, wb_ref, wo_ref, *rest, j, mixer_out):
    o_ref, win_bf_ref, wout_bf_ref, x_scr, xb_scr, acc_scr = rest[-6:]
    s = pl.program_id(0)
    shift, scale, gate = _mod_rows(mod_ref, j)

    @pl.when(s == 0)
    def _():
        x = h_ref[...]
        if mixer_out:
            y_ref, wmix_ref = rest[:2]
            x = x + _mod_rows(mod_ref, 1)[2] * _dot(y_ref[...], wmix_ref[...])
        x_scr[...] = x
        xb_scr[...] = _ada(x, g_ref[...], shift, scale).astype(BF16)
        acc_scr[...] = jnp.zeros_like(acc_scr)

    wa, wb, wo = wa_ref[0, 0].astype(BF16), wb_ref[0, 0].astype(BF16), wo_ref[0, 0].astype(BF16)
    win_bf_ref[0] = wa
    win_bf_ref[1] = wb
    wout_bf_ref[...] = wo
    xb = xb_scr[...]
    hid = (_silu(_dot(xb, wa)) * _dot(xb, wb)).astype(BF16)
    acc_scr[...] += _dot(hid, wo)

    @pl.when(s == pl.num_programs(0) - 1)
    def _():
        o_ref[...] = x_scr[...] + (0.5 * gate) * acc_scr[...]


def _ffn_ctx(hc, mod, g, w_in, w_out, layer, half, mixer_out=None):
    bsz, l, d = hc.shape
    f = w_out.shape[2]
    fc = CTX_FFN_CHUNK
    nf = f // fc
    n = bsz * l
    const = lambda shape: pl.BlockSpec(shape, lambda s: (0,) * len(shape))
    in_specs = [const((n, d)), const((1, MOD_ROWS, d)), const((1, d)),
                pl.BlockSpec((1, 1, d, fc), lambda s: (layer, half, 0, s)),
                pl.BlockSpec((1, 1, d, fc), lambda s: (layer, half, 0, nf + s)),
                pl.BlockSpec((1, 1, fc, d), lambda s: (layer, half, s, 0))]
    args = [hc.reshape(n, d), mod, g.reshape(1, d), w_in, w_in, w_out]
    if mixer_out is not None:
        y, wmix = mixer_out
        in_specs += [const((n, y.shape[-1])), _resident(wmix.shape)]
        args += [y.reshape(n, y.shape[-1]), wmix]
    out, w_in_bf, w_out_bf = pl.pallas_call(
        functools.partial(_ffn_ctx_kernel, j=2 * half, mixer_out=mixer_out is not None),
        grid=(nf,),
        in_specs=in_specs,
        out_specs=[const((n, d)),
                   pl.BlockSpec((2, d, fc), lambda s: (0, 0, s)),
                   pl.BlockSpec((fc, d), lambda s: (s, 0))],
        out_shape=[jax.ShapeDtypeStruct((n, d), F32),
                   jax.ShapeDtypeStruct((2, d, f), BF16),
                   jax.ShapeDtypeStruct((f, d), BF16)],
        scratch_shapes=[pltpu.VMEM((n, d), F32), pltpu.VMEM((n, d), BF16), pltpu.VMEM((n, d), F32)],
        compiler_params=_params("arbitrary"),
        name="ffn_ctx",
    )(*args)
    return out.reshape(bsz, l, d), w_in_bf, w_out_bf


def _ffn_weights_bf16(w_in, w_out, layer, half):
    return w_in[layer, half].astype(BF16), w_out[layer, half].astype(BF16)


def _ret_proj_kernel(h_ref, mod_ref, g_ref, w_ref, *rest, rope):
    if rope:
        cos_ref, sin_ref, q_ref, k_ref, v_ref, gate_ref = rest
    else:
        q_ref, k_ref, v_ref, gate_ref = rest
    shift, scale, _ = _mod_rows(mod_ref, 1)
    dk, nh = RET_QK_DIM, RET_HEADS
    half = dk // 2
    v0 = 2 * nh * dk
    nv = nh * RET_V_DIM
    tm = h_ref.shape[1]
    sub = min(tm, FFN_SUBTILE)
    for r0 in range(0, tm, sub):
        rows = slice(r0, r0 + sub)
        xb = _ada(h_ref[0, rows, :], g_ref[...], shift, scale).astype(BF16)

        def rot(x):
            if not rope:
                return x.astype(BF16)
            c, s = cos_ref[rows, :], sin_ref[rows, :]
            x1, x2 = x[:, :half], x[:, half:]
            return jnp.concatenate([x1 * c - x2 * s, x2 * c + x1 * s], axis=-1).astype(BF16)

        for hd in range(nh):
            q_ref[0, rows, hd * dk:(hd + 1) * dk] = rot(_dot(xb, w_ref[:, hd * dk:(hd + 1) * dk]))
            kcol = nh * dk + hd * dk
            k_ref[0, rows, hd * dk:(hd + 1) * dk] = rot(_dot(xb, w_ref[:, kcol:kcol + dk]) * dk ** -0.5)
        v_ref[0, rows, :] = _dot(xb, w_ref[:, v0:v0 + nv]).astype(BF16)
        gate_ref[0, rows, :] = _silu(_dot(xb, w_ref[:, v0 + nv:v0 + 2 * nv])).astype(BF16)


def _ret_proj(h, mod, g, w, cos=None, sin=None):
    bsz, t, d = h.shape
    tm = min(PROJ_TILE, t)
    rope = cos is not None
    nqk = RET_HEADS * RET_QK_DIM
    nv = RET_HEADS * RET_V_DIM
    row = lambda n: pl.BlockSpec((1, tm, n), lambda b, i: (b, i, 0))
    in_specs = [row(d), _mod_spec(mod, d), pl.BlockSpec((1, d), lambda b, i: (0, 0)), _resident(w.shape)]
    args = [h, mod, g.reshape(1, d), w]
    if rope:
        in_specs += [pl.BlockSpec((tm, RET_QK_DIM // 2), lambda b, i: (i, 0))] * 2
        args += [cos, sin]
    return pl.pallas_call(
        functools.partial(_ret_proj_kernel, rope=rope),
        grid=(bsz, t // tm),
        in_specs=in_specs,
        out_specs=[row(nqk), row(nqk), row(nv), row(nv)],
        out_shape=[jax.ShapeDtypeStruct((bsz, t, nqk), BF16), jax.ShapeDtypeStruct((bsz, t, nqk), BF16),
                   jax.ShapeDtypeStruct((bsz, t, nv), BF16), jax.ShapeDtypeStruct((bsz, t, nv), BF16)],
        compiler_params=_params("parallel", "parallel"),
        name="ret_proj",
    )(*args)


def _log_sigmoid(x):
    return jnp.minimum(x, 0.0) - jnp.log(1.0 + jnp.exp(-jnp.abs(x)))


def _ret_mix_kernel(decf_ref, decb_ref, q_ref, k_ref, v_ref, g_ref, gn_ref, s0f_ref, s0b_ref,
                    y_ref, sf_ref, sb_ref, s_scr, sb_scr, *, chunk):
    c = chunk
    nc = q_ref.shape[1] // c
    lgf = _log_sigmoid(decf_ref[0])[:, :1]
    lgb = _log_sigmoid(decb_ref[0])[:, :1]
    dist = lax.broadcasted_iota(jnp.int32, (c, c), 0) - lax.broadcasted_iota(jnp.int32, (c, c), 1)
    decay = jnp.where(dist >= 0,
                      jnp.exp(jnp.maximum(dist, 0).astype(F32) * lgf),
                      jnp.exp(jnp.maximum(-dist, 0).astype(F32) * lgb))
    pos = lax.broadcasted_iota(jnp.int32, (c, 1), 0).astype(F32)
    qd_f, kd_f, cd_f = jnp.exp((pos + 1.0) * lgf), jnp.exp((c - 1.0 - pos) * lgf), jnp.exp(c * lgf)
    qd_b, kd_b, cd_b = jnp.exp((c - pos) * lgb), jnp.exp(pos * lgb), jnp.exp(c * lgb)

    def rows(n):
        return pl.ds(pl.multiple_of(n * c, c), c)

    s_scr[...] = s0b_ref[0, 0]

    def bwd(i, carry):
        n = nc - 1 - i
        s = s_scr[...]
        sb_scr[n] = s.astype(BF16)
        kd = (k_ref[0, rows(n), :].astype(F32) * kd_b).astype(BF16)
        s_scr[...] = s * cd_b + _dot_tn(kd, v_ref[0, rows(n), :])
        return carry

    lax.fori_loop(0, nc, bwd, 0, unroll=min(8, nc))
    sb_ref[0, 0] = s_scr[...]
    s_scr[...] = s0f_ref[0, 0]

    def fwd(n, carry):
        q, k, v = q_ref[0, rows(n), :], k_ref[0, rows(n), :], v_ref[0, rows(n), :]
        qf, kf = q.astype(F32), k.astype(F32)
        s = s_scr[...]
        att = (_dot(q, k.T) * decay).astype(BF16)
        o = (_dot(att, v) + _dot((qf * qd_f).astype(BF16), s.astype(BF16))
             + _dot((qf * qd_b).astype(BF16), sb_scr[n]))
        mu = jnp.mean(o, axis=-1, keepdims=True)
        ctr = o - mu
        var = jnp.mean(ctr * ctr, axis=-1, keepdims=True)
        on = ctr * lax.rsqrt(var + EPS) * gn_ref[...]
        y_ref[0, rows(n), :] = (g_ref[0, rows(n), :].astype(F32) * on).astype(BF16)
        s_scr[...] = s * cd_f + _dot_tn((kf * kd_f).astype(BF16), v)
        return carry

    lax.fori_loop(0, nc, fwd, 0, unroll=min(8, nc))
    sf_ref[0, 0] = s_scr[...]


def _ret_mix(dec_f, dec_b, q, k, v, gate, gn_g, s0_f, s0_b):
    bsz, t, _ = q.shape
    nh, dk, dv = RET_HEADS, RET_QK_DIM, RET_V_DIM
    chunk = min(RET_CHUNK, t)
    seq = lambda n: pl.BlockSpec((1, t, n), lambda b, h: (b, 0, h))
    dec = pl.BlockSpec((1, 1, LANES), lambda b, h: (h, 0, 0))
    state = pl.BlockSpec((1, 1, dk, dv), lambda b, h: (b, h, 0, 0))
    return pl.pallas_call(
        functools.partial(_ret_mix_kernel, chunk=chunk),
        grid=(bsz, nh),
        in_specs=[dec, dec, seq(dk), seq(dk), seq(dv), seq(dv),
                  pl.BlockSpec((1, dv), lambda b, h: (0, h)), state, state],
        out_specs=[seq(dv), state, state],
        out_shape=[jax.ShapeDtypeStruct((bsz, t, nh * dv), BF16),
                   jax.ShapeDtypeStruct((bsz, nh, dk, dv), F32),
                   jax.ShapeDtypeStruct((bsz, nh, dk, dv), F32)],
        scratch_shapes=[pltpu.VMEM((dk, dv), F32), pltpu.VMEM((t // chunk, dk, dv), BF16)],
        compiler_params=_params("parallel", "parallel"),
        name="ret_mix",
    )(dec_f, dec_b, q, k, v, gate, gn_g.reshape(1, nh * dv), s0_f, s0_b)


def _qkv_proj_kernel(h_ref, mod_ref, g_ref, w_ref, *rest, widths, q_scale, rope):
    if rope:
        cos_ref, sin_ref = rest[:2]
        outs = rest[2:]
    else:
        outs = rest
    shift, scale, _ = _mod_rows(mod_ref, 1)
    tm = h_ref.shape[1]
    sub = min(tm, FFN_SUBTILE)
    for r0 in range(0, tm, sub):
        rows = slice(r0, r0 + sub)
        xb = _ada(h_ref[0, rows, :], g_ref[...], shift, scale).astype(BF16)
        col = 0
        for idx, (o_ref, n) in enumerate(zip(outs, widths)):
            if o_ref is not None:
                y = _dot(xb, w_ref[:, col:col + n])
                for c0 in range(0, n, LANES):
                    yc = y[:, c0:c0 + LANES]
                    if rope and idx < 2:
                        lane = lax.broadcasted_iota(jnp.int32, yc.shape, 1)
                        partner = jnp.where((lane & 16) == 0, pltpu.roll(yc, LANES - 16, axis=1),
                                            pltpu.roll(yc, 16, axis=1))
                        yc = yc * cos_ref[rows, :] + partner * sin_ref[rows, :]
                    if idx == 0:
                        yc = yc * q_scale
                    o_ref[0, rows, c0:c0 + LANES] = yc.astype(BF16)
            col += n


def _qkv_proj(h, mod, g, w, widths, q_scale, cos=None, sin=None, want_q=True):
    bsz, t, d = h.shape
    tm = min(PROJ_TILE, t)
    rope = cos is not None
    row = lambda n: pl.BlockSpec((1, tm, n), lambda b, i: (b, i, 0))
    in_specs = [row(d), _mod_spec(mod, d), pl.BlockSpec((1, d), lambda b, i: (0, 0)), _resident(w.shape)]
    args = [h, mod, g.reshape(1, d), w]
    if rope:
        in_specs += [pl.BlockSpec((tm, LANES), lambda b, i: (i, 0))] * 2
        args += [cos, sin]
    keep = [want_q, True, True]
    out_widths = [n for n, kp in zip(widths, keep) if kp]

    def body(*refs):
        n_in = len(args)
        outs = list(refs[n_in:])
        full = [outs.pop(0) if kp else None for kp in keep]
        _qkv_proj_kernel(*refs[:n_in], *full, widths=widths, q_scale=q_scale, rope=rope)

    res = pl.pallas_call(
        body,
        grid=(bsz, t // tm),
        in_specs=in_specs,
        out_specs=[row(n) for n in out_widths],
        out_shape=[jax.ShapeDtypeStruct((bsz, t, n), BF16) for n in out_widths],
        compiler_params=_params("parallel", "parallel"),
        name="qkv_proj",
    )(*args)
    return res if want_q else [None] + list(res)


def _split_heads(x):
    lane = lax.broadcasted_iota(jnp.int32, x.shape, 1)
    zero = jnp.zeros_like(x)
    return jnp.concatenate([jnp.where(lane < HEAD_DIM, x, zero), jnp.where(lane >= HEAD_DIM, x, zero)], axis=0)


def _merge_heads(o):
    r = o.shape[0] // 2
    lane = lax.broadcasted_iota(jnp.int32, (r, o.shape[1]), 1)
    return jnp.where(lane < HEAD_DIM, o[:r], o[r:])


def _with_ones(v):
    return jnp.concatenate([v, jnp.ones_like(v)], axis=1)


def _stage_with_ones(dst_ref, src_ref):
    for c0 in range(0, src_ref.shape[1], XPOSE_CHUNK):
        n = min(XPOSE_CHUNK, src_ref.shape[1] - c0)
        dst_ref[c0:c0 + n, :] = _with_ones(src_ref[0, c0:c0 + n, :])


def _stage_transposed(dst_ref, src_ref, row0, width):
    for c0 in range(0, width, XPOSE_CHUNK):
        n = min(XPOSE_CHUNK, width - c0)
        dst_ref[:, c0:c0 + n] = src_ref[0, row0 + c0:row0 + c0 + n, :].T


def _nat_kernel(q_ref, k_ref, v_ref, kc_ref, vc_ref, bias_ref, o_ref, kt_scr, va_scr, snb_a, scx_a, snb_b, scx_b,
                *, rows):
    w = GRID_W
    kh = min(NAT_KH, rows)
    nk = kh * w
    t = rows * w
    _stage_transposed(kt_scr.at[0], k_ref, 0, t)
    _stage_transposed(kt_scr.at[1], k_ref, w, t - 2 * w)
    kct = kc_ref[0].T
    _stage_with_ones(va_scr, v_ref)
    vca = _with_ones(vc_ref[0])

    def window(r):
        r0 = jnp.clip(r - kh // 2, 0, rows - kh)
        return r0, pl.multiple_of(r * w, w)

    def scores(r, snb_ref, scx_ref):
        r0, qs = window(r)
        ty = r - r0
        par = r0 & 1
        kts = pl.multiple_of((r0 - par) * w, 2 * w)
        q2 = _split_heads(q_ref[0, pl.ds(qs, w), :])
        bias = jnp.concatenate(
            [jnp.concatenate([bias_ref[hd, 2 * m - ty + NAT_KH - 1] for m in range(kh // 2)], axis=1)
             for hd in range(2)], axis=0)
        snb_ref[...] = _dot(q2, kt_scr[par, :, pl.ds(kts, nk)]) + bias
        scx_ref[...] = _dot(q2, kct)

    def finish(r, snb_ref, scx_ref):
        r0, qs = window(r)
        ks = pl.multiple_of(r0 * w, w)
        s_nb, s_cx = snb_ref[...], scx_ref[...]
        m = jnp.maximum(jnp.max(s_nb, axis=-1, keepdims=True), jnp.max(s_cx, axis=-1, keepdims=True))
        p_nb = jnp.exp2(s_nb - m)
        p_cx = jnp.exp2(s_cx - m)
        oa = _dot(p_nb.astype(BF16), va_scr[pl.ds(ks, nk), :]) + _dot(p_cx.astype(BF16), vca)
        o_ref[0, pl.ds(qs, w), :] = _merge_heads(oa[:, :LANES] / oa[:, LANES:]).astype(BF16)

    scores(0, snb_a, scx_a)

    def body(i, carry):
        r = 2 * i
        scores(r + 1, snb_b, scx_b)
        finish(r, snb_a, scx_a)
        scores(jnp.minimum(r + 2, rows - 1), snb_a, scx_a)
        finish(r + 1, snb_b, scx_b)
        return carry

    lax.fori_loop(0, rows // 2, body, 0, unroll=16)


def _nat_attention(q, k, v, kc, vc, bias):
    bsz, t, d = q.shape
    l = kc.shape[1]
    rows = t // GRID_W
    hp = d // LANES
    lat = pl.BlockSpec((1, t, LANES), lambda p, b: (b, 0, p))
    ctx = pl.BlockSpec((1, l, LANES), lambda p, b: (b, 0, p))
    return pl.pallas_call(
        functools.partial(_nat_kernel, rows=rows),
        grid=(hp, bsz),
        in_specs=[lat, lat, lat, ctx, ctx,
                  pl.BlockSpec((2,) + bias.shape[1:], lambda p, b: (p, 0, 0, 0))],
        out_specs=lat,
        out_shape=jax.ShapeDtypeStruct((bsz, t, d), BF16),
        scratch_shapes=[pltpu.VMEM((2, LANES, t), BF16), pltpu.VMEM((t, 2 * LANES), BF16)]
        + [pltpu.VMEM((2 * GRID_W, n), F32) for n in (min(NAT_KH, rows) * GRID_W, l)] * 2,
        compiler_params=_params("parallel", "parallel"),
        name="nat_attention",
    )(q, k, v, kc, vc, bias)


def _nat_bias_kernel(x_ref, o_ref, *, tn):
    w = GRID_W
    sh = w.bit_length() - 1
    assert 1 << sh == w
    k2 = x_ref.shape[1]
    n = pl.program_id(0) * tn + lax.broadcasted_iota(jnp.int32, (1, tn), 1)
    c, a, cc = n >> (sh + 1), (n >> sh) & 1, n & (w - 1)
    j = lax.broadcasted_iota(jnp.int32, (k2, 1), 0)
    sel = jnp.where(((j >> (sh + 1)) == a) & ((j & (2 * w - 1)) == cc - c + w - 1), 1.0, 0.0).astype(BF16)
    hi, mid, lo = _split3(x_ref[...])
    vals = _dot(hi, sel) + _dot(mid, sel) + _dot(lo, sel)
    c0 = jnp.clip(c - NAT_KW // 2, 0, w - NAT_KW)
    o_ref[...] = jnp.where((cc >= c0) & (cc < c0 + NAT_KW), vals * LOG2E, NEG_INF)


def _nat_bias(rpb):
    w = GRID_W
    nh, nr, ncol = rpb.shape
    left = w - NAT_KW
    v = jnp.pad(rpb.astype(F32), ((0, 0), (0, 0), (left, 2 * w - ncol - left)))
    pairs = jnp.concatenate([v[:, :-1], v[:, 1:]], axis=-1).reshape(nh * (nr - 1), 4 * w)
    tn = 16 * LANES
    out = pl.pallas_call(
        functools.partial(_nat_bias_kernel, tn=tn),
        grid=(w * 2 * w // tn,),
        in_specs=[pl.BlockSpec(pairs.shape, lambda i: (0, 0))],
        out_specs=pl.BlockSpec((pairs.shape[0], tn), lambda i: (0, i)),
        out_shape=jax.ShapeDtypeStruct((pairs.shape[0], w * 2 * w), F32),
        compiler_params=_params("parallel"),
        name="nat_bias",
    )(pairs)
    return out.reshape(nh, nr - 1, w, 2 * w)


def _ctx_attn_kernel(q_ref, k_ref, v_ref, o_ref):
    for c0 in range(0, q_ref.shape[2], LANES):
        cols = slice(c0, c0 + LANES)
        q2 = _split_heads(q_ref[0, :, cols])
        s = _dot(q2, k_ref[0, :, cols].T)
        p = jnp.exp2(s - jnp.max(s, axis=-1, keepdims=True))
        l = jnp.sum(p, axis=-1, keepdims=True)
        o_ref[0, :, cols] = _merge_heads(_dot(p.astype(BF16), v_ref[0, :, cols]) / l).astype(BF16)


def _ctx_attention(q, k, v):
    bsz, l, d = q.shape
    blk = pl.BlockSpec((1, l, d), lambda b: (b, 0, 0))
    return pl.pallas_call(
        _ctx_attn_kernel,
        grid=(bsz,),
        in_specs=[blk, blk, blk],
        out_specs=blk,
        out_shape=jax.ShapeDtypeStruct((bsz, l, d), BF16),
        compiler_params=_params("parallel"),
        name="ctx_attention",
    )(q, k, v)


POOL_HALO = SUBLANES


def _split3(x):
    hi = x.astype(BF16)
    r1 = x - hi.astype(F32)
    mid = r1.astype(BF16)
    return hi, mid, (r1 - mid.astype(F32)).astype(BF16)


def _pool_kernel(h_ref, prev_ref, next_ref, mod_ref, g_ref, pw_ref, ps_ref, o_ref, band_scr, *, sub, t_total):
    i = pl.program_id(1)
    nt = pl.num_programs(1)
    hl = POOL_HALO
    tm, d = h_ref.shape[1:]
    gd = d // len(POOL_WINDOWS)
    shift, scale, gate = _mod_rows(mod_ref, 1)

    @pl.when((pl.program_id(0) == 0) & (i == 0))
    def _():
        off = lax.broadcasted_iota(jnp.int32, (sub, sub), 1) - lax.broadcasted_iota(jnp.int32, (sub, sub), 0)
        for gi, win in enumerate(POOL_WINDOWS):
            band_scr[gi] = jnp.where((off >= -(win // 2)) & (off < win // 2), 1.0, 0.0).astype(BF16)

    er = lax.broadcasted_iota(jnp.int32, (2 * hl, 2 * hl), 0)
    hu = lax.broadcasted_iota(jnp.int32, (2 * hl, 2 * hl), 1)
    dist = jnp.where(hu < hl, hu - hl - er, hu - er + hl)
    same_side = (er < hl) == (hu < hl)
    edges = [jnp.where(same_side & (dist >= -(win // 2)) & (dist < win // 2), 1.0, 0.0).astype(BF16)
             for win in POOL_WINDOWS]

    for s0 in range(0, tm, sub):
        first, final = s0 == 0, s0 + sub == tm
        x = h_ref[0, s0:s0 + sub, :]
        above = prev_ref[0] if first else h_ref[0, s0 - hl:s0, :]
        below = next_ref[0] if final else h_ref[0, s0 + sub:s0 + sub + hl, :]
        xe = _ada(jnp.concatenate([above, x, below], axis=0), g_ref[...], shift, scale)
        xn = xe[hl:hl + sub]
        top = jnp.where(i > 0, xe[:hl], 0.0) if first else xe[:hl]
        bot = jnp.where(i < nt - 1, xe[hl + sub:], 0.0) if final else xe[hl + sub:]
        x3 = _split3(xn)
        h3 = _split3(jnp.concatenate([top, bot], axis=0))
        trow = i * tm + s0 + lax.broadcasted_iota(jnp.int32, (sub, 1), 0)
        parts = []
        for gi, win in enumerate(POOL_WINDOWS):
            half = win // 2
            cols = slice(gi * gd, (gi + 1) * gd)
            band, edge = band_scr[gi], edges[gi]
            tot = _dot(band, x3[0][:, cols]) + _dot(band, x3[1][:, cols]) + _dot(band, x3[2][:, cols])
            fix = _dot(edge, h3[0][:, cols]) + _dot(edge, h3[1][:, cols]) + _dot(edge, h3[2][:, cols])
            tot = jnp.concatenate([tot[:hl] + fix[:hl], tot[hl:sub - hl], tot[sub - hl:] + fix[hl:]], axis=0)
            cnt = (jnp.minimum(trow + half, t_total) - jnp.maximum(trow - half, 0)).astype(F32)
            pooled = (tot / cnt - xn[:, cols]).astype(BF16)
            parts.append(_dot(pooled, pw_ref[gi]))
        y = jnp.concatenate(parts, axis=-1) * ps_ref[...]
        o_ref[0, s0:s0 + sub, :] = x + gate * y


def _pool(h, mod, g, pw, ps):
    bsz, t, d = h.shape
    tm = min(POOL_TILE, t)
    sub = min(POOL_SUBTILE, tm)
    per = tm // POOL_HALO
    last = t // POOL_HALO - 1
    return pl.pallas_call(
        functools.partial(_pool_kernel, sub=sub, t_total=t),
        grid=(bsz, t // tm),
        in_specs=[pl.BlockSpec((1, tm, d), lambda b, i: (b, i, 0)),
                  pl.BlockSpec((1, POOL_HALO, d), lambda b, i: (b, jnp.maximum(i * per - 1, 0), 0)),
                  pl.BlockSpec((1, POOL_HALO, d), lambda b, i: (b, jnp.minimum((i + 1) * per, last), 0)),
                  _mod_spec(mod, d),
                  pl.BlockSpec((1, d), lambda b, i: (0, 0)),
                  _resident(pw.shape),
                  pl.BlockSpec((1, d), lambda b, i: (0, 0))],
        out_specs=pl.BlockSpec((1, tm, d), lambda b, i: (b, i, 0)),
        out_shape=jax.ShapeDtypeStruct((bsz, t, d), F32),
        scratch_shapes=[pltpu.VMEM((len(POOL_WINDOWS), sub, sub), BF16)],
        compiler_params=_params("arbitrary", "arbitrary"),
        name="pool",
    )(h, h, h, mod, g.reshape(1, d), pw, ps.reshape(1, d))


def _swa_kernel(sink_ref, q_ref, k_ref, v_ref, kc_ref, vc_ref, o_ref, kt_scr, va_scr, sloc_a, scx_a, sloc_b, scx_b,
                *, t_total):
    kv = pl.program_id(1)
    blk = SWA_BLOCK
    nb = t_total // blk
    grp = SWA_Q_HEADS // SWA_KV_HEADS
    nrow = grp * blk
    _stage_transposed(kt_scr, k_ref, 0, t_total)
    kct = kc_ref[0].T
    _stage_with_ones(va_scr, v_ref)
    vca = _with_ones(vc_ref[0])
    row = lax.broadcasted_iota(jnp.int32, (nrow, 1), 0)
    sink = jnp.zeros((nrow, 1), F32)
    for gi in range(grp):
        sink = jnp.where((row >= gi * blk) & (row < (gi + 1) * blk), sink_ref[kv * grp + gi] * LOG2E, sink)
    qi = lax.broadcasted_iota(jnp.int32, (nrow, blk), 0) & (blk - 1)
    kj = lax.broadcasted_iota(jnp.int32, (nrow, blk), 1)
    open_blk = jnp.zeros((nrow, blk), F32)
    prev_blk = jnp.where(kj >= qi, 0.0, NEG_INF)
    next_blk = jnp.where(kj <= qi, 0.0, NEG_INF)

    shut_blk = jnp.full((nrow, blk), NEG_INF, F32)
    span = 3 * blk
    mask_first = jnp.concatenate([open_blk, next_blk, shut_blk], axis=1)
    mask_mid = jnp.concatenate([prev_blk, open_blk, next_blk], axis=1)
    mask_last = jnp.concatenate([shut_blk, prev_blk, open_blk], axis=1)

    def offsets(n):
        qs = pl.multiple_of(n * blk, blk)
        return qs, pl.multiple_of(jnp.clip(qs - blk, 0, t_total - span), blk)

    def scores(n, mask, sloc_ref, scx_ref):
        qs, ks = offsets(n)
        qb = q_ref[0, pl.ds(qs, blk), :]
        q4 = jnp.concatenate([_split_heads(qb[:, :LANES]), _split_heads(qb[:, LANES:])], axis=0)
        sloc_ref[...] = _dot(q4, kt_scr[:, pl.ds(ks, span)]) + mask
        scx_ref[...] = _dot(q4, kct)

    def finish(n, sloc_ref, scx_ref):
        qs, ks = offsets(n)
        s_loc, s_cx = sloc_ref[...], scx_ref[...]
        m = jnp.maximum(jnp.maximum(jnp.max(s_loc, axis=-1, keepdims=True),
                                    jnp.max(s_cx, axis=-1, keepdims=True)), sink)
        p_loc = jnp.exp2(s_loc - m)
        p_cx = jnp.exp2(s_cx - m)
        oa = _dot(p_loc.astype(BF16), va_scr[pl.ds(ks, span), :]) + _dot(p_cx.astype(BF16), vca)
        o = oa[:, :LANES] / (oa[:, LANES:] + jnp.exp2(sink - m))
        out = jnp.concatenate([_merge_heads(o[:2 * blk]), _merge_heads(o[2 * blk:])], axis=-1)
        o_ref[0, pl.ds(qs, blk), :] = out.astype(BF16)

    slot_a, slot_b = (sloc_a, scx_a), (sloc_b, scx_b)
    scores(0, mask_first, *slot_a)
    scores(1, mask_mid, *slot_b)
    finish(0, *slot_a)

    def body(i, carry):
        n = 1 + 2 * i
        scores(n + 1, mask_mid, *slot_a)
        finish(n, *slot_b)
        scores(n + 2, mask_mid, *slot_b)
        finish(n + 1, *slot_a)
        return carry

    lax.fori_loop(0, (nb - 4) // 2, body, 0, unroll=7)
    scores(nb - 2, mask_mid, *slot_a)
    finish(nb - 3, *slot_b)
    scores(nb - 1, mask_last, *slot_b)
    finish(nb - 2, *slot_a)
    finish(nb - 1, *slot_b)


def _swa_attention(sink, q, k, v, kc, vc):
    bsz, t, d = q.shape
    l = kc.shape[1]
    nb = t // SWA_BLOCK
    assert nb >= 4 and nb % 2 == 0 and SWA_WINDOW == SWA_BLOCK
    qw = d // SWA_KV_HEADS
    grid_spec = pltpu.PrefetchScalarGridSpec(
        num_scalar_prefetch=1,
        grid=(bsz, SWA_KV_HEADS),
        in_specs=[pl.BlockSpec((1, t, qw), lambda b, h, s: (b, 0, h)),
                  pl.BlockSpec((1, t, LANES), lambda b, h, s: (b, 0, h)),
                  pl.BlockSpec((1, t, LANES), lambda b, h, s: (b, 0, h)),
                  pl.BlockSpec((1, l, LANES), lambda b, h, s: (b, 0, h)),
                  pl.BlockSpec((1, l, LANES), lambda b, h, s: (b, 0, h))],
        out_specs=pl.BlockSpec((1, t, qw), lambda b, h, s: (b, 0, h)),
        scratch_shapes=[pltpu.VMEM((LANES, t), BF16), pltpu.VMEM((t, 2 * LANES), BF16)]
        + [pltpu.VMEM((SWA_Q_HEADS // SWA_KV_HEADS * SWA_BLOCK, n), F32) for n in (3 * SWA_BLOCK, l)] * 2,
    )
    return pl.pallas_call(
        functools.partial(_swa_kernel, t_total=t),
        grid_spec=grid_spec,
        out_shape=jax.ShapeDtypeStruct((bsz, t, d), BF16),
        compiler_params=_params("parallel", "parallel"),
        name="swa_attention",
    )(sink, q, k, v, kc, vc)


def _rope_angles(positions, dim):
    seg = dim // len(positions)
    inv = ROPE_BASE ** (-np.arange(0, seg, 2, dtype=np.float64) / seg)
    return np.concatenate([np.tile(p.astype(np.float64)[:, None] * inv, (1, 2)) for p in positions], axis=-1)


def _table(x):
    return jnp.asarray(x.astype(np.float32))


def _dup_heads(w, heads, dh):
    d = w.shape[0]
    return jnp.broadcast_to(w.reshape(d, heads, 1, dh), (d, heads, 2, dh)).reshape(d, heads * 2 * dh)


def kernel(x, c, ctx, c_ctx, w_mod, b_mod, norm_g, ffn_w_in, ffn_w_out, ret_w_in, ret_w_out, ret_gn_g, ret_decay_f, ret_decay_b, nat_w_qkv, nat_w_o, nat_rpb, pool_w, pool_scale, swa_w_qkv, swa_w_o, swa_sink, final_norm_g):
    bsz, t, d = x.shape
    depth = w_mod.shape[0]
    cc = jnp.concatenate([c, c_ctx[None], jnp.zeros((SUBLANES - bsz - 1, d), F32)], axis=0)
    mods = _modulation(cc, w_mod, b_mod)
    h, hc = x, ctx
    for i in range(depth):
        kind, occ = i % N_MIXERS, i // N_MIXERS
        last = i == depth - 1
        ctx_live = (not last) or kind != 2
        ml = mods[i, :bsz].reshape(bsz, MOD_ROWS, d)
        mc = mods[i, bsz:bsz + 1].reshape(1, MOD_ROWS, d)
        if ctx_live:
            hc, w_in, w_out = _ffn_ctx(hc, mc, norm_g[i, 0], ffn_w_in, ffn_w_out, i, 0)
        else:
            w_in, w_out = _ffn_weights_bf16(ffn_w_in, ffn_w_out, i, 0)
        h = _ffn(h, ml, norm_g[i, 0], w_in, w_out, 0)
        g1 = norm_g[i, 1]
        yl = yc = None
        if kind == 0:
            assert not last, "retention as the last layer is not wired up"
            w = ret_w_in[occ].astype(BF16)
            wo = ret_w_out[occ].astype(BF16)
            ang = _rope_angles([np.arange(t)], RET_QK_DIM)[:, :RET_QK_DIM // 2]
            dec_f = jnp.broadcast_to(ret_decay_f[occ].astype(F32)[:, None, None], (RET_HEADS, 1, LANES))
            dec_b = jnp.broadcast_to(ret_decay_b[occ].astype(F32)[:, None, None], (RET_HEADS, 1, LANES))
            qc, kc, vc, gc = _ret_proj(hc, mc, g1, w)
            zeros = jnp.zeros((bsz, RET_HEADS, RET_QK_DIM, RET_V_DIM), F32)
            oc, s_f, s_b = _ret_mix(dec_f, dec_b, qc, kc, vc, gc, ret_gn_g[occ], zeros, zeros)
            ql, kl, vl, gl = _ret_proj(h, ml, g1, w, _table(np.cos(ang)), _table(np.sin(ang)))
            ol, _, _ = _ret_mix(dec_f, dec_b, ql, kl, vl, gl, ret_gn_g[occ], s_f, s_b)
            yl, yc = (ol, wo), (oc, wo)
        elif kind == 1:
            w = nat_w_qkv[occ].astype(BF16)
            wo = nat_w_o[occ].astype(BF16)
            widths = (d, d, d)
            qc, kc, vc = _qkv_proj(hc, mc, g1, w, widths, NAT_HEAD_DIM ** -0.5 * LOG2E, want_q=not last)
            ql, kl, vl = _qkv_proj(h, ml, g1, w, widths, NAT_HEAD_DIM ** -0.5 * LOG2E)
            yl = (_nat_attention(ql, kl, vl, kc, vc, _nat_bias(nat_rpb[occ])), wo)
            if not last:
                yc = (_ctx_attention(qc, kc, vc), wo)
        elif kind == 2:
            pw = pool_w[occ].astype(BF16)
            h_new = _pool(h, ml, g1, pw, pool_scale[occ])
            if not last:
                hc = _pool(hc, mc, g1, pw, pool_scale[occ])
            h = h_new
        else:
            assert last, "windowed attention with live context outputs is not wired up"
            nq = SWA_Q_HEADS * SWA_HEAD_DIM
            nkv = SWA_KV_HEADS * SWA_HEAD_DIM
            wq = swa_w_qkv[occ][:, :nq]
            wk = _dup_heads(swa_w_qkv[occ][:, nq:nq + nkv], SWA_KV_HEADS, SWA_HEAD_DIM)
            wv = _dup_heads(swa_w_qkv[occ][:, nq + nkv:], SWA_KV_HEADS, SWA_HEAD_DIM)
            w = jnp.concatenate([wq, wk, wv], axis=-1).astype(BF16)
            wo = swa_w_o[occ].astype(BF16)
            widths = (nq, 2 * nkv, 2 * nkv)
            tt = np.arange(t)
            ang = _rope_angles([tt // GRID_W, tt % GRID_W], SWA_HEAD_DIM)
            ang = np.tile(ang, (1, LANES // SWA_HEAD_DIM))
            lane = np.arange(LANES)
            sin = np.where(lane % 32 < 16, -np.sin(ang), np.sin(ang))
            _, kc, vc = _qkv_proj(hc, mc, g1, w, widths, SWA_HEAD_DIM ** -0.5 * LOG2E, want_q=False)
            ql, kl, vl = _qkv_proj(h, ml, g1, w, widths, SWA_HEAD_DIM ** -0.5 * LOG2E, _table(np.cos(ang)), _table(sin))
            yl = (_swa_attention(swa_sink[occ].astype(F32), ql, kl, vl, kc, vc), wo)
        if not last:
            hc, w_in, w_out = _ffn_ctx(hc, mc, norm_g[i, 2], ffn_w_in, ffn_w_out, i, 1, yc)
        else:
            w_in, w_out = _ffn_weights_bf16(ffn_w_in, ffn_w_out, i, 1)
        h = _ffn(h, ml, norm_g[i, 2], w_in, w_out, 1, yl, final_norm_g if last else None)
    return h
```

```python
import functools
import math

import jax
import jax.numpy as jnp
import numpy as np
from jax import lax
from jax.experimental import pallas as pl
from jax.experimental.pallas import tpu as pltpu

F32 = jnp.float32
BF16 = jnp.bfloat16

EPS = 1e-6
NEG_INF = -1e30
LOG2E = math.log2(math.e)
ROPE_BASE = 10000.0
GRID_W = 64
N_MIXERS = 4
FFN_HIDDEN = 2816
RET_HEADS = 4
RET_QK_DIM = 256
RET_V_DIM = 512
NAT_HEAD_DIM = 64
NAT_KH = 8
NAT_KW = 16
POOL_WINDOWS = (2, 4, 8, 16)
SWA_Q_HEADS = 16
SWA_KV_HEADS = 4
SWA_HEAD_DIM = 64
SWA_WINDOW = 128
SWA_BLOCK = 128
HEAD_DIM = 64
assert NAT_HEAD_DIM == SWA_HEAD_DIM == HEAD_DIM
MOD_ROWS = 9

LANES = 128
SUBLANES = 8
VMEM_LIMIT = 56 * 1024 * 1024
MOD_TILE = 2304
POOL_TILE = 512
POOL_SUBTILE = 256
PROJ_TILE = 512
FFN_TILE = 1024
FFN_SUBTILE = 128
FFN_FUSED_SUBTILE = 512
PROJ_SUBTILE = 256
CTX_FFN_CHUNK = 256
RET_CHUNK = 256
XPOSE_CHUNK = 512


def _params(*sem):
    return pltpu.CompilerParams(dimension_semantics=sem, vmem_limit_bytes=VMEM_LIMIT)


def _resident(shape):
    nd = len(shape)
    return pl.BlockSpec(shape, lambda *_: (0,) * nd, pipeline_mode=pl.Buffered(1))


def _silu(x):
    return x * jax.nn.sigmoid(x)


def _ada(x, g, shift, scale):
    var = jnp.mean(x * x, axis=-1, keepdims=True)
    y = x * lax.rsqrt(var + EPS) * g
    return y * (1.0 + scale) + shift


def _mod_rows(mod_ref, j):
    return (mod_ref[0, 3 * j:3 * j + 1, :], mod_ref[0, 3 * j + 1:3 * j + 2, :],
            mod_ref[0, 3 * j + 2:3 * j + 3, :])


def _mod_spec(mod, d):
    if mod.shape[0] == 1:
        return pl.BlockSpec((1, MOD_ROWS, d), lambda b, t: (0, 0, 0))
    return pl.BlockSpec((1, MOD_ROWS, d), lambda b, t: (b, 0, 0))


def _dot(a, b):
    return jnp.dot(a, b, preferred_element_type=F32)


def _dot_tn(a, b):
    return lax.dot_general(a, b, (((0,), (0,)), ((), ())), preferred_element_type=F32)


def _mod_kernel(c_ref, w_ref, b_ref, o_ref):
    s = _silu(c_ref[...]).astype(BF16)
    o_ref[0] = _dot(s, w_ref[0].astype(BF16)) + b_ref[0]


def _modulation(cc, w_mod, b_mod):
    depth, d, n = w_mod.shape
    tn = MOD_TILE
    return pl.pallas_call(
        _mod_kernel,
        grid=(depth, n // tn),
        in_specs=[pl.BlockSpec((SUBLANES, d), lambda l, j: (0, 0)),
                  pl.BlockSpec((1, d, tn), lambda l, j: (l, 0, j)),
                  pl.BlockSpec((1, 1, tn), lambda l, j: (l, 0, j))],
        out_specs=pl.BlockSpec((1, SUBLANES, tn), lambda l, j: (l, 0, j)),
        out_shape=jax.ShapeDtypeStruct((depth, SUBLANES, n), F32),
        compiler_params=_params("arbitrary", "arbitrary"),
        name="modulation",
    )(cc, w_mod, b_mod.reshape(depth, 1, n))


def _ffn_kernel(h_ref, mod_ref, g_ref, win_ref, wout_ref, *rest, j, mixer_out, final):
    o_ref = rest[-1]
    shift, scale, gate = _mod_rows(mod_ref, j)
    tm = h_ref.shape[1]
    sub = min(tm, FFN_FUSED_SUBTILE if mixer_out else FFN_SUBTILE)
    for r0 in range(0, tm, sub):
        x = h_ref[0, r0:r0 + sub, :]
        if mixer_out:
            y_ref, wo_ref = rest[:2]
            x = x + _mod_rows(mod_ref, 1)[2] * _dot(y_ref[0, r0:r0 + sub, :], wo_ref[...])
        xb = _ada(x, g_ref[...], shift, scale).astype(BF16)
        if len(win_ref.shape) == 3:
            wg, wv = win_ref[0], win_ref[1]
        else:
            f = win_ref.shape[1] // 2
            wg, wv = win_ref[:, :f], win_ref[:, f:]
        hid = (_silu(_dot(xb, wg)) * _dot(xb, wv)).astype(BF16)
        out = x + (0.5 * gate) * _dot(hid, wout_ref[...])
        if final:
            var = jnp.mean(out * out, axis=-1, keepdims=True)
            out = out * lax.rsqrt(var + EPS) * rest[-2][...]
        o_ref[0, r0:r0 + sub, :] = out


def _ffn(h, mod, g, w_in, w_out, half, mixer_out=None, final_g=None):
    bsz, t, d = h.shape
    tm = min(FFN_TILE, t)
    row = lambda n: pl.BlockSpec((1, tm, n), lambda b, i: (b, i, 0))
    vec = pl.BlockSpec((1, d), lambda b, i: (0, 0))
    in_specs = [row(d), _mod_spec(mod, d), vec, _resident(w_in.shape), _resident(w_out.shape)]
    args = [h, mod, g.reshape(1, d), w_in, w_out]
    if mixer_out is not None:
        y, wo = mixer_out
        in_specs += [row(y.shape[-1]), _resident(wo.shape)]
        args += [y, wo]
    if final_g is not None:
        in_specs.append(vec)
        args.append(final_g.reshape(1, d))
    return pl.pallas_call(
        functools.partial(_ffn_kernel, j=2 * half, mixer_out=mixer_out is not None, final=final_g is not None),
        grid=(bsz, t // tm),
        in_specs=in_specs,
        out_specs=row(d),
        out_shape=jax.ShapeDtypeStruct((bsz, t, d), F32),
        compiler_params=_params("parallel", "parallel"),
        name="ffn",
    )(*args)


def _ffn_ctx_kernel(h_ref, mod_ref, g_ref, wa_ref, wb_ref, wo_ref, *rest, j, mixer_out):
    o_ref, win_bf_ref, wout_bf_ref, x_scr, xb_scr, acc_scr = rest[-6:]
    s = pl.program_id(0)
    shift, scale, gate = _mod_rows(mod_ref, j)

    @pl.when(s == 0)
    def _():
        x = h_ref[...]
        if mixer_out:
            y_ref, wmix_ref = rest[:2]
            x = x + _mod_rows(mod_ref, 1)[2] * _dot(y_ref[...], wmix_ref[...])
        x_scr[...] = x
        xb_scr[...] = _ada(x, g_ref[...], shift, scale).astype(BF16)
        acc_scr[...] = jnp.zeros_like(acc_scr)

    wa, wb, wo = wa_ref[0, 0].astype(BF16), wb_ref[0, 0].astype(BF16), wo_ref[0, 0].astype(BF16)
    win_bf_ref[0] = wa
    win_bf_ref[1] = wb
    wout_bf_ref[...] = wo
    xb = xb_scr[...]
    hid = (_silu(_dot(xb, wa)) * _dot(xb, wb)).astype(BF16)
    acc_scr[...] += _dot(hid, wo)

    @pl.when(s == pl.num_programs(0) - 1)
    def _():
        o_ref[...] = x_scr[...] + (0.5 * gate) * acc_scr[...]


def _ffn_ctx(hc, mod, g, w_in, w_out, layer, half, mixer_out=None):
    bsz, l, d = hc.shape
    f = w_out.shape[2]
    fc = CTX_FFN_CHUNK
    nf = f // fc
    n = bsz * l
    const = lambda shape: pl.BlockSpec(shape, lambda s: (0,) * len(shape))
    in_specs = [const((n, d)), const((1, MOD_ROWS, d)), const((1, d)),
                pl.BlockSpec((1, 1, d, fc), lambda s: (layer, half, 0, s)),
                pl.BlockSpec((1, 1, d, fc), lambda s: (layer, half, 0, nf + s)),
                pl.BlockSpec((1, 1, fc, d), lambda s: (layer, half, s, 0))]
    args = [hc.reshape(n, d), mod, g.reshape(1, d), w_in, w_in, w_out]
    if mixer_out is not None:
        y, wmix = mixer_out
        in_specs += [const((n, y.shape[-1])), _resident(wmix.shape)]
        args += [y.reshape(n, y.shape[-1]), wmix]
    out, w_in_bf, w_out_bf = pl.pallas_call(
        functools.partial(_ffn_ctx_kernel, j=2 * half, mixer_out=mixer_out is not None),
        grid=(nf,),
        in_specs=in_specs,
        out_specs=[const((n, d)),
                   pl.BlockSpec((2, d, fc), lambda s: (0, 0, s)),
                   pl.BlockSpec((fc, d), lambda s: (s, 0))],
        out_shape=[jax.ShapeDtypeStruct((n, d), F32),
                   jax.ShapeDtypeStruct((2, d, f), BF16),
                   jax.ShapeDtypeStruct((f, d), BF16)],
        scratch_shapes=[pltpu.VMEM((n, d), F32), pltpu.VMEM((n, d), BF16), pltpu.VMEM((n, d), F32)],
        compiler_params=_params("arbitrary"),
        name="ffn_ctx",
    )(*args)
    return out.reshape(bsz, l, d), w_in_bf, w_out_bf


def _ffn_weights_bf16(w_in, w_out, layer, half):
    return w_in[layer, half].astype(BF16), w_out[layer, half].astype(BF16)


def _ret_proj_kernel(h_ref, mod_ref, g_ref, w_ref, *rest, rope):
    if rope:
        cos_ref, sin_ref, q_ref, k_ref, v_ref, gate_ref = rest
    else:
        q_ref, k_ref, v_ref, gate_ref = rest
    shift, scale, _ = _mod_rows(mod_ref, 1)
    dk, nh = RET_QK_DIM, RET_HEADS
    half = dk // 2
    v0 = 2 * nh * dk
    nv = nh * RET_V_DIM
    tm = h_ref.shape[1]
    sub = min(tm, PROJ_SUBTILE)
    for r0 in range(0, tm, sub):
        rows = slice(r0, r0 + sub)
        xb = _ada(h_ref[0, rows, :], g_ref[...], shift, scale).astype(BF16)

        def rot(x):
            if not rope:
                return x.astype(BF16)
            c, s = cos_ref[rows, :], sin_ref[rows, :]
            x1, x2 = x[:, :half], x[:, half:]
            return jnp.concatenate([x1 * c - x2 * s, x2 * c + x1 * s], axis=-1).astype(BF16)

        for hd in range(nh):
            q_ref[0, rows, hd * dk:(hd + 1) * dk] = rot(_dot(xb, w_ref[:, hd * dk:(hd + 1) * dk]))
            kcol = nh * dk + hd * dk
            k_ref[0, rows, hd * dk:(hd + 1) * dk] = rot(_dot(xb, w_ref[:, kcol:kcol + dk]) * dk ** -0.5)
        v_ref[0, rows, :] = _dot(xb, w_ref[:, v0:v0 + nv]).astype(BF16)
        gate_ref[0, rows, :] = _silu(_dot(xb, w_ref[:, v0 + nv:v0 + 2 * nv])).astype(BF16)


def _ret_proj(h, mod, g, w, cos=None, sin=None):
    bsz, t, d = h.shape
    tm = min(PROJ_TILE, t)
    rope = cos is not None
    nqk = RET_HEADS * RET_QK_DIM
    nv = RET_HEADS * RET_V_DIM
    row = lambda n: pl.BlockSpec((1, tm, n), lambda b, i: (b, i, 0))
    in_specs = [row(d), _mod_spec(mod, d), pl.BlockSpec((1, d), lambda b, i: (0, 0)), _resident(w.shape)]
    args = [h, mod, g.reshape(1, d), w]
    if rope:
        in_specs += [pl.BlockSpec((tm, RET_QK_DIM // 2), lambda b, i: (i, 0))] * 2
        args += [cos, sin]
    return pl.pallas_call(
        functools.partial(_ret_proj_kernel, rope=rope),
        grid=(bsz, t // tm),
        in_specs=in_specs,
        out_specs=[row(nqk), row(nqk), row(nv), row(nv)],
        out_shape=[jax.ShapeDtypeStruct((bsz, t, nqk), BF16), jax.ShapeDtypeStruct((bsz, t, nqk), BF16),
                   jax.ShapeDtypeStruct((bsz, t, nv), BF16), jax.ShapeDtypeStruct((bsz, t, nv), BF16)],
        compiler_params=_params("parallel", "parallel"),
        name="ret_proj",
    )(*args)


def _log_sigmoid(x):
    return jnp.minimum(x, 0.0) - jnp.log(1.0 + jnp.exp(-jnp.abs(x)))


def _ret_mix_kernel(decf_ref, decb_ref, q_ref, k_ref, v_ref, g_ref, gn_ref, s0f_ref, s0b_ref,
                    y_ref, sf_ref, sb_ref, s_scr, sb_scr, *, chunk):
    c = chunk
    nc = q_ref.shape[1] // c
    lgf = _log_sigmoid(decf_ref[0])[:, :1]
    lgb = _log_sigmoid(decb_ref[0])[:, :1]
    dist = lax.broadcasted_iota(jnp.int32, (c, c), 0) - lax.broadcasted_iota(jnp.int32, (c, c), 1)
    decay = jnp.where(dist >= 0,
                      jnp.exp(jnp.maximum(dist, 0).astype(F32) * lgf),
                      jnp.exp(jnp.maximum(-dist, 0).astype(F32) * lgb))
    pos = lax.broadcasted_iota(jnp.int32, (c, 1), 0).astype(F32)
    qd_f, kd_f, cd_f = jnp.exp((pos + 1.0) * lgf), jnp.exp((c - 1.0 - pos) * lgf), jnp.exp(c * lgf)
    qd_b, kd_b, cd_b = jnp.exp((c - pos) * lgb), jnp.exp(pos * lgb), jnp.exp(c * lgb)

    def rows(n):
        return pl.ds(pl.multiple_of(n * c, c), c)

    s_scr[...] = s0b_ref[0, 0]

    def bwd(i, carry):
        n = nc - 1 - i
        s = s_scr[...]
        sb_scr[n] = s.astype(BF16)
        kd = (k_ref[0, rows(n), :].astype(F32) * kd_b).astype(BF16)
        s_scr[...] = s * cd_b + _dot_tn(kd, v_ref[0, rows(n), :])
        return carry

    lax.fori_loop(0, nc, bwd, 0, unroll=min(8, nc))
    sb_ref[0, 0] = s_scr[...]
    s_scr[...] = s0f_ref[0, 0]

    def fwd(n, carry):
        q, k, v = q_ref[0, rows(n), :], k_ref[0, rows(n), :], v_ref[0, rows(n), :]
        qf, kf = q.astype(F32), k.astype(F32)
        s = s_scr[...]
        att = (_dot(q, k.T) * decay).astype(BF16)
        o = (_dot(att, v) + _dot((qf * qd_f).astype(BF16), s.astype(BF16))
             + _dot((qf * qd_b).astype(BF16), sb_scr[n]))
        mu = jnp.mean(o, axis=-1, keepdims=True)
        ctr = o - mu
        var = jnp.mean(ctr * ctr, axis=-1, keepdims=True)
        on = ctr * lax.rsqrt(var + EPS) * gn_ref[...]
        y_ref[0, rows(n), :] = (g_ref[0, rows(n), :].astype(F32) * on).astype(BF16)
        s_scr[...] = s * cd_f + _dot_tn((kf * kd_f).astype(BF16), v)
        return carry

    lax.fori_loop(0, nc, fwd, 0, unroll=min(8, nc))
    sf_ref[0, 0] = s_scr[...]


def _ret_mix(dec_f, dec_b, q, k, v, gate, gn_g, s0_f, s0_b):
    bsz, t, _ = q.shape
    nh, dk, dv = RET_HEADS, RET_QK_DIM, RET_V_DIM
    chunk = min(RET_CHUNK, t)
    seq = lambda n: pl.BlockSpec((1, t, n), lambda b, h: (b, 0, h))
    dec = pl.BlockSpec((1, 1, LANES), lambda b, h: (h, 0, 0))
    state = pl.BlockSpec((1, 1, dk, dv), lambda b, h: (b, h, 0, 0))
    return pl.pallas_call(
        functools.partial(_ret_mix_kernel, chunk=chunk),
        grid=(bsz, nh),
        in_specs=[dec, dec, seq(dk), seq(dk), seq(dv), seq(dv),
                  pl.BlockSpec((1, dv), lambda b, h: (0, h)), state, state],
        out_specs=[seq(dv), state, state],
        out_shape=[jax.ShapeDtypeStruct((bsz, t, nh * dv), BF16),
                   jax.ShapeDtypeStruct((bsz, nh, dk, dv), F32),
                   jax.ShapeDtypeStruct((bsz, nh, dk, dv), F32)],
        scratch_shapes=[pltpu.VMEM((dk, dv), F32), pltpu.VMEM((t // chunk, dk, dv), BF16)],
        compiler_params=_params("parallel", "parallel"),
        name="ret_mix",
    )(dec_f, dec_b, q, k, v, gate, gn_g.reshape(1, nh * dv), s0_f, s0_b)


def _qkv_proj_kernel(h_ref, mod_ref, g_ref, w_ref, *rest, widths, q_scale, rope):
    if rope:
        cos_ref, sin_ref = rest[:2]
        outs = rest[2:]
    else:
        outs = rest
    shift, scale, _ = _mod_rows(mod_ref, 1)
    tm = h_ref.shape[1]
    sub = min(tm, PROJ_SUBTILE)
    for r0 in range(0, tm, sub):
        rows = slice(r0, r0 + sub)
        xb = _ada(h_ref[0, rows, :], g_ref[...], shift, scale).astype(BF16)
        col = 0
        for idx, (o_ref, n) in enumerate(zip(outs, widths)):
            if o_ref is not None:
                y = _dot(xb, w_ref[:, col:col + n])
                for c0 in range(0, n, LANES):
                    yc = y[:, c0:c0 + LANES]
                    if rope and idx < 2:
                        lane = lax.broadcasted_iota(jnp.int32, yc.shape, 1)
                        partner = jnp.where((lane & 16) == 0, pltpu.roll(yc, LANES - 16, axis=1),
                                            pltpu.roll(yc, 16, axis=1))
                        yc = yc * cos_ref[rows, :] + partner * sin_ref[rows, :]
                    if idx == 0:
                        yc = yc * q_scale
                    o_ref[0, rows, c0:c0 + LANES] = yc.astype(BF16)
            col += n


def _qkv_proj(h, mod, g, w, widths, q_scale, cos=None, sin=None, want_q=True):
    bsz, t, d = h.shape
    tm = min(PROJ_TILE, t)
    rope = cos is not None
    row = lambda n: pl.BlockSpec((1, tm, n), lambda b, i: (b, i, 0))
    in_specs = [row(d), _mod_spec(mod, d), pl.BlockSpec((1, d), lambda b, i: (0, 0)), _resident(w.shape)]
    args = [h, mod, g.reshape(1, d), w]
    if rope:
        in_specs += [pl.BlockSpec((tm, LANES), lambda b, i: (i, 0))] * 2
        args += [cos, sin]
    keep = [want_q, True, True]
    out_widths = [n for n, kp in zip(widths, keep) if kp]

    def body(*refs):
        n_in = len(args)
        outs = list(refs[n_in:])
        full = [outs.pop(0) if kp else None for kp in keep]
        _qkv_proj_kernel(*refs[:n_in], *full, widths=widths, q_scale=q_scale, rope=rope)

    res = pl.pallas_call(
        body,
        grid=(bsz, t // tm),
        in_specs=in_specs,
        out_specs=[row(n) for n in out_widths],
        out_shape=[jax.ShapeDtypeStruct((bsz, t, n), BF16) for n in out_widths],
        compiler_params=_params("parallel", "parallel"),
        name="qkv_proj",
    )(*args)
    return res if want_q else [None] + list(res)


def _split_heads(x):
    lane = lax.broadcasted_iota(jnp.int32, x.shape, 1)
    zero = jnp.zeros_like(x)
    return jnp.concatenate([jnp.where(lane < HEAD_DIM, x, zero), jnp.where(lane >= HEAD_DIM, x, zero)], axis=0)


def _merge_heads(o):
    r = o.shape[0] // 2
    lane = lax.broadcasted_iota(jnp.int32, (r, o.shape[1]), 1)
    return jnp.where(lane < HEAD_DIM, o[:r], o[r:])


def _with_ones(v):
    return jnp.concatenate([v, jnp.ones_like(v)], axis=1)


def _stage_with_ones(dst_ref, src_ref):
    for c0 in range(0, src_ref.shape[1], XPOSE_CHUNK):
        n = min(XPOSE_CHUNK, src_ref.shape[1] - c0)
        dst_ref[c0:c0 + n, :] = _with_ones(src_ref[0, c0:c0 + n, :])


def _stage_transposed(dst_ref, src_ref, row0, width):
    for c0 in range(0, width, XPOSE_CHUNK):
        n = min(XPOSE_CHUNK, width - c0)
        dst_ref[:, c0:c0 + n] = src_ref[0, row0 + c0:row0 + c0 + n, :].T


def _nat_kernel(q_ref, k_ref, v_ref, kc_ref, vc_ref, bias_ref, o_ref, kt_scr, va_scr, snb_a, scx_a, snb_b, scx_b,
                *, rows):
    w = GRID_W
    kh = min(NAT_KH, rows)
    nk = kh * w
    t = rows * w
    _stage_transposed(kt_scr.at[0], k_ref, 0, t)
    _stage_transposed(kt_scr.at[1], k_ref, w, t - 2 * w)
    kct = kc_ref[0].T
    _stage_with_ones(va_scr, v_ref)
    vca = _with_ones(vc_ref[0])

    def window(r):
        r0 = jnp.clip(r - kh // 2, 0, rows - kh)
        return r0, pl.multiple_of(r * w, w)

    def scores(r, snb_ref, scx_ref):
        r0, qs = window(r)
        ty = r - r0
        par = r0 & 1
        kts = pl.multiple_of((r0 - par) * w, 2 * w)
        q2 = _split_heads(q_ref[0, pl.ds(qs, w), :])
        bias = jnp.concatenate(
            [jnp.concatenate([bias_ref[hd, 2 * m - ty + NAT_KH - 1] for m in range(kh // 2)], axis=1)
             for hd in range(2)], axis=0)
        snb_ref[...] = _dot(q2, kt_scr[par, :, pl.ds(kts, nk)]) + bias
        scx_ref[...] = _dot(q2, kct)

    def finish(r, snb_ref, scx_ref):
        r0, qs = window(r)
        ks = pl.multiple_of(r0 * w, w)
        s_nb, s_cx = snb_ref[...], scx_ref[...]
        m = jnp.maximum(jnp.max(s_nb, axis=-1, keepdims=True), jnp.max(s_cx, axis=-1, keepdims=True))
        p_nb = jnp.exp2(s_nb - m)
        p_cx = jnp.exp2(s_cx - m)
        oa = _dot(p_nb.astype(BF16), va_scr[pl.ds(ks, nk), :]) + _dot(p_cx.astype(BF16), vca)
        o_ref[0, pl.ds(qs, w), :] = _merge_heads(oa[:, :LANES] / oa[:, LANES:]).astype(BF16)

    scores(0, snb_a, scx_a)

    def body(i, carry):
        r = 2 * i
        scores(r + 1, snb_b, scx_b)
        finish(r, snb_a, scx_a)
        scores(jnp.minimum(r + 2, rows - 1), snb_a, scx_a)
        finish(r + 1, snb_b, scx_b)
        return carry

    lax.fori_loop(0, rows // 2, body, 0, unroll=16)


def _nat_attention(q, k, v, kc, vc, bias):
    bsz, t, d = q.shape
    l = kc.shape[1]
    rows = t // GRID_W
    hp = d // LANES
    lat = pl.BlockSpec((1, t, LANES), lambda p, b: (b, 0, p))
    ctx = pl.BlockSpec((1, l, LANES), lambda p, b: (b, 0, p))
    return pl.pallas_call(
        functools.partial(_nat_kernel, rows=rows),
        grid=(hp, bsz),
        in_specs=[lat, lat, lat, ctx, ctx,
                  pl.BlockSpec((2,) + bias.shape[1:], lambda p, b: (p, 0, 0, 0))],
        out_specs=lat,
        out_shape=jax.ShapeDtypeStruct((bsz, t, d), BF16),
        scratch_shapes=[pltpu.VMEM((2, LANES, t), BF16), pltpu.VMEM((t, 2 * LANES), BF16)]
        + [pltpu.VMEM((2 * GRID_W, n), F32) for n in (min(NAT_KH, rows) * GRID_W, l)] * 2,
        compiler_params=_params("parallel", "parallel"),
        name="nat_attention",
    )(q, k, v, kc, vc, bias)


def _nat_bias_kernel(x_ref, o_ref, *, tn):
    w = GRID_W
    sh = w.bit_length() - 1
    assert 1 << sh == w
    k2 = x_ref.shape[1]
    n = pl.program_id(0) * tn + lax.broadcasted_iota(jnp.int32, (1, tn), 1)
    c, a, cc = n >> (sh + 1), (n >> sh) & 1, n & (w - 1)
    j = lax.broadcasted_iota(jnp.int32, (k2, 1), 0)
    sel = jnp.where(((j >> (sh + 1)) == a) & ((j & (2 * w - 1)) == cc - c + w - 1), 1.0, 0.0).astype(BF16)
    hi, mid, lo = _split3(x_ref[...])
    vals = _dot(hi, sel) + _dot(mid, sel) + _dot(lo, sel)
    c0 = jnp.clip(c - NAT_KW // 2, 0, w - NAT_KW)
    o_ref[...] = jnp.where((cc >= c0) & (cc < c0 + NAT_KW), vals * LOG2E, NEG_INF)


def _nat_bias(rpb):
    w = GRID_W
    nh, nr, ncol = rpb.shape
    left = w - NAT_KW
    v = jnp.pad(rpb.astype(F32), ((0, 0), (0, 0), (left, 2 * w - ncol - left)))
    pairs = jnp.concatenate([v[:, :-1], v[:, 1:]], axis=-1).reshape(nh * (nr - 1), 4 * w)
    tn = 16 * LANES
    out = pl.pallas_call(
        functools.partial(_nat_bias_kernel, tn=tn),
        grid=(w * 2 * w // tn,),
        in_specs=[pl.BlockSpec(pairs.shape, lambda i: (0, 0))],
        out_specs=pl.BlockSpec((pairs.shape[0], tn), lambda i: (0, i)),
        out_shape=jax.ShapeDtypeStruct((pairs.shape[0], w * 2 * w), F32),
        compiler_params=_params("parallel"),
        name="nat_bias",
    )(pairs)
    return out.reshape(nh, nr - 1, w, 2 * w)


def _ctx_attn_kernel(q_ref, k_ref, v_ref, o_ref):
    for c0 in range(0, q_ref.shape[2], LANES):
        cols = slice(c0, c0 + LANES)
        q2 = _split_heads(q_ref[0, :, cols])
        s = _dot(q2, k_ref[0, :, cols].T)
        p = jnp.exp2(s - jnp.max(s, axis=-1, keepdims=True))
        l = jnp.sum(p, axis=-1, keepdims=True)
        o_ref[0, :, cols] = _merge_heads(_dot(p.astype(BF16), v_ref[0, :, cols]) / l).astype(BF16)


def _ctx_attention(q, k, v):
    bsz, l, d = q.shape
    blk = pl.BlockSpec((1, l, d), lambda b: (b, 0, 0))
    return pl.pallas_call(
        _ctx_attn_kernel,
        grid=(bsz,),
        in_specs=[blk, blk, blk],
        out_specs=blk,
        out_shape=jax.ShapeDtypeStruct((bsz, l, d), BF16),
        compiler_params=_params("parallel"),
        name="ctx_attention",
    )(q, k, v)


POOL_HALO = SUBLANES


def _split3(x):
    hi = x.astype(BF16)
    r1 = x - hi.astype(F32)
    mid = r1.astype(BF16)
    return hi, mid, (r1 - mid.astype(F32)).astype(BF16)


def _pool_kernel(h_ref, prev_ref, next_ref, mod_ref, g_ref, pw_ref, ps_ref, o_ref, band_scr, *, sub, t_total):
    i = pl.program_id(1)
    nt = pl.num_programs(1)
    hl = POOL_HALO
    tm, d = h_ref.shape[1:]
    gd = d // len(POOL_WINDOWS)
    shift, scale, gate = _mod_rows(mod_ref, 1)

    @pl.when((pl.program_id(0) == 0) & (i == 0))
    def _():
        off = lax.broadcasted_iota(jnp.int32, (sub, sub), 1) - lax.broadcasted_iota(jnp.int32, (sub, sub), 0)
        for gi, win in enumerate(POOL_WINDOWS):
            band_scr[gi] = jnp.where((off >= -(win // 2)) & (off < win // 2), 1.0, 0.0).astype(BF16)

    er = lax.broadcasted_iota(jnp.int32, (2 * hl, 2 * hl), 0)
    hu = lax.broadcasted_iota(jnp.int32, (2 * hl, 2 * hl), 1)
    dist = jnp.where(hu < hl, hu - hl - er, hu - er + hl)
    same_side = (er < hl) == (hu < hl)
    edges = [jnp.where(same_side & (dist >= -(win // 2)) & (dist < win // 2), 1.0, 0.0).astype(BF16)
             for win in POOL_WINDOWS]

    for s0 in range(0, tm, sub):
        first, final = s0 == 0, s0 + sub == tm
        x = h_ref[0, s0:s0 + sub, :]
        above = prev_ref[0] if first else h_ref[0, s0 - hl:s0, :]
        below = next_ref[0] if final else h_ref[0, s0 + sub:s0 + sub + hl, :]
        xe = _ada(jnp.concatenate([above, x, below], axis=0), g_ref[...], shift, scale)
        xn = xe[hl:hl + sub]
        top = jnp.where(i > 0, xe[:hl], 0.0) if first else xe[:hl]
        bot = jnp.where(i < nt - 1, xe[hl + sub:], 0.0) if final else xe[hl + sub:]
        x3 = _split3(xn)
        h3 = _split3(jnp.concatenate([top, bot], axis=0))
        trow = i * tm + s0 + lax.broadcasted_iota(jnp.int32, (sub, 1), 0)
        parts = []
        for gi, win in enumerate(POOL_WINDOWS):
            half = win // 2
            cols = slice(gi * gd, (gi + 1) * gd)
            band, edge = band_scr[gi], edges[gi]
            tot = _dot(band, x3[0][:, cols]) + _dot(band, x3[1][:, cols]) + _dot(band, x3[2][:, cols])
            fix = _dot(edge, h3[0][:, cols]) + _dot(edge, h3[1][:, cols]) + _dot(edge, h3[2][:, cols])
            tot = jnp.concatenate([tot[:hl] + fix[:hl], tot[hl:sub - hl], tot[sub - hl:] + fix[hl:]], axis=0)
            cnt = (jnp.minimum(trow + half, t_total) - jnp.maximum(trow - half, 0)).astype(F32)
            pooled = (tot / cnt - xn[:, cols]).astype(BF16)
            parts.append(_dot(pooled, pw_ref[gi]))
        y = jnp.concatenate(parts, axis=-1) * ps_ref[...]
        o_ref[0, s0:s0 + sub, :] = x + gate * y


def _pool(h, mod, g, pw, ps):
    bsz, t, d = h.shape
    tm = min(POOL_TILE, t)
    sub = min(POOL_SUBTILE, tm)
    per = tm // POOL_HALO
    last = t // POOL_HALO - 1
    return pl.pallas_call(
        functools.partial(_pool_kernel, sub=sub, t_total=t),
        grid=(bsz, t // tm),
        in_specs=[pl.BlockSpec((1, tm, d), lambda b, i: (b, i, 0)),
                  pl.BlockSpec((1, POOL_HALO, d), lambda b, i: (b, jnp.maximum(i * per - 1, 0), 0)),
                  pl.BlockSpec((1, POOL_HALO, d), lambda b, i: (b, jnp.minimum((i + 1) * per, last), 0)),
                  _mod_spec(mod, d),
                  pl.BlockSpec((1, d), lambda b, i: (0, 0)),
                  _resident(pw.shape),
                  pl.BlockSpec((1, d), lambda b, i: (0, 0))],
        out_specs=pl.BlockSpec((1, tm, d), lambda b, i: (b, i, 0)),
        out_shape=jax.ShapeDtypeStruct((bsz, t, d), F32),
        scratch_shapes=[pltpu.VMEM((len(POOL_WINDOWS), sub, sub), BF16)],
        compiler_params=_params("arbitrary", "arbitrary"),
        name="pool",
    )(h, h, h, mod, g.reshape(1, d), pw, ps.reshape(1, d))


def _swa_kernel(sink_ref, q_ref, k_ref, v_ref, kc_ref, vc_ref, o_ref, kt_scr, va_scr, sloc_a, scx_a, sloc_b, scx_b,
                *, t_total):
    kv = pl.program_id(1)
    blk = SWA_BLOCK
    nb = t_total // blk
    grp = SWA_Q_HEADS // SWA_KV_HEADS
    nrow = grp * blk
    _stage_transposed(kt_scr, k_ref, 0, t_total)
    kct = kc_ref[0].T
    _stage_with_ones(va_scr, v_ref)
    vca = _with_ones(vc_ref[0])
    row = lax.broadcasted_iota(jnp.int32, (nrow, 1), 0)
    sink = jnp.zeros((nrow, 1), F32)
    for gi in range(grp):
        sink = jnp.where((row >= gi * blk) & (row < (gi + 1) * blk), sink_ref[kv * grp + gi] * LOG2E, sink)
    qi = lax.broadcasted_iota(jnp.int32, (nrow, blk), 0) & (blk - 1)
    kj = lax.broadcasted_iota(jnp.int32, (nrow, blk), 1)
    open_blk = jnp.zeros((nrow, blk), F32)
    prev_blk = jnp.where(kj >= qi, 0.0, NEG_INF)
    next_blk = jnp.where(kj <= qi, 0.0, NEG_INF)

    shut_blk = jnp.full((nrow, blk), NEG_INF, F32)
    span = 3 * blk
    mask_first = jnp.concatenate([open_blk, next_blk, shut_blk], axis=1)
    mask_mid = jnp.concatenate([prev_blk, open_blk, next_blk], axis=1)
    mask_last = jnp.concatenate([shut_blk, prev_blk, open_blk], axis=1)

    def offsets(n):
        qs = pl.multiple_of(n * blk, blk)
        return qs, pl.multiple_of(jnp.clip(qs - blk, 0, t_total - span), blk)

    def scores(n, mask, sloc_ref, scx_ref):
        qs, ks = offsets(n)
        qb = q_ref[0, pl.ds(qs, blk), :]
        q4 = jnp.concatenate([_split_heads(qb[:, :LANES]), _split_heads(qb[:, LANES:])], axis=0)
        sloc_ref[...] = _dot(q4, kt_scr[:, pl.ds(ks, span)]) + mask
        scx_ref[...] = _dot(q4, kct)

    def finish(n, sloc_ref, scx_ref):
        qs, ks = offsets(n)
        s_loc, s_cx = sloc_ref[...], scx_ref[...]
        m = jnp.maximum(jnp.maximum(jnp.max(s_loc, axis=-1, keepdims=True),
                                    jnp.max(s_cx, axis=-1, keepdims=True)), sink)
        p_loc = jnp.exp2(s_loc - m)
        p_cx = jnp.exp2(s_cx - m)
        oa = _dot(p_loc.astype(BF16), va_scr[pl.ds(ks, span), :]) + _dot(p_cx.astype(BF16), vca)
        o = oa[:, :LANES] / (oa[:, LANES:] + jnp.exp2(sink - m))
        out = jnp.concatenate([_merge_heads(o[:2 * blk]), _merge_heads(o[2 * blk:])], axis=-1)
        o_ref[0, pl.ds(qs, blk), :] = out.astype(BF16)

    slot_a, slot_b = (sloc_a, scx_a), (sloc_b, scx_b)
    scores(0, mask_first, *slot_a)
    scores(1, mask_mid, *slot_b)
    finish(0, *slot_a)

    def body(i, carry):
        n = 1 + 2 * i
        scores(n + 1, mask_mid, *slot_a)
        finish(n, *slot_b)
        scores(n + 2, mask_mid, *slot_b)
        finish(n + 1, *slot_a)
        return carry

    lax.fori_loop(0, (nb - 4) // 2, body, 0, unroll=7)
    scores(nb - 2, mask_mid, *slot_a)
    finish(nb - 3, *slot_b)
    scores(nb - 1, mask_last, *slot_b)
    finish(nb - 2, *slot_a)
    finish(nb - 1, *slot_b)


def _swa_attention(sink, q, k, v, kc, vc):
    bsz, t, d = q.shape
    l = kc.shape[1]
    nb = t // SWA_BLOCK
    assert nb >= 4 and nb % 2 == 0 and SWA_WINDOW == SWA_BLOCK
    qw = d // SWA_KV_HEADS
    grid_spec = pltpu.PrefetchScalarGridSpec(
        num_scalar_prefetch=1,
        grid=(bsz, SWA_KV_HEADS),
        in_specs=[pl.BlockSpec((1, t, qw), lambda b, h, s: (b, 0, h)),
                  pl.BlockSpec((1, t, LANES), lambda b, h, s: (b, 0, h)),
                  pl.BlockSpec((1, t, LANES), lambda b, h, s: (b, 0, h)),
                  pl.BlockSpec((1, l, LANES), lambda b, h, s: (b, 0, h)),
                  pl.BlockSpec((1, l, LANES), lambda b, h, s: (b, 0, h))],
        out_specs=pl.BlockSpec((1, t, qw), lambda b, h, s: (b, 0, h)),
        scratch_shapes=[pltpu.VMEM((LANES, t), BF16), pltpu.VMEM((t, 2 * LANES), BF16)]
        + [pltpu.VMEM((SWA_Q_HEADS // SWA_KV_HEADS * SWA_BLOCK, n), F32) for n in (3 * SWA_BLOCK, l)] * 2,
    )
    return pl.pallas_call(
        functools.partial(_swa_kernel, t_total=t),
        grid_spec=grid_spec,
        out_shape=jax.ShapeDtypeStruct((bsz, t, d), BF16),
        compiler_params=_params("parallel", "parallel"),
        name="swa_attention",
    )(sink, q, k, v, kc, vc)


def _rope_angles(positions, dim):
    seg = dim // len(positions)
    inv = ROPE_BASE ** (-np.arange(0, seg, 2, dtype=np.float64) / seg)
    return np.concatenate([np.tile(p.astype(np.float64)[:, None] * inv, (1, 2)) for p in positions], axis=-1)


def _table(x):
    return jnp.asarray(x.astype(np.float32))


def _dup_heads(w, heads, dh):
    d = w.shape[0]
    return jnp.broadcast_to(w.reshape(d, heads, 1, dh), (d, heads, 2, dh)).reshape(d, heads * 2 * dh)


def kernel(x, c, ctx, c_ctx, w_mod, b_mod, norm_g, ffn_w_in, ffn_w_out, ret_w_in, ret_w_out, ret_gn_g, ret_decay_f, ret_decay_b, nat_w_qkv, nat_w_o, nat_rpb, pool_w, pool_scale, swa_w_qkv, swa_w_o, swa_sink, final_norm_g):
    bsz, t, d = x.shape
    depth = w_mod.shape[0]
    cc = jnp.concatenate([c, c_ctx[None], jnp.zeros((SUBLANES - bsz - 1, d), F32)], axis=0)
    mods = _modulation(cc, w_mod, b_mod)
    h, hc = x, ctx
    for i in range(depth):
        kind, occ = i % N_MIXERS, i // N_MIXERS
        last = i == depth - 1
        ctx_live = (not last) or kind != 2
        ml = mods[i, :bsz].reshape(bsz, MOD_ROWS, d)
        mc = mods[i, bsz:bsz + 1].reshape(1, MOD_ROWS, d)
        if ctx_live:
            hc, w_in, w_out = _ffn_ctx(hc, mc, norm_g[i, 0], ffn_w_in, ffn_w_out, i, 0)
        else:
            w_in, w_out = _ffn_weights_bf16(ffn_w_in, ffn_w_out, i, 0)
        h = _ffn(h, ml, norm_g[i, 0], w_in, w_out, 0)
        g1 = norm_g[i, 1]
        yl = yc = None
        if kind == 0:
            assert not last, "retention as the last layer is not wired up"
            w = ret_w_in[occ].astype(BF16)
            wo = ret_w_out[occ].astype(BF16)
            ang = _rope_angles([np.arange(t)], RET_QK_DIM)[:, :RET_QK_DIM // 2]
            dec_f = jnp.broadcast_to(ret_decay_f[occ].astype(F32)[:, None, None], (RET_HEADS, 1, LANES))
            dec_b = jnp.broadcast_to(ret_decay_b[occ].astype(F32)[:, None, None], (RET_HEADS, 1, LANES))
            qc, kc, vc, gc = _ret_proj(hc, mc, g1, w)
            zeros = jnp.zeros((bsz, RET_HEADS, RET_QK_DIM, RET_V_DIM), F32)
            oc, s_f, s_b = _ret_mix(dec_f, dec_b, qc, kc, vc, gc, ret_gn_g[occ], zeros, zeros)
            ql, kl, vl, gl = _ret_proj(h, ml, g1, w, _table(np.cos(ang)), _table(np.sin(ang)))
            ol, _, _ = _ret_mix(dec_f, dec_b, ql, kl, vl, gl, ret_gn_g[occ], s_f, s_b)
            yl, yc = (ol, wo), (oc, wo)
        elif kind == 1:
            w = nat_w_qkv[occ].astype(BF16)
            wo = nat_w_o[occ].astype(BF16)
            widths = (d, d, d)
            qc, kc, vc = _qkv_proj(hc, mc, g1, w, widths, NAT_HEAD_DIM ** -0.5 * LOG2E, want_q=not last)
            ql, kl, vl = _qkv_proj(h, ml, g1, w, widths, NAT_HEAD_DIM ** -0.5 * LOG2E)
            yl = (_nat_attention(ql, kl, vl, kc, vc, _nat_bias(nat_rpb[occ])), wo)
            if not last:
                yc = (_ctx_attention(qc, kc, vc), wo)
        elif kind == 2:
            pw = pool_w[occ].astype(BF16)
            h_new = _pool(h, ml, g1, pw, pool_scale[occ])
            if not last:
                hc = _pool(hc, mc, g1, pw, pool_scale[occ])
            h = h_new
        else:
            assert last, "windowed attention with live context outputs is not wired up"
            nq = SWA_Q_HEADS * SWA_HEAD_DIM
            nkv = SWA_KV_HEADS * SWA_HEAD_DIM
            wq = swa_w_qkv[occ][:, :nq]
            wk = _dup_heads(swa_w_qkv[occ][:, nq:nq + nkv], SWA_KV_HEADS, SWA_HEAD_DIM)
            wv = _dup_heads(swa_w_qkv[occ][:, nq + nkv:], SWA_KV_HEADS, SWA_HEAD_DIM)
            w = jnp.concatenate([wq, wk, wv], axis=-1).astype(BF16)
            wo = swa_w_o[occ].astype(BF16)
            widths = (nq, 2 * nkv, 2 * nkv)
            tt = np.arange(t)
            ang = _rope_angles([tt // GRID_W, tt % GRID_W], SWA_HEAD_DIM)
            ang = np.tile(ang, (1, LANES // SWA_HEAD_DIM))
            lane = np.arange(LANES)
            sin = np.where(lane % 32 < 16, -np.sin(ang), np.sin(ang))
            _, kc, vc = _qkv_proj(hc, mc, g1, w, widths, SWA_HEAD_DIM ** -0.5 * LOG2E, want_q=False)
            ql, kl, vl = _qkv_proj(h, ml, g1, w, widths, SWA_HEAD_DIM ** -0.5 * LOG2E, _table(np.cos(ang)), _table(sin))
            yl = (_swa_attention(swa_sink[occ].astype(F32), ql, kl, vl, kc, vc), wo)
        if not last:
            hc, w_in, w_out = _ffn_ctx(hc, mc, norm_g[i, 2], ffn_w_in, ffn_w_out, i, 1, yc)
        else:
            w_in, w_out = _ffn_weights_bf16(ffn_w_in, ffn_w_out, i, 1)
        h = _ffn(h, ml, norm_g[i, 2], w_in, w_out, 1, yl, final_norm_g if last else None)
    return h
```
